```python
import functools
import jax, jax.numpy as jnp
from jax import lax
import numpy as np

D_MODEL = 1024
BATCH = 8
SEQ = 4096
DEPTH = 4
DEC_BATCH = 16
DEC_SEQ = 64
PAST_LEN = 1024

CHUNK = 64
QBLOCK = 128
ROPE_THETA = 500000.0
EPS = 1e-6
HEAD_DIM = 64
ROT_DIM = HEAD_DIM // 4
A_HEADS = 8
A_KV_HEADS = 2
IDX_HEADS = 8
IDX_DIM = 32
IDX_ROT = IDX_DIM // 4
TOPK_MAX = 256
B_HEADS = 8
Q_RANK = 256
KV_RANK = 128
B_NOPE = 64
B_ROPE = 32
B_VDIM = 64
C_HEADS = 8
D_HEADS = 8
D_FF = 2816
CONV_W = 3

N_EVEN = (DEPTH + 1) // 2
N_ODD = DEPTH // 2
EVEN_SPLIT = [A_HEADS * HEAD_DIM, A_KV_HEADS * HEAD_DIM, A_KV_HEADS * HEAD_DIM,
              IDX_HEADS * IDX_DIM, IDX_DIM, IDX_HEADS, Q_RANK, KV_RANK, B_ROPE]
ODD_SPLIT = [C_HEADS * HEAD_DIM] * 3 + [C_HEADS] + [D_HEADS * HEAD_DIM] * 3
EVEN_IN = sum(EVEN_SPLIT)
ODD_IN = sum(ODD_SPLIT)
EVEN_MIX = A_HEADS * HEAD_DIM + B_HEADS * B_VDIM
ODD_MIX = (C_HEADS + D_HEADS) * HEAD_DIM

kernel_name = 'hybrid_dsa_mla_fox_stickbreak_convffn_step'


def rmsnorm(x, g):
    xf = x.astype(jnp.float32)
    y = xf * lax.rsqrt(jnp.mean(xf * xf, axis=-1, keepdims=True) + EPS)
    return (y * g.astype(jnp.float32)).astype(x.dtype)


def rope(x, pos, n_rot):
    half = n_rot // 2
    inv = ROPE_THETA ** (-jnp.arange(half, dtype=jnp.float32) * 2.0 / n_rot)
    ang = pos.astype(jnp.float32)[:, None] * inv[None, :]
    shape = (ang.shape[0],) + (1,) * (x.ndim - 3) + (half,)
    cos = jnp.cos(ang).reshape(shape).astype(x.dtype)
    sin = jnp.sin(ang).reshape(shape).astype(x.dtype)
    x1 = x[..., :half]
    x2 = x[..., half:n_rot]
    return jnp.concatenate([x1 * cos - x2 * sin, x1 * sin + x2 * cos, x[..., n_rot:]], axis=-1)


def chunk_mask(q_pos, k_pos):
    return (k_pos[None, :] // CHUNK) <= (q_pos[:, None] // CHUNK)


def run_queries(fn, q_arrays, q_pos, blocked):
    if not blocked:
        return fn(*q_arrays, q_pos)
    n_blk = q_pos.shape[0] // QBLOCK

    def split(a):
        return jnp.moveaxis(a.reshape((a.shape[0], n_blk, QBLOCK) + a.shape[2:]), 1, 0)

    blocks = tuple(split(a) for a in q_arrays)
    out = lax.map(lambda args: fn(*args[0], args[1]), (blocks, q_pos.reshape(n_blk, QBLOCK)))
    out = jnp.moveaxis(out, 0, 1)
    return out.reshape((out.shape[0], n_blk * QBLOCK) + out.shape[3:])


def dsa_core(k, v, ki, k_pos, topk, q, qi, wi, q_pos):
    sc = jnp.einsum('bqhd,bsd->bqhs', qi, ki).astype(jnp.float32)
    idx_score = jnp.einsum('bqhs,bqh->bqs', jax.nn.relu(sc), wi.astype(jnp.float32))
    idx_score = jnp.where(chunk_mask(q_pos, k_pos)[None], idx_score, -jnp.inf)
    _, sel = lax.top_k(idx_score, topk)
    valid = (k_pos[sel] // CHUNK) <= (q_pos // CHUNK)[None, :, None]
    gather = jax.vmap(lambda a, i: a[i])
    kg = gather(k, sel)
    vg = gather(v, sel)
    b_, q_ = q.shape[0], q.shape[1]
    qg = q.reshape(b_, q_, A_KV_HEADS, A_HEADS // A_KV_HEADS, HEAD_DIM)
    logits = jnp.einsum('bqgrd,bqkgd->bqgrk', qg, kg).astype(jnp.float32) * HEAD_DIM ** -0.5
    logits = jnp.where(valid[:, :, None, None, :], logits, -jnp.inf)
    p = jax.nn.softmax(logits, axis=-1).astype(v.dtype)
    out = jnp.einsum('bqgrk,bqkgd->bqgrd', p, vg)
    return out.reshape(b_, q_, A_HEADS * HEAD_DIM)


def mla_core(k_nope, k_rope, v, k_pos, q_nope, q_rope, q_pos):
    logits = (jnp.einsum('bqhd,bshd->bhqs', q_nope, k_nope)
              + jnp.einsum('bqhd,bsd->bhqs', q_rope, k_rope)).astype(jnp.float32)
    logits = logits * (B_NOPE + B_ROPE) ** -0.5
    logits = jnp.where(chunk_mask(q_pos, k_pos)[None, None], logits, -jnp.inf)
    p = jax.nn.softmax(logits, axis=-1).astype(v.dtype)
    out = jnp.einsum('bhqs,bshd->bqhd', p, v)
    return out.reshape(out.shape[0], out.shape[1], B_HEADS * B_VDIM)


def fox_core(k, v, ck, k_pos, q, cq, q_pos):
    logits = jnp.einsum('bqhd,bshd->bhqs', q, k).astype(jnp.float32) * HEAD_DIM ** -0.5
    logits = logits + jnp.transpose(cq, (0, 2, 1))[..., None] - jnp.transpose(ck, (0, 2, 1))[:, :, None, :]
    mask = k_pos[None, :] <= q_pos[:, None]
    logits = jnp.where(mask[None, None], logits, -jnp.inf)
    p = jax.nn.softmax(logits, axis=-1).astype(v.dtype)
    out = jnp.einsum('bhqs,bshd->bqhd', p, v)
    return out.reshape(out.shape[0], out.shape[1], C_HEADS * HEAD_DIM)


def sb_core(k, v, k_pos, q, q_pos):
    z = jnp.einsum('bqhd,bshd->bhqs', q, k).astype(jnp.float32) * HEAD_DIM ** -0.5
    mask = (k_pos[None, :] < q_pos[:, None])[None, None]
    log_beta = jax.nn.log_sigmoid(z)
    log_1m = jnp.where(mask, jax.nn.log_sigmoid(-z), 0.0)
    after = lax.cumsum(log_1m, axis=3, reverse=True) - log_1m
    a = jnp.where(mask, jnp.exp(log_beta + after), 0.0).astype(v.dtype)
    out = jnp.einsum('bhqs,bshd->bqhd', a, v)
    return out.reshape(out.shape[0], out.shape[1], D_HEADS * HEAD_DIM)


def even_mixer(h, q_pos, past, w_in, g_bq, g_bkv, w_uq, w_ukv, w_o):
    bsz, s_len, _ = h.shape
    cuts = np.cumsum(EVEN_SPLIT)[:-1].tolist()
    qa, ka, va, qi, ki, wi, cq, ckv, kr = jnp.split(h @ w_in, cuts, axis=-1)
    qa = rope(qa.reshape(bsz, s_len, A_HEADS, HEAD_DIM), q_pos, ROT_DIM)
    ka = rope(ka.reshape(bsz, s_len, A_KV_HEADS, HEAD_DIM), q_pos, ROT_DIM)
    va = va.reshape(bsz, s_len, A_KV_HEADS, HEAD_DIM)
    qi = rope(qi.reshape(bsz, s_len, IDX_HEADS, IDX_DIM), q_pos, IDX_ROT) * IDX_DIM ** -0.5
    ki = rope(ki, q_pos, IDX_ROT)
    wi = wi * IDX_HEADS ** -0.5
    qb = (rmsnorm(cq, g_bq) @ w_uq).reshape(bsz, s_len, B_HEADS, B_NOPE + B_ROPE)
    qn = qb[..., :B_NOPE]
    qr = rope(qb[..., B_NOPE:], q_pos, B_ROPE)
    lat = rmsnorm(ckv, g_bkv)
    kr = rope(kr, q_pos, B_ROPE)
    new_rows = (ka, va, ki, lat, kr)
    if past is None:
        keys = new_rows
        k_pos = q_pos
    else:
        keys = tuple(jnp.concatenate([p_, n_], axis=1) for p_, n_ in zip(past, new_rows))
        k_pos = jnp.arange(keys[0].shape[1], dtype=jnp.int32)
    k_a, v_a, k_i, lat_all, kr_all = keys
    n_keys = k_a.shape[1]
    kv = (lat_all @ w_ukv).reshape(bsz, n_keys, B_HEADS, B_NOPE + B_VDIM)
    kn, vb = kv[..., :B_NOPE], kv[..., B_NOPE:]
    topk = min(TOPK_MAX, n_keys // 4)
    blocked = past is None
    out_a = run_queries(functools.partial(dsa_core, k_a, v_a, k_i, k_pos, topk), (qa, qi, wi), q_pos, blocked)
    out_b = run_queries(functools.partial(mla_core, kn, kr_all, vb, k_pos), (qn, qr), q_pos, blocked)
    return jnp.concatenate([out_a, out_b], axis=-1) @ w_o, new_rows


def odd_mixer(h, q_pos, past, w_in, b_f, w_o):
    bsz, s_len, _ = h.shape
    cuts = np.cumsum(ODD_SPLIT)[:-1].tolist()
    qc, kc, vc, fc, qd, kd, vd = jnp.split(h @ w_in, cuts, axis=-1)
    heads = lambda a, n: a.reshape(bsz, s_len, n, HEAD_DIM)
    qc, kc, vc = heads(qc, C_HEADS), heads(kc, C_HEADS), heads(vc, C_HEADS)
    qd, kd, vd = heads(qd, D_HEADS), heads(kd, D_HEADS), heads(vd, D_HEADS)
    logf = jax.nn.log_sigmoid((fc + b_f).astype(jnp.float32))
    new_rows = (kc, vc, logf, kd, vd)
    if past is None:
        keys = new_rows
        k_pos = q_pos
    else:
        keys = tuple(jnp.concatenate([p_, n_], axis=1) for p_, n_ in zip(past, new_rows))
        k_pos = jnp.arange(keys[0].shape[1], dtype=jnp.int32)
    k_c, v_c, logf_all, k_d, v_d = keys
    c_all = jnp.cumsum(logf_all.astype(jnp.float32), axis=1)
    c_q = c_all[:, c_all.shape[1] - s_len:]
    blocked = past is None
    out_c = run_queries(functools.partial(fox_core, k_c, v_c, c_all, k_pos), (qc, c_q), q_pos, blocked)
    out_d = run_queries(functools.partial(sb_core, k_d, v_d, k_pos), (qd,), q_pos, blocked)
    return jnp.concatenate([out_c, out_d], axis=-1) @ w_o, new_rows


def conv_ffn(h, buf, w_up, w_conv, b_conv, w_down):
    gate, up = jnp.split(h @ w_up, 2, axis=-1)
    s_len = h.shape[1]
    if buf is None:
        buf = jnp.zeros((h.shape[0], CONV_W - 1, D_FF), gate.dtype)
    ext = jnp.concatenate([buf.astype(gate.dtype), gate], axis=1)
    gc = b_conv + ext[:, 0:s_len] * w_conv[0]
    for i in range(1, CONV_W):
        gc = gc + ext[:, i:i + s_len] * w_conv[i]
    y = (jax.nn.silu(gc) * up) @ w_down
    return y, ext[:, ext.shape[1] - (CONV_W - 1):]


def trunk(x, q_pos, caches, params):
    (g_mix, g_ffn, g_final, w_in_even, g_b_q, g_b_kv, w_b_uq, w_b_ukv, w_o_even,
     w_in_odd, b_forget, w_o_odd, w_up, w_conv, b_conv, w_down) = params
    even_rows = [[] for _ in range(5)]
    odd_rows = [[] for _ in range(5)]
    conv_rows = []
    for l in range(DEPTH):
        h = rmsnorm(x, g_mix[l])
        j = l // 2
        if l % 2 == 0:
            past = None if caches is None else tuple(c[j] for c in caches[0:5])
            y, rows = even_mixer(h, q_pos, past, w_in_even[j], g_b_q[j], g_b_kv[j], w_b_uq[j], w_b_ukv[j], w_o_even[j])
            for lst, r in zip(even_rows, rows):
                lst.append(r)
        else:
            past = None if caches is None else tuple(c[j] for c in caches[5:10])
            y, rows = odd_mixer(h, q_pos, past, w_in_odd[j], b_forget[j], w_o_odd[j])
            for lst, r in zip(odd_rows, rows):
                lst.append(r)
        x = x + y
        buf = None if caches is None else caches[10][l]
        y, new_buf = conv_ffn(rmsnorm(x, g_ffn[l]), buf, w_up[l], w_conv[l], b_conv[l], w_down[l])
        conv_rows.append(new_buf)
        x = x + y
    out = rmsnorm(x, g_final)
    states = [jnp.stack(r, axis=0) for r in even_rows + odd_rows] + [jnp.stack(conv_rows, axis=0)]
    return out, states


def setup_inputs(seed: int = 0) -> dict:
    key = jax.random.key(seed)
    ks = iter(jax.random.split(key, 40))

    def nrm(shape, scale=1.0):
        return jax.random.normal(next(ks), shape, jnp.float32) * scale

    db, pl = DEC_BATCH, PAST_LEN
    return {
        'x_prompt': nrm((BATCH, SEQ, D_MODEL)),
        'x_sample': nrm((DEC_BATCH, DEC_SEQ, D_MODEL)),
        'cache_a_k': nrm((N_EVEN, db, pl, A_KV_HEADS, HEAD_DIM)),
        'cache_a_v': nrm((N_EVEN, db, pl, A_KV_HEADS, HEAD_DIM)),
        'cache_a_idx_k': nrm((N_EVEN, db, pl, IDX_DIM)),
        'cache_b_latent': nrm((N_EVEN, db, pl, KV_RANK)),
        'cache_b_rope': nrm((N_EVEN, db, pl, B_ROPE)),
        'cache_c_k': nrm((N_ODD, db, pl, C_HEADS, HEAD_DIM)),
        'cache_c_v': nrm((N_ODD, db, pl, C_HEADS, HEAD_DIM)),
        'cache_c_logf': jax.nn.log_sigmoid(nrm((N_ODD, db, pl, C_HEADS))),
        'cache_d_k': nrm((N_ODD, db, pl, D_HEADS, HEAD_DIM)),
        'cache_d_v': nrm((N_ODD, db, pl, D_HEADS, HEAD_DIM)),
        'state_ffn_conv': nrm((DEPTH, db, CONV_W - 1, D_FF)),
        'g_mix': 1.0 + nrm((DEPTH, D_MODEL), 0.01),
        'g_ffn': 1.0 + nrm((DEPTH, D_MODEL), 0.01),
        'g_final': 1.0 + nrm((D_MODEL,), 0.01),
        'w_in_even': nrm((N_EVEN, D_MODEL, EVEN_IN), D_MODEL ** -0.5),
        'g_b_q': 1.0 + nrm((N_EVEN, Q_RANK), 0.01),
        'g_b_kv': 1.0 + nrm((N_EVEN, KV_RANK), 0.01),
        'w_b_uq': nrm((N_EVEN, Q_RANK, B_HEADS * (B_NOPE + B_ROPE)), Q_RANK ** -0.5),
        'w_b_ukv': nrm((N_EVEN, KV_RANK, B_HEADS * (B_NOPE + B_VDIM)), KV_RANK ** -0.5),
        'w_o_even': nrm((N_EVEN, EVEN_MIX, D_MODEL), EVEN_MIX ** -0.5),
        'w_in_odd': nrm((N_ODD, D_MODEL, ODD_IN), D_MODEL ** -0.5),
        'b_forget': nrm((N_ODD, C_HEADS), 0.1),
        'w_o_odd': nrm((N_ODD, ODD_MIX, D_MODEL), ODD_MIX ** -0.5),
        'w_up': nrm((DEPTH, D_MODEL, 2 * D_FF), D_MODEL ** -0.5),
        'w_conv': nrm((DEPTH, CONV_W, D_FF), CONV_W ** -0.5),
        'b_conv': nrm((DEPTH, D_FF), 0.01),
        'w_down': nrm((DEPTH, D_FF, D_MODEL), D_FF ** -0.5),
    }


def reference(x_prompt, x_sample, cache_a_k, cache_a_v, cache_a_idx_k, cache_b_latent, cache_b_rope,
              cache_c_k, cache_c_v, cache_c_logf, cache_d_k, cache_d_v, state_ffn_conv,
              g_mix, g_ffn, g_final, w_in_even, g_b_q, g_b_kv, w_b_uq, w_b_ukv, w_o_even,
              w_in_odd, b_forget, w_o_odd, w_up, w_conv, b_conv, w_down):
    params = (g_mix, g_ffn, g_final, w_in_even, g_b_q, g_b_kv, w_b_uq, w_b_ukv, w_o_even,
              w_in_odd, b_forget, w_o_odd, w_up, w_conv, b_conv, w_down)
    caches = (cache_a_k, cache_a_v, cache_a_idx_k, cache_b_latent, cache_b_rope,
              cache_c_k, cache_c_v, cache_c_logf, cache_d_k, cache_d_v, state_ffn_conv)
    past_len = cache_a_k.shape[2]
    pos_prompt = jnp.arange(x_prompt.shape[1], dtype=jnp.int32)
    pos_sample = past_len + jnp.arange(x_sample.shape[1], dtype=jnp.int32)
    y_prompt, p_states = trunk(x_prompt, pos_prompt, None, params)
    (p_a_k, p_a_v, p_a_idx_k, p_b_latent, p_b_rope,
     p_c_k, p_c_v, p_c_logf, p_d_k, p_d_v, p_conv) = p_states
    y_sample, s_states = trunk(x_sample, pos_sample, caches, params)
    (s_a_k, s_a_v, s_a_idx_k, s_b_latent, s_b_rope,
     s_c_k, s_c_v, s_c_logf, s_d_k, s_d_v, s_conv) = s_states
    return (y_prompt, y_sample,
            p_a_k, p_a_v, p_a_idx_k, p_b_latent, p_b_rope, p_c_k, p_c_v, p_c_logf, p_d_k, p_d_v, p_conv,
            s_a_k, s_a_v, s_a_idx_k, s_b_latent, s_b_rope, s_c_k, s_c_v, s_c_logf, s_d_k, s_d_v, s_conv)
```

```python
import functools

import jax
import jax.numpy as jnp
import numpy as np
from jax import lax
from jax.experimental import pallas as pl
from jax.experimental.pallas import tpu as pltpu

F32 = jnp.float32
BF16 = jnp.bfloat16
I32 = jnp.int32

CHUNK = 64
ROPE_THETA = 500000.0
EPS = 1e-6
HEAD_DIM = 64
ROT_DIM = HEAD_DIM // 4
A_HEADS = 8
A_KV_HEADS = 2
IDX_HEADS = 8
IDX_DIM = 32
IDX_ROT = IDX_DIM // 4
TOPK_MAX = 256
B_HEADS = 8
Q_RANK = 256
KV_RANK = 128
B_NOPE = 64
B_ROPE = 32
B_VDIM = 64
C_HEADS = 8
D_HEADS = 8
D_FF = 2816
CONV_W = 3

EVEN_SPLIT = [A_HEADS * HEAD_DIM, A_KV_HEADS * HEAD_DIM, A_KV_HEADS * HEAD_DIM,
              IDX_HEADS * IDX_DIM, IDX_DIM, IDX_HEADS, Q_RANK, KV_RANK, B_ROPE]
ODD_SPLIT = [C_HEADS * HEAD_DIM] * 3 + [C_HEADS] + [D_HEADS * HEAD_DIM] * 3

LANES = 128
VMEM_LIMIT_BYTES = 56 * 1024 * 1024
MASKED = -1e30
INT_MIN = -2 ** 31
LOG2E = 1.4426950408889634
F32_EXP2_UNDERFLOW = -152.0

_NT = (((1,), (1,)), ((), ()))


def _params(*sem):
    return pltpu.CompilerParams(dimension_semantics=sem, vmem_limit_bytes=VMEM_LIMIT_BYTES)


def _round_up(n, m):
    return (n + m - 1) // m * m


def _chunk_of(pos):
    return jnp.right_shift(pos, CHUNK.bit_length() - 1)


def _proj_kernel(x_ref, g_ref, w_ref, *out_refs, norm, emit_h):
    x = x_ref[...]
    if norm:
        ms = jnp.mean(x * x, axis=-1, keepdims=True)
        x = x * lax.rsqrt(ms + EPS) * g_ref[...]
    out_refs[0][...] = jnp.dot(x.astype(BF16), w_ref[...], preferred_element_type=F32)
    if emit_h:
        out_refs[1][...] = x


def _proj(x, g, w, *, norm=True, emit_h=False, tm=512):
    m, k = x.shape
    n = w.shape[1]
    tm = min(tm, m)
    assert m % tm == 0
    out_shape = [jax.ShapeDtypeStruct((m, n), F32)]
    out_specs = [pl.BlockSpec((tm, n), lambda i: (i, 0))]
    if emit_h:
        out_shape.append(jax.ShapeDtypeStruct((m, k), F32))
        out_specs.append(pl.BlockSpec((tm, k), lambda i: (i, 0)))
    res = pl.pallas_call(
        functools.partial(_proj_kernel, norm=norm, emit_h=emit_h),
        grid=(m // tm,),
        in_specs=[pl.BlockSpec((tm, k), lambda i: (i, 0)),
                  pl.BlockSpec((1, k), lambda i: (0, 0)),
                  pl.BlockSpec((k, n), lambda i: (0, 0))],
        out_specs=out_specs,
        out_shape=out_shape,
        compiler_params=_params("parallel"),
        name="proj",
    )(x, g.reshape(1, k).astype(F32), w.astype(BF16))
    return res if emit_h else res[0]


def _matmul_residual_kernel(a_ref, w_ref, r_ref, o_ref):
    o_ref[...] = r_ref[...] + jnp.dot(a_ref[...], w_ref[...], preferred_element_type=F32)


def _matmul_residual(a, w, res, *, tm=512):
    m, k = a.shape
    n = w.shape[1]
    tm = min(tm, m)
    assert m % tm == 0
    return pl.pallas_call(
        _matmul_residual_kernel,
        grid=(m // tm,),
        in_specs=[pl.BlockSpec((tm, k), lambda i: (i, 0)),
                  pl.BlockSpec((k, n), lambda i: (0, 0)),
                  pl.BlockSpec((tm, n), lambda i: (i, 0))],
        out_specs=pl.BlockSpec((tm, n), lambda i: (i, 0)),
        out_shape=jax.ShapeDtypeStruct((m, n), F32),
        compiler_params=_params("parallel"),
        name="matmul_residual",
    )(a, w.astype(BF16), res)


def _conv_ffn_kernel(x_ref, g_ref, wg_ref, wu_ref, wc_ref, bc_ref, wd_ref, st_ref,
                     y_ref, ns_ref, h_s, acc_s, carry_s, *, tiles_per_seq, tm, tf):
    i = pl.program_id(0)
    c = pl.program_id(1)
    nc = pl.num_programs(1)
    cols = pl.ds(pl.multiple_of(c * tf, tf), tf)

    @pl.when(c == 0)
    def _():
        x = x_ref[...]
        ms = jnp.mean(x * x, axis=-1, keepdims=True)
        h_s[...] = (x * lax.rsqrt(ms + EPS) * g_ref[...]).astype(BF16)
        acc_s[...] = jnp.zeros_like(acc_s)

    h = h_s[...]
    gate = jnp.dot(h, wg_ref[...], preferred_element_type=F32)
    up = jnp.dot(h, wu_ref[...], preferred_element_type=F32)

    @pl.when(i % tiles_per_seq == 0)
    def _():
        carry_s[c] = st_ref[0, :, cols]

    prev = carry_s[c]
    row = lax.broadcasted_iota(I32, gate.shape, 0)
    g1 = pltpu.roll(gate, 1, 0)
    g1 = jnp.where(row == 0, prev[1:2, :], g1)
    g2 = pltpu.roll(gate, 2, 0)
    g2 = jnp.where(row == 0, prev[0:1, :], jnp.where(row == 1, prev[1:2, :], g2))
    wc = wc_ref[...]
    gc = bc_ref[...] + g2 * wc[0:1, :]
    gc = gc + g1 * wc[1:2, :]
    gc = gc + gate * wc[2:3, :]
    act = gc * jax.nn.sigmoid(gc) * up
    acc_s[...] += jnp.dot(act.astype(BF16), wd_ref[...], preferred_element_type=F32)

    tail = gate[tm - 2:tm, :]
    carry_s[c] = tail
    ns_ref[0, :, cols] = tail

    @pl.when(c == nc - 1)
    def _():
        y_ref[...] = x_ref[...] + acc_s[...]


def _conv_ffn(x, g, w_up, w_conv, b_conv, w_down, state, *, seq_len, tm, tf=256):
    m, d = x.shape
    nseq = m // seq_len
    tm = min(tm, seq_len)
    assert seq_len % tm == 0 and D_FF % tf == 0
    tiles_per_seq = seq_len // tm
    nc = D_FF // tf
    wg = w_up[:, :D_FF].astype(BF16)
    wu = w_up[:, D_FF:].astype(BF16)
    state_spec = pl.BlockSpec((1, CONV_W - 1, D_FF), lambda i, c: (i // tiles_per_seq, 0, 0))
    y, ns = pl.pallas_call(
        functools.partial(_conv_ffn_kernel, tiles_per_seq=tiles_per_seq, tm=tm, tf=tf),
        grid=(m // tm, nc),
        in_specs=[pl.BlockSpec((tm, d), lambda i, c: (i, 0)),
                  pl.BlockSpec((1, d), lambda i, c: (0, 0)),
                  pl.BlockSpec((d, tf), lambda i, c: (0, c)),
                  pl.BlockSpec((d, tf), lambda i, c: (0, c)),
                  pl.BlockSpec((CONV_W, tf), lambda i, c: (0, c)),
                  pl.BlockSpec((1, tf), lambda i, c: (0, c)),
                  pl.BlockSpec((tf, d), lambda i, c: (c, 0)),
                  state_spec],
        out_specs=[pl.BlockSpec((tm, d), lambda i, c: (i, 0)), state_spec],
        out_shape=[jax.ShapeDtypeStruct((m, d), F32),
                   jax.ShapeDtypeStruct((nseq, CONV_W - 1, D_FF), F32)],
        scratch_shapes=[pltpu.VMEM((tm, d), BF16),
                        pltpu.VMEM((tm, d), F32),
                        pltpu.VMEM((nc, CONV_W - 1, tf), F32)],
        compiler_params=_params("arbitrary", "arbitrary"),
        name="conv_ffn",
    )(x, g.reshape(1, d).astype(F32), wg, wu, w_conv.astype(F32), b_conv.reshape(1, D_FF).astype(F32),
      w_down.astype(BF16), state.astype(F32))
    return y, ns


def _attn_call(kernel, inputs, in_specs, out_cols, *, bsz, sq, tq, scratch=(), name):
    return pl.pallas_call(
        kernel,
        grid=(bsz, sq // tq),
        in_specs=in_specs,
        out_specs=pl.BlockSpec((1, tq, out_cols), lambda b, i: (b, i, 0)),
        out_shape=jax.ShapeDtypeStruct((bsz, sq, out_cols), BF16),
        scratch_shapes=list(scratch),
        compiler_params=_params("parallel", "arbitrary"),
        name=name,
    )(*inputs)


def _q_spec(tq, cols):
    return pl.BlockSpec((1, tq, cols), lambda b, i: (b, i, 0))


def _k_spec(rows, cols):
    return pl.BlockSpec((1, rows, cols), lambda b, i: (b, 0, 0))


def _tile_geometry(i, *, tq, span, q_off):
    q_lo = q_off + i * tq
    return q_lo, q_lo // span


def _lane_tiles(x):
    return [x[:, j * LANES:(j + 1) * LANES] for j in range(x.shape[1] // LANES)]


def _visible(kind, q_lo, ks, *, tq, tk, sk):
    q_pos = q_lo + lax.broadcasted_iota(I32, (tq, 1), 0)
    k_pos = ks + lax.broadcasted_iota(I32, (1, tk), 1)
    if kind == "chunk":
        return (_chunk_of(k_pos) <= _chunk_of(q_pos)) & (k_pos < sk)
    if kind == "causal":
        return k_pos <= q_pos
    assert kind == "strict"
    return k_pos < q_pos


def _span_blocks(g, *, tk, group):
    return [pl.multiple_of((g * group + u) * tk, tk) for u in range(group)]


def _softmax_head(q, k_ref, v_ref, s_scr, kcol, vcol, *, tq, tk, group, g_own, own_mask, bias_ref=None):
    def logits(ks):
        k = k_ref[0, pl.ds(ks, tk), kcol:kcol + LANES]
        s = lax.dot_general(q, k, _NT, preferred_element_type=F32)
        if bias_ref is not None:
            s = s + bias_ref[:, pl.ds(ks, tk)]
        return s

    def keep(ks, s, mx):
        s_scr[:, pl.ds(ks, tk)] = s
        for t in _lane_tiles(s):
            mx = jnp.maximum(mx, t)
        return mx

    def full_span(g, mx):
        for ks in _span_blocks(g, tk=tk, group=group):
            mx = keep(ks, logits(ks), mx)
        return mx

    mx = jnp.full((tq, LANES), MASKED, F32)
    if own_mask is None:
        mx = lax.fori_loop(0, g_own + 1, full_span, mx)
    else:
        mx = lax.fori_loop(0, g_own, full_span, mx)
        for ks in _span_blocks(g_own, tk=tk, group=group):
            mx = keep(ks, jnp.where(own_mask(ks), logits(ks), MASKED), mx)
    m = jnp.broadcast_to(jnp.max(mx, axis=1, keepdims=True), (tq, LANES))

    def weigh_span(g, acc):
        for ks in _span_blocks(g, tk=tk, group=group):
            s = s_scr[:, pl.ds(ks, tk)]
            p = jnp.concatenate([jnp.exp2(t - m) for t in _lane_tiles(s)], axis=1)
            v = v_ref[0, pl.ds(ks, tk), vcol:vcol + LANES]
            acc = acc + jnp.dot(p.astype(BF16), v, preferred_element_type=F32)
        return acc

    acc = lax.fori_loop(0, g_own + 1, weigh_span, jnp.zeros((tq, LANES), F32))
    return acc[:, :HEAD_DIM] / acc[:, HEAD_DIM:HEAD_DIM + 1]


def _softmax_attn_kernel(q_ref, k_ref, v_ref, o_ref, s_scr, *, heads, mask_kind, tq, tk, group, q_off, sk):
    q_lo, g_own = _tile_geometry(pl.program_id(1), tq=tq, span=tk * group, q_off=q_off)
    own_mask = functools.partial(_visible, mask_kind, q_lo, tq=tq, tk=tk, sk=sk)
    for h in range(heads):
        q = q_ref[0, :, h * LANES:(h + 1) * LANES]
        out = _softmax_head(q, k_ref, v_ref, s_scr, h * LANES, h * LANES,
                            tq=tq, tk=tk, group=group, g_own=g_own, own_mask=own_mask)
        o_ref[0, :, h * HEAD_DIM:(h + 1) * HEAD_DIM] = out.astype(o_ref.dtype)


def _check_tiling(sk_pad, *, tq, tk, group, q_off):
    assert (tk * group) % tq == 0 and q_off % tq == 0 and sk_pad % (tk * group) == 0 and tk % LANES == 0


def _softmax_attention(q, k, v, *, heads, mask_kind, q_off, sk, tq, tk, group, name):
    bsz, sq, _ = q.shape
    sk_pad = k.shape[1]
    _check_tiling(sk_pad, tq=tq, tk=tk, group=group, q_off=q_off)
    kern = functools.partial(_softmax_attn_kernel, heads=heads, mask_kind=mask_kind,
                             tq=tq, tk=tk, group=group, q_off=q_off, sk=sk)
    cols = heads * LANES
    return _attn_call(kern, (q, k, v), [_q_spec(tq, cols), _k_spec(sk_pad, cols), _k_spec(sk_pad, cols)],
                      heads * HEAD_DIM, bsz=bsz, sq=sq, tq=tq,
                      scratch=(pltpu.VMEM((tq, sk_pad), F32),), name=name)


def _split_bf16(x):
    hi = x.astype(BF16)
    lo = (x - hi.astype(F32)).astype(BF16)
    return hi, lo


SB_HEADS_PER_LOOP = 2


def _sb_kernel(q_ref, k_ref, v_ref, o_ref, *, tq, tk, q_off, sk):
    q_lo, n_before = _tile_geometry(pl.program_id(1), tq=tq, span=tk, q_off=q_off)
    r = lax.broadcasted_iota(I32, (tk, tk), 0)
    cidx = lax.broadcasted_iota(I32, (tk, tk), 1)
    later = jnp.where(r > cidx, 1.0, 0.0).astype(BF16)

    def block(q, h, kb, run, acc, mask):
        ks = pl.multiple_of(kb * tk, tk)
        k = k_ref[0, pl.ds(ks, tk), h * LANES:(h + 1) * LANES]
        v = v_ref[0, pl.ds(ks, tk), h * LANES:(h + 1) * LANES]
        z = lax.dot_general(q, k, _NT, preferred_element_type=F32)
        log_beta = jnp.minimum(z, 0.0) - jnp.log2(1.0 + jnp.exp2(-jnp.abs(z)))
        log_1m = log_beta - z
        if mask is not None:
            log_1m = jnp.where(mask, log_1m, 0.0)
        hi, lo = _split_bf16(log_1m)
        after = (jnp.dot(hi, later, preferred_element_type=F32)
                 + jnp.dot(lo, later, preferred_element_type=F32))
        a = jnp.exp2(log_beta + after + run)
        if mask is not None:
            a = jnp.where(mask, a, 0.0)
        acc = acc + jnp.dot(a.astype(BF16), v, preferred_element_type=F32)
        run = run + after[:, 0:1] + log_1m[:, 0:1]
        return run, acc

    own = _visible("strict", q_lo, n_before * tk, tq=tq, tk=tk, sk=sk)
    for h0 in range(0, D_HEADS, SB_HEADS_PER_LOOP):
        hs = range(h0, h0 + SB_HEADS_PER_LOOP)
        qs = [q_ref[0, :, h * LANES:(h + 1) * LANES] for h in hs]
        state = []
        for h, q in zip(hs, qs):
            state.extend(block(q, h, n_before, jnp.zeros((tq, 1), F32), jnp.zeros((tq, LANES), F32), own))

        def alive(state):
            top = functools.reduce(jnp.maximum, state[0::2])
            return jnp.max(top) > F32_EXP2_UNDERFLOW

        def cond(c):
            return (c[0] < n_before) & c[1]

        def body(c, hs=hs, qs=qs):
            j, _, state = c
            new = []
            for n, (h, q) in enumerate(zip(hs, qs)):
                new.extend(block(q, h, n_before - 1 - j, state[2 * n], state[2 * n + 1], None))
            return j + 1, alive(new), tuple(new)

        _, _, state = lax.while_loop(cond, body, (jnp.int32(0), alive(state), tuple(state)))
        for n, h in enumerate(hs):
            o_ref[0, :, h * HEAD_DIM:(h + 1) * HEAD_DIM] = state[2 * n + 1][:, :HEAD_DIM].astype(o_ref.dtype)


def _sb_attention(q, k, v, *, q_off, sk, tq, tk):
    bsz, sq, _ = q.shape
    sk_pad = k.shape[1]
    _check_tiling(sk_pad, tq=tq, tk=tk, group=1, q_off=q_off)
    kern = functools.partial(_sb_kernel, tq=tq, tk=tk, q_off=q_off, sk=sk)
    cols = D_HEADS * LANES
    return _attn_call(kern, (q, k, v), [_q_spec(tq, cols), _k_spec(sk_pad, cols), _k_spec(sk_pad, cols)],
                      D_HEADS * HEAD_DIM, bsz=bsz, sq=sq, tq=tq, name="sb_attention")


def _sortable_key(score):
    bits = lax.bitcast_convert_type(score, I32)
    return jnp.where(bits < 0, bits ^ 0x7FFFFFFF, bits)


KEY_OF_NEG_INF = -0x7F800001


def _dsa_kernel(q_ref, k_ref, v_ref, qi_ref, ki_ref, wi_ref, o_ref, key_s, bias_s, s_scr, w_s,
                *, tq, tk, group, q_off, sk, topk):
    q_lo, g_own = _tile_geometry(pl.program_id(1), tq=tq, span=tk * group, q_off=q_off)
    nt = tk // LANES
    spans = functools.partial(_span_blocks, tk=tk, group=group)
    own_visible = functools.partial(_visible, "chunk", q_lo, tq=tq, tk=tk, sk=sk)

    def over_spans(body, init, own_body=None):
        def span(g, c):
            for ks in spans(g):
                c = body(ks, c)
            return c
        if own_body is None:
            return lax.fori_loop(0, g_own + 1, span, init)
        c = lax.fori_loop(0, g_own, span, init)
        for ks in spans(g_own):
            c = own_body(ks, c)
        return c

    for h in range(IDX_HEADS):
        w_s[h] = jnp.broadcast_to(wi_ref[0, :, h:h + 1], (tq, LANES))
    qis = [qi_ref[0, :, h * IDX_DIM:(h + 1) * IDX_DIM] for h in range(IDX_HEADS)]

    def score_block(ks, visible):
        ki = ki_ref[0, pl.ds(ks, tk), :]
        total = jnp.zeros((tq, tk), F32)
        for h in range(IDX_HEADS):
            sc = lax.dot_general(qis[h], ki, _NT, preferred_element_type=F32)
            total = total + jnp.maximum(sc, 0.0) * jnp.concatenate([w_s[h]] * nt, axis=1)
        if visible is not None:
            total = jnp.where(visible(ks), total, -jnp.inf)
        key_s[:, pl.ds(ks, tk)] = _sortable_key(total)
        return 0

    over_spans(lambda ks, c: score_block(ks, None), 0, lambda ks, c: score_block(ks, own_visible))

    def count(hit):
        def body(ks, cnt):
            for t in _lane_tiles(key_s[:, pl.ds(ks, tk)]):
                cnt = cnt + jnp.where(hit(t), 1.0, 0.0)
            return cnt
        return jnp.sum(over_spans(body, jnp.zeros((tq, LANES), F32)), axis=1, keepdims=True)

    def count_ge(cand):
        cand = jnp.broadcast_to(cand, (tq, LANES))
        return count(lambda t: t >= cand)

    kf = float(topk)
    lowest = jnp.full((tq, 1), INT_MIN, I32)
    zero = jnp.zeros((tq, 1), I32)
    n_zero = count_ge(zero)
    ok = n_zero >= kf
    thr = jnp.where(ok, zero, lowest)
    n_thr = jnp.where(ok, n_zero, float(tk * group) * (g_own + 1).astype(F32))

    def bit_step(it, c):
        thr, n_thr = c
        cand = thr | jnp.left_shift(jnp.int32(1), 30 - it)
        n_cand = count_ge(cand)
        ok = n_cand >= kf
        return jnp.where(ok, cand, thr), jnp.where(ok, n_cand, n_thr)

    thr, n_thr = lax.fori_loop(0, 31, bit_step, (thr, n_thr))
    thr_b = jnp.broadcast_to(thr, (tq, LANES))

    def write_bias(ks, bias, visible):
        if visible is not None:
            bias = jnp.where(visible(ks), bias, MASKED)
        bias_s[:, pl.ds(ks, tk)] = bias

    def plain_block(ks, visible):
        key = key_s[:, pl.ds(ks, tk)]
        bias = jnp.concatenate([jnp.where(t >= thr_b, 0.0, MASKED) for t in _lane_tiles(key)], axis=1)
        write_bias(ks, bias, visible)
        return 0

    tied = jnp.max(jnp.where(thr > KEY_OF_NEG_INF, n_thr, 0.0)) > kf

    @pl.when(jnp.logical_not(tied))
    def _():
        over_spans(lambda ks, c: plain_block(ks, None), 0, lambda ks, c: plain_block(ks, own_visible))

    @pl.when(tied)
    def _():
        r = lax.broadcasted_iota(I32, (tk, tk), 0)
        cidx = lax.broadcasted_iota(I32, (tk, tk), 1)
        upto = jnp.where(r <= cidx, 1.0, 0.0).astype(BF16)
        need = kf - count(lambda t: t > thr_b)

        def tie_block(ks, seen, visible):
            key = key_s[:, pl.ds(ks, tk)]
            tie = jnp.where(key == thr, 1.0, 0.0)
            rank = seen + jnp.dot(tie.astype(BF16), upto, preferred_element_type=F32)
            sel = (key > thr) | ((key == thr) & (rank <= need))
            write_bias(ks, jnp.where(sel, 0.0, MASKED), visible)
            return seen + jnp.sum(tie, axis=1, keepdims=True)

        over_spans(lambda ks, seen: tie_block(ks, seen, None), jnp.zeros((tq, 1), F32),
                   lambda ks, seen: tie_block(ks, seen, own_visible))

    rep = A_HEADS // A_KV_HEADS
    for h in range(A_HEADS):
        g = h // rep
        q = q_ref[0, :, h * LANES:(h + 1) * LANES]
        out = _softmax_head(q, k_ref, v_ref, s_scr, g * LANES, g * LANES,
                            tq=tq, tk=tk, group=group, g_own=g_own, own_mask=None, bias_ref=bias_s)
        o_ref[0, :, h * HEAD_DIM:(h + 1) * HEAD_DIM] = out.astype(o_ref.dtype)


def _dsa_attention(q, k, v, qi, ki, wi, *, q_off, sk, tq, tk, group):
    bsz, sq, _ = q.shape
    sk_pad = k.shape[1]
    _check_tiling(sk_pad, tq=tq, tk=tk, group=group, q_off=q_off)
    topk = min(TOPK_MAX, sk // 4)
    kern = functools.partial(_dsa_kernel, tq=tq, tk=tk, group=group, q_off=q_off, sk=sk, topk=topk)
    kv_cols = A_KV_HEADS * LANES
    return _attn_call(kern, (q, k, v, qi, ki, wi),
                      [_q_spec(tq, A_HEADS * LANES), _k_spec(sk_pad, kv_cols), _k_spec(sk_pad, kv_cols),
                       _q_spec(tq, IDX_HEADS * IDX_DIM), _k_spec(sk_pad, IDX_DIM), _q_spec(tq, IDX_HEADS)],
                      A_HEADS * HEAD_DIM, bsz=bsz, sq=sq, tq=tq,
                      scratch=(pltpu.VMEM((tq, sk_pad), I32), pltpu.VMEM((tq, sk_pad), F32),
                               pltpu.VMEM((tq, sk_pad), F32), pltpu.VMEM((IDX_HEADS, tq, LANES), F32)),
                      name="dsa_attention")


def _rope(x, pos, n_rot):
    half = n_rot // 2
    inv = ROPE_THETA ** (-jnp.arange(half, dtype=F32) * 2.0 / n_rot)
    ang = pos.astype(F32)[:, None] * inv[None, :]
    shape = (ang.shape[0],) + (1,) * (x.ndim - 3) + (half,)
    cos = jnp.cos(ang).reshape(shape)
    sin = jnp.sin(ang).reshape(shape)
    x1 = x[..., :half]
    x2 = x[..., half:n_rot]
    return jnp.concatenate([x1 * cos - x2 * sin, x1 * sin + x2 * cos, x[..., n_rot:]], axis=-1)


def _head_lanes(parts, ones_col=False):
    bsz, s_len, heads = parts[0].shape[:3]
    parts = [p.astype(BF16) for p in parts]
    used = sum(p.shape[-1] for p in parts)
    if ones_col:
        parts.append(jnp.ones((bsz, s_len, heads, 1), BF16))
        used += 1
    parts.append(jnp.zeros((bsz, s_len, heads, LANES - used), BF16))
    return jnp.concatenate(parts, axis=-1).reshape(bsz, s_len, heads * LANES)


def _pad_rows(rows, sk_pad):
    pad = sk_pad - rows.shape[1]
    return jnp.pad(rows, ((0, 0), (0, pad), (0, 0))) if pad else rows


def _with_past(past, new):
    return new if past is None else jnp.concatenate([past, new], axis=1)


def _three_bf16(x):
    def top(v):
        return lax.bitcast_convert_type(lax.bitcast_convert_type(v, I32) & jnp.int32(-65536), F32)
    hi = top(x)
    mid = top(x - hi)
    return hi, mid, x - hi - mid


def _tiles(s_len, past_len):
    tq = min(256, s_len)
    tk, group = (256, 4) if past_len == 0 else (128, 3)
    return tq, tk, group, _round_up(past_len + s_len, tk * group)


def _even_mixer(x, h_gain, q_pos, past, w_in, g_bq, g_bkv, w_uq, w_ukv, w_o):
    bsz, s_len, d = x.shape
    m = bsz * s_len
    past_len = 0 if past is None else past[0].shape[1]
    sk = past_len + s_len
    tq, tk, group, sk_pad = _tiles(s_len, past_len)
    cuts = np.cumsum(EVEN_SPLIT)[:-1].tolist()
    proj = _proj(x.reshape(m, d), h_gain, w_in).reshape(bsz, s_len, -1)
    qa, ka, va, qi, ki, wi, cq, ckv, kr = jnp.split(proj, cuts, axis=-1)
    qa = _rope(qa.reshape(bsz, s_len, A_HEADS, HEAD_DIM), q_pos, ROT_DIM)
    ka = _rope(ka.reshape(bsz, s_len, A_KV_HEADS, HEAD_DIM), q_pos, ROT_DIM)
    va = va.reshape(bsz, s_len, A_KV_HEADS, HEAD_DIM)
    qi = _rope(qi.reshape(bsz, s_len, IDX_HEADS, IDX_DIM), q_pos, IDX_ROT) * IDX_DIM ** -0.5
    ki = _rope(ki, q_pos, IDX_ROT)
    wi = wi * IDX_HEADS ** -0.5
    qb = _proj(cq.reshape(m, Q_RANK), g_bq, w_uq).reshape(bsz, s_len, B_HEADS, B_NOPE + B_ROPE)
    qn = qb[..., :B_NOPE]
    qr = _rope(qb[..., B_NOPE:], q_pos, B_ROPE)
    kv_new, lat = _proj(ckv.reshape(m, KV_RANK), g_bkv, w_ukv, emit_h=True)
    lat = lat.reshape(bsz, s_len, KV_RANK)
    kr = _rope(kr, q_pos, B_ROPE)
    new_rows = (ka, va, ki, lat, kr)
    p = (None,) * 5 if past is None else past
    if past is None:
        kv = kv_new.reshape(bsz, s_len, B_HEADS, B_NOPE + B_VDIM)
    else:
        lat_all = jnp.concatenate([past[3], lat], axis=1)
        kv = _proj(lat_all.reshape(bsz * sk, KV_RANK), jnp.ones((KV_RANK,), F32), w_ukv, norm=False,
                   tm=sk).reshape(bsz, sk, B_HEADS, B_NOPE + B_VDIM)
    kr_all = _with_past(p[4], kr)
    kn, vb = kv[..., :B_NOPE], kv[..., B_NOPE:]

    out_a = _dsa_attention(
        _head_lanes([qa * (HEAD_DIM ** -0.5 * LOG2E)]),
        _pad_rows(_head_lanes([_with_past(p[0], ka)]), sk_pad),
        _pad_rows(_head_lanes([_with_past(p[1], va)], ones_col=True), sk_pad),
        qi.reshape(bsz, s_len, -1).astype(BF16), _pad_rows(_with_past(p[2], ki), sk_pad).astype(BF16), wi,
        q_off=past_len, sk=sk, tq=tq, tk=tk, group=group)

    scale_b = (B_NOPE + B_ROPE) ** -0.5 * LOG2E
    kr_h = jnp.broadcast_to(kr_all[:, :, None, :], (bsz, sk, B_HEADS, B_ROPE))
    out_b = _softmax_attention(
        _head_lanes([qn * scale_b, qr * scale_b]),
        _pad_rows(_head_lanes([kn, kr_h]), sk_pad),
        _pad_rows(_head_lanes([vb], ones_col=True), sk_pad),
        heads=B_HEADS, mask_kind="chunk", q_off=past_len, sk=sk, tq=tq, tk=tk, group=group,
        name="mla_attention")

    mix = jnp.concatenate([out_a, out_b], axis=-1).reshape(m, -1)
    y = _matmul_residual(mix, w_o, x.reshape(m, d)).reshape(bsz, s_len, d)
    return y, new_rows


def _odd_mixer(x, h_gain, q_pos, past, w_in, b_f, w_o):
    bsz, s_len, d = x.shape
    m = bsz * s_len
    past_len = 0 if past is None else past[0].shape[1]
    sk = past_len + s_len
    tq, tk, group, sk_pad = _tiles(s_len, past_len)
    cuts = np.cumsum(ODD_SPLIT)[:-1].tolist()
    proj = _proj(x.reshape(m, d), h_gain, w_in).reshape(bsz, s_len, -1)
    qc, kc, vc, fc, qd, kd, vd = jnp.split(proj, cuts, axis=-1)
    heads = lambda a, n: a.reshape(bsz, s_len, n, HEAD_DIM)
    qc, kc, vc = heads(qc, C_HEADS), heads(kc, C_HEADS), heads(vc, C_HEADS)
    qd, kd, vd = heads(qd, D_HEADS), heads(kd, D_HEADS), heads(vd, D_HEADS)
    logf = jax.nn.log_sigmoid(fc + b_f)
    new_rows = (kc, vc, logf, kd, vd)
    p = (None,) * 5 if past is None else past
    scale = HEAD_DIM ** -0.5 * LOG2E

    c_all = jnp.cumsum(_with_past(p[2], logf), axis=1)
    bias = _three_bf16(-LOG2E * c_all)
    ones = jnp.ones((bsz, s_len, C_HEADS, len(bias)), F32)
    out_c = _softmax_attention(
        _head_lanes([qc * scale, ones]),
        _pad_rows(_head_lanes([_with_past(p[0], kc)] + [b[..., None] for b in bias]), sk_pad),
        _pad_rows(_head_lanes([_with_past(p[1], vc)], ones_col=True), sk_pad),
        heads=C_HEADS, mask_kind="causal", q_off=past_len, sk=sk, tq=tq, tk=tk, group=group,
        name="fox_attention")

    out_d = _sb_attention(
        _head_lanes([qd * scale]),
        _pad_rows(_head_lanes([_with_past(p[3], kd)]), sk_pad),
        _pad_rows(_head_lanes([_with_past(p[4], vd)]), sk_pad),
        q_off=past_len, sk=sk, tq=tq, tk=tk)

    mix = jnp.concatenate([out_c, out_d], axis=-1).reshape(m, -1)
    y = _matmul_residual(mix, w_o, x.reshape(m, d)).reshape(bsz, s_len, d)
    return y, new_rows


def _final_norm_kernel(x_ref, g_ref, o_ref):
    x = x_ref[...]
    ms = jnp.mean(x * x, axis=-1, keepdims=True)
    o_ref[...] = x * lax.rsqrt(ms + EPS) * g_ref[...]


def _final_norm(x, g, *, tm=512):
    m, d = x.shape
    tm = min(tm, m)
    return pl.pallas_call(
        _final_norm_kernel,
        grid=(m // tm,),
        in_specs=[pl.BlockSpec((tm, d), lambda i: (i, 0)), pl.BlockSpec((1, d), lambda i: (0, 0))],
        out_specs=pl.BlockSpec((tm, d), lambda i: (i, 0)),
        out_shape=jax.ShapeDtypeStruct((m, d), F32),
        compiler_params=_params("parallel"),
        name="final_norm",
    )(x, g.reshape(1, d).astype(F32))


def _trunk(x, q_pos, caches, params):
    (g_mix, g_ffn, g_final, w_in_even, g_b_q, g_b_kv, w_b_uq, w_b_ukv, w_o_even,
     w_in_odd, b_forget, w_o_odd, w_up, w_conv, b_conv, w_down) = params
    bsz, s_len, d = x.shape
    depth = g_mix.shape[0]
    even_rows = [[] for _ in range(5)]
    odd_rows = [[] for _ in range(5)]
    conv_rows = []
    for l in range(depth):
        j = l // 2
        if l % 2 == 0:
            past = None if caches is None else tuple(c[j] for c in caches[0:5])
            x, rows = _even_mixer(x, g_mix[l], q_pos, past, w_in_even[j], g_b_q[j], g_b_kv[j],
                                  w_b_uq[j], w_b_ukv[j], w_o_even[j])
            for lst, r in zip(even_rows, rows):
                lst.append(r)
        else:
            past = None if caches is None else tuple(c[j] for c in caches[5:10])
            x, rows = _odd_mixer(x, g_mix[l], q_pos, past, w_in_odd[j], b_forget[j], w_o_odd[j])
            for lst, r in zip(odd_rows, rows):
                lst.append(r)
        state = jnp.zeros((bsz, CONV_W - 1, D_FF), F32) if caches is None else caches[10][l]
        y, new_buf = _conv_ffn(x.reshape(bsz * s_len, d), g_ffn[l], w_up[l], w_conv[l], b_conv[l], w_down[l],
                               state, seq_len=s_len, tm=1024)
        x = y.reshape(bsz, s_len, d)
        conv_rows.append(new_buf)
    out = _final_norm(x.reshape(bsz * s_len, d), g_final).reshape(bsz, s_len, d)
    states = [jnp.stack(r, axis=0) for r in even_rows + odd_rows] + [jnp.stack(conv_rows, axis=0)]
    return out, states


def kernel(x_prompt, x_sample, cache_a_k, cache_a_v, cache_a_idx_k, cache_b_latent, cache_b_rope,
           cache_c_k, cache_c_v, cache_c_logf, cache_d_k, cache_d_v, state_ffn_conv,
           g_mix, g_ffn, g_final, w_in_even, g_b_q, g_b_kv, w_b_uq, w_b_ukv, w_o_even,
           w_in_odd, b_forget, w_o_odd, w_up, w_conv, b_conv, w_down):
    params = (g_mix, g_ffn, g_final, w_in_even, g_b_q, g_b_kv, w_b_uq, w_b_ukv, w_o_even,
              w_in_odd, b_forget, w_o_odd, w_up, w_conv, b_conv, w_down)
    caches = (cache_a_k, cache_a_v, cache_a_idx_k, cache_b_latent, cache_b_rope,
              cache_c_k, cache_c_v, cache_c_logf, cache_d_k, cache_d_v, state_ffn_conv)
    past_len = cache_a_k.shape[2]
    pos_prompt = jnp.arange(x_prompt.shape[1], dtype=I32)
    pos_sample = past_len + jnp.arange(x_sample.shape[1], dtype=I32)
    y_prompt, p_states = _trunk(x_prompt, pos_prompt, None, params)
    y_sample, s_states = _trunk(x_sample, pos_sample, caches, params)
    return (y_prompt, y_sample, *p_states, *s_states)
```

```python
import functools

import jax
import jax.numpy as jnp
import numpy as np
from jax import lax
from jax.experimental import pallas as pl
from jax.experimental.pallas import tpu as pltpu

F32 = jnp.float32
BF16 = jnp.bfloat16
I32 = jnp.int32

CHUNK = 64
ROPE_THETA = 500000.0
EPS = 1e-6
HEAD_DIM = 64
ROT_DIM = HEAD_DIM // 4
A_HEADS = 8
A_KV_HEADS = 2
IDX_HEADS = 8
IDX_DIM = 32
IDX_ROT = IDX_DIM // 4
TOPK_MAX = 256
B_HEADS = 8
Q_RANK = 256
KV_RANK = 128
B_NOPE = 64
B_ROPE = 32
B_VDIM = 64
C_HEADS = 8
D_HEADS = 8
D_FF = 2816
CONV_W = 3

EVEN_SPLIT = [A_HEADS * HEAD_DIM, A_KV_HEADS * HEAD_DIM, A_KV_HEADS * HEAD_DIM,
              IDX_HEADS * IDX_DIM, IDX_DIM, IDX_HEADS, Q_RANK, KV_RANK, B_ROPE]
ODD_SPLIT = [C_HEADS * HEAD_DIM] * 3 + [C_HEADS] + [D_HEADS * HEAD_DIM] * 3

LANES = 128
VMEM_LIMIT_BYTES = 56 * 1024 * 1024
MASKED = -1e30
INT_MIN = -2 ** 31
LOG2E = 1.4426950408889634
F32_EXP2_UNDERFLOW = -152.0

_NT = (((1,), (1,)), ((), ()))


def _params(*sem):
    return pltpu.CompilerParams(dimension_semantics=sem, vmem_limit_bytes=VMEM_LIMIT_BYTES)


def _round_up(n, m):
    return (n + m - 1) // m * m


def _chunk_of(pos):
    return jnp.right_shift(pos, CHUNK.bit_length() - 1)


def _proj_kernel(x_ref, g_ref, w_ref, *out_refs, norm, emit_h):
    x = x_ref[...]
    if norm:
        ms = jnp.mean(x * x, axis=-1, keepdims=True)
        x = x * lax.rsqrt(ms + EPS) * g_ref[...]
    out_refs[0][...] = jnp.dot(x.astype(BF16), w_ref[...], preferred_element_type=F32)
    if emit_h:
        out_refs[1][...] = x


def _proj(x, g, w, *, norm=True, emit_h=False, tm=512):
    m, k = x.shape
    n = w.shape[1]
    tm = min(tm, m)
    assert m % tm == 0
    out_shape = [jax.ShapeDtypeStruct((m, n), F32)]
    out_specs = [pl.BlockSpec((tm, n), lambda i: (i, 0))]
    if emit_h:
        out_shape.append(jax.ShapeDtypeStruct((m, k), F32))
        out_specs.append(pl.BlockSpec((tm, k), lambda i: (i, 0)))
    res = pl.pallas_call(
        functools.partial(_proj_kernel, norm=norm, emit_h=emit_h),
        grid=(m // tm,),
        in_specs=[pl.BlockSpec((tm, k), lambda i: (i, 0)),
                  pl.BlockSpec((1, k), lambda i: (0, 0)),
                  pl.BlockSpec((k, n), lambda i: (0, 0))],
        out_specs=out_specs,
        out_shape=out_shape,
        compiler_params=_params("parallel"),
        name="proj",
    )(x, g.reshape(1, k).astype(F32), w.astype(BF16))
    return res if emit_h else res[0]


def _matmul_residual_kernel(a_ref, w_ref, r_ref, o_ref):
    o_ref[...] = r_ref[...] + jnp.dot(a_ref[...], w_ref[...], preferred_element_type=F32)


def _matmul_residual(a, w, res, *, tm=512):
    m, k = a.shape
    n = w.shape[1]
    tm = min(tm, m)
    assert m % tm == 0
    return pl.pallas_call(
        _matmul_residual_kernel,
        grid=(m // tm,),
        in_specs=[pl.BlockSpec((tm, k), lambda i: (i, 0)),
                  pl.BlockSpec((k, n), lambda i: (0, 0)),
                  pl.BlockSpec((tm, n), lambda i: (i, 0))],
        out_specs=pl.BlockSpec((tm, n), lambda i: (i, 0)),
        out_shape=jax.ShapeDtypeStruct((m, n), F32),
        compiler_params=_params("parallel"),
        name="matmul_residual",
    )(a, w.astype(BF16), res)


def _conv_ffn_kernel(x_ref, g_ref, wg_ref, wu_ref, wc_ref, bc_ref, wd_ref, st_ref,
                     y_ref, ns_ref, h_s, acc_s, carry_s, *, tiles_per_seq, tm, tf):
    i = pl.program_id(0)
    c = pl.program_id(1)
    nc = pl.num_programs(1)
    cols = pl.ds(pl.multiple_of(c * tf, tf), tf)

    @pl.when(c == 0)
    def _():
        x = x_ref[...]
        ms = jnp.mean(x * x, axis=-1, keepdims=True)
        h_s[...] = (x * lax.rsqrt(ms + EPS) * g_ref[...]).astype(BF16)
        acc_s[...] = jnp.zeros_like(acc_s)

    h = h_s[...]
    gate = jnp.dot(h, wg_ref[...], preferred_element_type=F32)
    up = jnp.dot(h, wu_ref[...], preferred_element_type=F32)

    @pl.when(i % tiles_per_seq == 0)
    def _():
        carry_s[c] = st_ref[0, :, cols]

    prev = carry_s[c]
    row = lax.broadcasted_iota(I32, gate.shape, 0)
    g1 = pltpu.roll(gate, 1, 0)
    g1 = jnp.where(row == 0, prev[1:2, :], g1)
    g2 = pltpu.roll(gate, 2, 0)
    g2 = jnp.where(row == 0, prev[0:1, :], jnp.where(row == 1, prev[1:2, :], g2))
    wc = wc_ref[...]
    gc = bc_ref[...] + g2 * wc[0:1, :]
    gc = gc + g1 * wc[1:2, :]
    gc = gc + gate * wc[2:3, :]
    act = gc * jax.nn.sigmoid(gc) * up
    acc_s[...] += jnp.dot(act.astype(BF16), wd_ref[...], preferred_element_type=F32)

    tail = gate[tm - 2:tm, :]
    carry_s[c] = tail
    ns_ref[0, :, cols] = tail

    @pl.when(c == nc - 1)
    def _():
        y_ref[...] = x_ref[...] + acc_s[...]


def _conv_ffn(x, g, w_up, w_conv, b_conv, w_down, state, *, seq_len, tm, tf=256):
    m, d = x.shape
    nseq = m // seq_len
    tm = min(tm, seq_len)
    assert seq_len % tm == 0 and D_FF % tf == 0
    tiles_per_seq = seq_len // tm
    nc = D_FF // tf
    wg = w_up[:, :D_FF].astype(BF16)
    wu = w_up[:, D_FF:].astype(BF16)
    state_spec = pl.BlockSpec((1, CONV_W - 1, D_FF), lambda i, c: (i // tiles_per_seq, 0, 0))
    y, ns = pl.pallas_call(
        functools.partial(_conv_ffn_kernel, tiles_per_seq=tiles_per_seq, tm=tm, tf=tf),
        grid=(m // tm, nc),
        in_specs=[pl.BlockSpec((tm, d), lambda i, c: (i, 0)),
                  pl.BlockSpec((1, d), lambda i, c: (0, 0)),
                  pl.BlockSpec((d, tf), lambda i, c: (0, c)),
                  pl.BlockSpec((d, tf), lambda i, c: (0, c)),
                  pl.BlockSpec((CONV_W, tf), lambda i, c: (0, c)),
                  pl.BlockSpec((1, tf), lambda i, c: (0, c)),
                  pl.BlockSpec((tf, d), lambda i, c: (c, 0)),
                  state_spec],
        out_specs=[pl.BlockSpec((tm, d), lambda i, c: (i, 0)), state_spec],
        out_shape=[jax.ShapeDtypeStruct((m, d), F32),
                   jax.ShapeDtypeStruct((nseq, CONV_W - 1, D_FF), F32)],
        scratch_shapes=[pltpu.VMEM((tm, d), BF16),
                        pltpu.VMEM((tm, d), F32),
                        pltpu.VMEM((nc, CONV_W - 1, tf), F32)],
        compiler_params=_params("arbitrary", "arbitrary"),
        name="conv_ffn",
    )(x, g.reshape(1, d).astype(F32), wg, wu, w_conv.astype(F32), b_conv.reshape(1, D_FF).astype(F32),
      w_down.astype(BF16), state.astype(F32))
    return y, ns


def _attn_call(kernel, inputs, in_specs, out_cols, *, bsz, sq, tq, scratch=(), name):
    return pl.pallas_call(
        kernel,
        grid=(bsz, sq // tq),
        in_specs=in_specs,
        out_specs=pl.BlockSpec((1, tq, out_cols), lambda b, i: (b, i, 0)),
        out_shape=jax.ShapeDtypeStruct((bsz, sq, out_cols), BF16),
        scratch_shapes=list(scratch),
        compiler_params=_params("parallel", "arbitrary"),
        name=name,
    )(*inputs)


def _q_spec(tq, cols):
    return pl.BlockSpec((1, tq, cols), lambda b, i: (b, i, 0))


def _k_spec(rows, cols):
    return pl.BlockSpec((1, rows, cols), lambda b, i: (b, 0, 0))


def _tile_geometry(i, *, tq, span, q_off):
    q_lo = q_off + i * tq
    return q_lo, q_lo // span


def _lane_tiles(x):
    return [x[:, j * LANES:(j + 1) * LANES] for j in range(x.shape[1] // LANES)]


def _visible(kind, q_lo, ks, *, tq, tk, sk):
    q_pos = q_lo + lax.broadcasted_iota(I32, (tq, 1), 0)
    k_pos = ks + lax.broadcasted_iota(I32, (1, tk), 1)
    if kind == "chunk":
        return (_chunk_of(k_pos) <= _chunk_of(q_pos)) & (k_pos < sk)
    if kind == "causal":
        return k_pos <= q_pos
    assert kind == "strict"
    return k_pos < q_pos


def _span_blocks(g, *, tk, group):
    return [pl.multiple_of((g * group + u) * tk, tk) for u in range(group)]


def _softmax_head(q, k_ref, v_ref, s_scr, kcol, vcol, *, tq, tk, group, g_own, own_mask, bias_ref=None):
    def logits(ks):
        k = k_ref[0, pl.ds(ks, tk), kcol:kcol + LANES]
        s = lax.dot_general(q, k, _NT, preferred_element_type=F32)
        if bias_ref is not None:
            s = s + bias_ref[:, pl.ds(ks, tk)]
        return s

    def keep(ks, s, mx):
        s_scr[:, pl.ds(ks, tk)] = s
        for t in _lane_tiles(s):
            mx = jnp.maximum(mx, t)
        return mx

    def full_span(g, mx):
        for ks in _span_blocks(g, tk=tk, group=group):
            mx = keep(ks, logits(ks), mx)
        return mx

    mx = jnp.full((tq, LANES), MASKED, F32)
    if own_mask is None:
        mx = lax.fori_loop(0, g_own + 1, full_span, mx)
    else:
        mx = lax.fori_loop(0, g_own, full_span, mx)
        for ks in _span_blocks(g_own, tk=tk, group=group):
            mx = keep(ks, jnp.where(own_mask(ks), logits(ks), MASKED), mx)
    m = jnp.broadcast_to(jnp.max(mx, axis=1, keepdims=True), (tq, LANES))

    def weigh_span(g, acc):
        for ks in _span_blocks(g, tk=tk, group=group):
            s = s_scr[:, pl.ds(ks, tk)]
            p = jnp.concatenate([jnp.exp2(t - m) for t in _lane_tiles(s)], axis=1)
            v = v_ref[0, pl.ds(ks, tk), vcol:vcol + LANES]
            acc = acc + jnp.dot(p.astype(BF16), v, preferred_element_type=F32)
        return acc

    acc = lax.fori_loop(0, g_own + 1, weigh_span, jnp.zeros((tq, LANES), F32))
    return acc[:, :HEAD_DIM] / acc[:, HEAD_DIM:HEAD_DIM + 1]


def _softmax_attn_kernel(q_ref, k_ref, v_ref, o_ref, s_scr, *, heads, mask_kind, tq, tk, group, q_off, sk):
    q_lo, g_own = _tile_geometry(pl.program_id(1), tq=tq, span=tk * group, q_off=q_off)
    own_mask = functools.partial(_visible, mask_kind, q_lo, tq=tq, tk=tk, sk=sk)
    for h in range(heads):
        q = q_ref[0, :, h * LANES:(h + 1) * LANES]
        out = _softmax_head(q, k_ref, v_ref, s_scr, h * LANES, h * LANES,
                            tq=tq, tk=tk, group=group, g_own=g_own, own_mask=own_mask)
        o_ref[0, :, h * HEAD_DIM:(h + 1) * HEAD_DIM] = out.astype(o_ref.dtype)


def _check_tiling(sk_pad, *, tq, tk, group, q_off):
    assert (tk * group) % tq == 0 and q_off % tq == 0 and sk_pad % (tk * group) == 0 and tk % LANES == 0


def _softmax_attention(q, k, v, *, heads, mask_kind, q_off, sk, tq, tk, group, name):
    bsz, sq, _ = q.shape
    sk_pad = k.shape[1]
    _check_tiling(sk_pad, tq=tq, tk=tk, group=group, q_off=q_off)
    kern = functools.partial(_softmax_attn_kernel, heads=heads, mask_kind=mask_kind,
                             tq=tq, tk=tk, group=group, q_off=q_off, sk=sk)
    cols = heads * LANES
    return _attn_call(kern, (q, k, v), [_q_spec(tq, cols), _k_spec(sk_pad, cols), _k_spec(sk_pad, cols)],
                      heads * HEAD_DIM, bsz=bsz, sq=sq, tq=tq,
                      scratch=(pltpu.VMEM((tq, sk_pad), F32),), name=name)


def _split_bf16(x):
    hi = x.astype(BF16)
    lo = (x - hi.astype(F32)).astype(BF16)
    return hi, lo


SB_HEADS_PER_LOOP = 2


def _sb_kernel(q_ref, k_ref, v_ref, o_ref, *, tq, tk, q_off, sk):
    q_lo, n_before = _tile_geometry(pl.program_id(1), tq=tq, span=tk, q_off=q_off)
    r = lax.broadcasted_iota(I32, (tk, tk), 0)
    cidx = lax.broadcasted_iota(I32, (tk, tk), 1)
    later = jnp.where(r > cidx, 1.0, 0.0).astype(BF16)

    def block(q, h, kb, run, acc, mask):
        ks = pl.multiple_of(kb * tk, tk)
        k = k_ref[0, pl.ds(ks, tk), h * LANES:(h + 1) * LANES]
        v = v_ref[0, pl.ds(ks, tk), h * LANES:(h + 1) * LANES]
        z = lax.dot_general(q, k, _NT, preferred_element_type=F32)
        log_beta = jnp.minimum(z, 0.0) - jnp.log2(1.0 + jnp.exp2(-jnp.abs(z)))
        log_1m = log_beta - z
        if mask is not None:
            log_1m = jnp.where(mask, log_1m, 0.0)
        hi, lo = _split_bf16(log_1m)
        after = (jnp.dot(hi, later, preferred_element_type=F32)
                 + jnp.dot(lo, later, preferred_element_type=F32))
        a = jnp.exp2(log_beta + after + run)
        if mask is not None:
            a = jnp.where(mask, a, 0.0)
        acc = acc + jnp.dot(a.astype(BF16), v, preferred_element_type=F32)
        run = run + after[:, 0:1] + log_1m[:, 0:1]
        return run, acc

    own = _visible("strict", q_lo, n_before * tk, tq=tq, tk=tk, sk=sk)
    for h0 in range(0, D_HEADS, SB_HEADS_PER_LOOP):
        hs = range(h0, h0 + SB_HEADS_PER_LOOP)
        qs = [q_ref[0, :, h * LANES:(h + 1) * LANES] for h in hs]
        state = []
        for h, q in zip(hs, qs):
            state.extend(block(q, h, n_before, jnp.zeros((tq, 1), F32), jnp.zeros((tq, LANES), F32), own))

        def alive(state):
            top = functools.reduce(jnp.maximum, state[0::2])
            return jnp.max(top) > F32_EXP2_UNDERFLOW

        def cond(c):
            return (c[0] < n_before) & c[1]

        def body(c, hs=hs, qs=qs):
            j, _, state = c
            new = []
            for n, (h, q) in enumerate(zip(hs, qs)):
                new.extend(block(q, h, n_before - 1 - j, state[2 * n], state[2 * n + 1], None))
            return j + 1, alive(new), tuple(new)

        _, _, state = lax.while_loop(cond, body, (jnp.int32(0), alive(state), tuple(state)))
        for n, h in enumerate(hs):
            o_ref[0, :, h * HEAD_DIM:(h + 1) * HEAD_DIM] = state[2 * n + 1][:, :HEAD_DIM].astype(o_ref.dtype)


def _sb_attention(q, k, v, *, q_off, sk, tq, tk):
    bsz, sq, _ = q.shape
    sk_pad = k.shape[1]
    _check_tiling(sk_pad, tq=tq, tk=tk, group=1, q_off=q_off)
    kern = functools.partial(_sb_kernel, tq=tq, tk=tk, q_off=q_off, sk=sk)
    cols = D_HEADS * LANES
    return _attn_call(kern, (q, k, v), [_q_spec(tq, cols), _k_spec(sk_pad, cols), _k_spec(sk_pad, cols)],
                      D_HEADS * HEAD_DIM, bsz=bsz, sq=sq, tq=tq, name="sb_attention")


def _sortable_key(score):
    bits = lax.bitcast_convert_type(score, I32)
    return jnp.where(bits < 0, bits ^ 0x7FFFFFFF, bits)


KEY_OF_NEG_INF = -0x7F800001


def _dsa_kernel(q_ref, k_ref, v_ref, qi_ref, ki_ref, wi_ref, o_ref, key_s, bias_s, s_scr, w_s,
                *, tq, tk, group, q_off, sk, topk):
    q_lo, g_own = _tile_geometry(pl.program_id(1), tq=tq, span=tk * group, q_off=q_off)
    nt = tk // LANES
    spans = functools.partial(_span_blocks, tk=tk, group=group)
    own_visible = functools.partial(_visible, "chunk", q_lo, tq=tq, tk=tk, sk=sk)

    def over_spans(body, init, own_body=None):
        def span(g, c):
            for ks in spans(g):
                c = body(ks, c)
            return c
        if own_body is None:
            return lax.fori_loop(0, g_own + 1, span, init)
        c = lax.fori_loop(0, g_own, span, init)
        for ks in spans(g_own):
            c = own_body(ks, c)
        return c

    for h in range(IDX_HEADS):
        w_s[h] = jnp.broadcast_to(wi_ref[0, :, h:h + 1], (tq, LANES))
    qis = [qi_ref[0, :, h * IDX_DIM:(h + 1) * IDX_DIM] for h in range(IDX_HEADS)]

    def score_block(ks, visible):
        ki = ki_ref[0, pl.ds(ks, tk), :]
        total = jnp.zeros((tq, tk), F32)
        for h in range(IDX_HEADS):
            sc = lax.dot_general(qis[h], ki, _NT, preferred_element_type=F32)
            total = total + jnp.maximum(sc, 0.0) * jnp.concatenate([w_s[h]] * nt, axis=1)
        if visible is not None:
            total = jnp.where(visible(ks), total, -jnp.inf)
        key_s[:, pl.ds(ks, tk)] = _sortable_key(total)
        return 0

    over_spans(lambda ks, c: score_block(ks, None), 0, lambda ks, c: score_block(ks, own_visible))

    def count(hit):
        def body(ks, cnt):
            for t in _lane_tiles(key_s[:, pl.ds(ks, tk)]):
                cnt = cnt + jnp.where(hit(t), 1.0, 0.0)
            return cnt
        return jnp.sum(over_spans(body, jnp.zeros((tq, LANES), F32)), axis=1, keepdims=True)

    def count_ge(cand):
        cand = jnp.broadcast_to(cand, (tq, LANES))
        return count(lambda t: t >= cand)

    kf = float(topk)
    lowest = jnp.full((tq, 1), INT_MIN, I32)
    zero = jnp.zeros((tq, 1), I32)
    n_zero = count_ge(zero)
    ok = n_zero >= kf
    thr = jnp.where(ok, zero, lowest)
    n_thr = jnp.where(ok, n_zero, float(tk * group) * (g_own + 1).astype(F32))

    def bit_step(it, c):
        thr, n_thr = c
        cand = thr | jnp.left_shift(jnp.int32(1), 30 - it)
        n_cand = count_ge(cand)
        ok = n_cand >= kf
        return jnp.where(ok, cand, thr), jnp.where(ok, n_cand, n_thr)

    thr, n_thr = lax.fori_loop(0, 31, bit_step, (thr, n_thr))
    thr_b = jnp.broadcast_to(thr, (tq, LANES))

    def write_bias(ks, bias, visible):
        if visible is not None:
            bias = jnp.where(visible(ks), bias, MASKED)
        bias_s[:, pl.ds(ks, tk)] = bias

    def plain_block(ks, visible):
        key = key_s[:, pl.ds(ks, tk)]
        bias = jnp.concatenate([jnp.where(t >= thr_b, 0.0, MASKED) for t in _lane_tiles(key)], axis=1)
        write_bias(ks, bias, visible)
        return 0

    tied = jnp.max(jnp.where(thr > KEY_OF_NEG_INF, n_thr, 0.0)) > kf

    @pl.when(jnp.logical_not(tied))
    def _():
        over_spans(lambda ks, c: plain_block(ks, None), 0, lambda ks, c: plain_block(ks, own_visible))

    @pl.when(tied)
    def _():
        r = lax.broadcasted_iota(I32, (tk, tk), 0)
        cidx = lax.broadcasted_iota(I32, (tk, tk), 1)
        upto = jnp.where(r <= cidx, 1.0, 0.0).astype(BF16)
        need = kf - count(lambda t: t > thr_b)

        def tie_block(ks, seen, visible):
            key = key_s[:, pl.ds(ks, tk)]
            tie = jnp.where(key == thr, 1.0, 0.0)
            rank = seen + jnp.dot(tie.astype(BF16), upto, preferred_element_type=F32)
            sel = (key > thr) | ((key == thr) & (rank <= need))
            write_bias(ks, jnp.where(sel, 0.0, MASKED), visible)
            return seen + jnp.sum(tie, axis=1, keepdims=True)

        over_spans(lambda ks, seen: tie_block(ks, seen, None), jnp.zeros((tq, 1), F32),
                   lambda ks, seen: tie_block(ks, seen, own_visible))

    rep = A_HEADS // A_KV_HEADS
    for h in range(A_HEADS):
        g = h // rep
        q = q_ref[0, :, h * LANES:(h + 1) * LANES]
        out = _softmax_head(q, k_ref, v_ref, s_scr, g * LANES, g * LANES,
                            tq=tq, tk=tk, group=group, g_own=g_own, own_mask=None, bias_ref=bias_s)
        o_ref[0, :, h * HEAD_DIM:(h + 1) * HEAD_DIM] = out.astype(o_ref.dtype)


def _dsa_attention(q, k, v, qi, ki, wi, *, q_off, sk, tq, tk, group):
    bsz, sq, _ = q.shape
    sk_pad = k.shape[1]
    _check_tiling(sk_pad, tq=tq, tk=tk, group=group, q_off=q_off)
    topk = min(TOPK_MAX, sk // 4)
    kern = functools.partial(_dsa_kernel, tq=tq, tk=tk, group=group, q_off=q_off, sk=sk, topk=topk)
    kv_cols = A_KV_HEADS * LANES
    return _attn_call(kern, (q, k, v, qi, ki, wi),
                      [_q_spec(tq, A_HEADS * LANES), _k_spec(sk_pad, kv_cols), _k_spec(sk_pad, kv_cols),
                       _q_spec(tq, IDX_HEADS * IDX_DIM), _k_spec(sk_pad, IDX_DIM), _q_spec(tq, IDX_HEADS)],
                      A_HEADS * HEAD_DIM, bsz=bsz, sq=sq, tq=tq,
                      scratch=(pltpu.VMEM((tq, sk_pad), I32), pltpu.VMEM((tq, sk_pad), F32),
                               pltpu.VMEM((tq, sk_pad), F32), pltpu.VMEM((IDX_HEADS, tq, LANES), F32)),
                      name="dsa_attention")


def _rmsnorm_rows(x, g):
    ms = jnp.mean(x * x, axis=-1, keepdims=True)
    return x * lax.rsqrt(ms + EPS) * g


def _lane_index(shape):
    return lax.broadcasted_iota(I32, shape, len(shape) - 1)


def _one_hot_row(*lanes):
    lane = _lane_index((1, LANES))
    hit = functools.reduce(jnp.logical_or, [lane == l for l in lanes])
    return jnp.where(hit, 1.0, 0.0)


def _rotate(tile, tables, half):
    cos, sin_up, sin_dn = tables
    return tile * cos + pltpu.roll(tile, half, 1) * sin_up + pltpu.roll(tile, LANES - half, 1) * sin_dn


def _split_pair(pair):
    low = _lane_index(pair.shape) < HEAD_DIM
    return jnp.where(low, pair, 0.0), jnp.where(low, pltpu.roll(pair, HEAD_DIM, 1), 0.0)


def _store_heads(out_ref, compact, *, scale=None, extra=None):
    for p, pair in enumerate(_lane_tiles(compact)):
        for u, tile in enumerate(_split_pair(pair)):
            if scale is not None:
                tile = tile * scale
            if extra is not None:
                tile = tile + extra
            h = 2 * p + u
            out_ref[:, h * LANES:(h + 1) * LANES] = tile.astype(out_ref.dtype)


_E_QA, _E_KA, _E_VA, _E_QI, _E_KI, _E_WI, _E_CQ, _E_CKV, _E_KR, _E_END = (
    0, 512, 640, 768, 1024, 1152, 1280, 1536, 1664, 1792)
_B_ROPE_LANE = B_NOPE


def _even_proj_kernel(x_ref, g_ref, w_ref, gq_ref, wuq_ref, gkv_ref, wukv_ref, ta_ref, ti_ref, tb_ref,
                      qa_o, ka_o, va_o, qi_o, ki_o, wi_o, qb_o, kb_o, vb_o,
                      ka_s, va_s, ki_s, lat_s, kr_s):
    h = _rmsnorm_rows(x_ref[...], g_ref[...]).astype(BF16)
    p = jnp.dot(h, w_ref[...], preferred_element_type=F32)
    ta = (ta_ref[0], ta_ref[1], ta_ref[2])
    ti = (ti_ref[0], ti_ref[1], ti_ref[2])
    tb = (tb_ref[0], tb_ref[1], tb_ref[2])
    ones_col = _one_hot_row(HEAD_DIM)

    qa = jnp.concatenate([_rotate(t, ta, ROT_DIM // 2) for t in _lane_tiles(p[:, _E_QA:_E_KA])], axis=1)
    _store_heads(qa_o, qa, scale=HEAD_DIM ** -0.5 * LOG2E)
    ka = _rotate(p[:, _E_KA:_E_VA], ta, ROT_DIM // 2)
    ka_s[...] = ka
    _store_heads(ka_o, ka)
    va = p[:, _E_VA:_E_QI]
    va_s[...] = va
    _store_heads(va_o, va, extra=ones_col)
    qi = jnp.concatenate([_rotate(t, ti, IDX_ROT // 2) for t in _lane_tiles(p[:, _E_QI:_E_KI])], axis=1)
    qi_o[...] = (qi * IDX_DIM ** -0.5).astype(qi_o.dtype)
    ki = _rotate(p[:, _E_KI:_E_WI], ti, IDX_ROT // 2)[:, :IDX_DIM]
    ki_s[...] = ki
    ki_o[...] = ki.astype(ki_o.dtype)
    wi_o[...] = p[:, _E_WI:_E_WI + IDX_HEADS] * IDX_HEADS ** -0.5

    cq = _rmsnorm_rows(p[:, _E_CQ:_E_CKV], gq_ref[...]).astype(BF16)
    qb = jnp.dot(cq, wuq_ref[...], preferred_element_type=F32)
    scale_b = (B_NOPE + B_ROPE) ** -0.5 * LOG2E
    for hd, t in enumerate(_lane_tiles(qb)):
        qb_o[:, hd * LANES:(hd + 1) * LANES] = (_rotate(t, tb, B_ROPE // 2) * scale_b).astype(qb_o.dtype)
    lat = _rmsnorm_rows(p[:, _E_CKV:_E_KR], gkv_ref[...])
    lat_s[...] = lat
    kv = jnp.dot(lat.astype(BF16), wukv_ref[...], preferred_element_type=F32)
    kr = _rotate(p[:, _E_KR:_E_END], tb, B_ROPE // 2)
    kr_s[...] = kr[:, _B_ROPE_LANE:_B_ROPE_LANE + B_ROPE]
    tiles = _lane_tiles(kv)
    for hd in range(B_HEADS):
        kb_o[:, hd * LANES:(hd + 1) * LANES] = (tiles[hd] + kr).astype(kb_o.dtype)
        vb_o[:, hd * LANES:(hd + 1) * LANES] = (tiles[B_HEADS + hd] + ones_col).astype(vb_o.dtype)


def _pad_cols(w, width):
    return jnp.pad(w, ((0, 0), (0, width - w.shape[1])))


def _even_weights(w_in, w_uq, w_ukv):
    cuts = np.cumsum(EVEN_SPLIT)[:-1].tolist()
    qa, ka, va, qi, ki, wi, cq, ckv, kr = jnp.split(w_in, cuts, axis=1)
    kr = jnp.pad(kr, ((0, 0), (_B_ROPE_LANE, LANES - _B_ROPE_LANE - B_ROPE)))
    w = jnp.concatenate([qa, ka, va, qi, _pad_cols(ki, LANES), _pad_cols(wi, LANES), cq, ckv, kr], axis=1)
    d = w_in.shape[0]
    wuq = jnp.pad(w_uq.reshape(Q_RANK, B_HEADS, B_NOPE + B_ROPE),
                  ((0, 0), (0, 0), (0, LANES - B_NOPE - B_ROPE))).reshape(Q_RANK, B_HEADS * LANES)
    kvw = w_ukv.reshape(KV_RANK, B_HEADS, B_NOPE + B_VDIM)
    pad = lambda a: jnp.pad(a, ((0, 0), (0, 0), (0, LANES - a.shape[2]))).reshape(KV_RANK, B_HEADS * LANES)
    wukv = jnp.concatenate([pad(kvw[:, :, :B_NOPE]), pad(kvw[:, :, B_NOPE:])], axis=1)
    assert w.shape == (d, _E_END)
    return w.astype(BF16), wuq.astype(BF16), wukv.astype(BF16)


def _rope_tables(pos, rows, *, period, start, n_rot):
    half = n_rot // 2
    inv = ROPE_THETA ** (-jnp.arange(half, dtype=F32) * 2.0 / n_rot)
    ang = pos.astype(F32)[:, None] * inv[None, :]
    cos, sin = jnp.cos(ang), jnp.sin(ang)
    off = np.arange(LANES) % period - start
    idx = np.where((off >= 0) & (off < n_rot), off % half, 0)
    lower = (off >= 0) & (off < half)
    upper = (off >= half) & (off < n_rot)
    cos_t = jnp.where(lower | upper, cos[:, idx], 1.0)
    sin_up = jnp.where(upper, sin[:, idx], 0.0)
    sin_dn = jnp.where(lower, -sin[:, idx], 0.0)
    tabs = jnp.stack([cos_t, sin_up, sin_dn])
    return jnp.tile(tabs, (1, rows // pos.shape[0], 1))


def _even_proj(x, g, weights, g_bq, g_bkv, tables, *, seq_len, tm=512):
    m, d = x.shape
    w, wuq, wukv = weights
    tm = min(tm, m)
    table_rows = tables[0].shape[1]
    assert m % tm == 0 and table_rows % tm == 0 and (seq_len % tm == 0 or tm % seq_len == 0)
    tblocks = table_rows // tm
    row = lambda cols: pl.BlockSpec((tm, cols), lambda i: (i, 0))
    const = lambda a: pl.BlockSpec(a.shape, lambda i: (0,) * a.ndim)
    tspec = pl.BlockSpec((3, tm, LANES), lambda i: (0, i % tblocks, 0))
    g2, gq2, gkv2 = g.reshape(1, d), g_bq.reshape(1, Q_RANK), g_bkv.reshape(1, KV_RANK)
    kv_cols = A_KV_HEADS * HEAD_DIM
    outs = [(A_HEADS * LANES, BF16), (A_KV_HEADS * LANES, BF16), (A_KV_HEADS * LANES, BF16),
            (IDX_HEADS * IDX_DIM, BF16), (IDX_DIM, BF16), (IDX_HEADS, F32),
            (B_HEADS * LANES, BF16), (B_HEADS * LANES, BF16), (B_HEADS * LANES, BF16),
            (kv_cols, F32), (kv_cols, F32), (IDX_DIM, F32), (KV_RANK, F32), (B_ROPE, F32)]
    return pl.pallas_call(
        _even_proj_kernel,
        grid=(m // tm,),
        in_specs=[row(d), const(g2), const(w), const(gq2), const(wuq), const(gkv2), const(wukv),
                  tspec, tspec, tspec],
        out_specs=[row(c) for c, _ in outs],
        out_shape=[jax.ShapeDtypeStruct((m, c), t) for c, t in outs],
        compiler_params=_params("parallel"),
        name="even_proj",
    )(x, g2, w, gq2, wuq, gkv2, wukv, *tables)


_O_QC, _O_KC, _O_VC, _O_F, _O_QD, _O_KD, _O_VD, _O_END = 0, 512, 1024, 1536, 1664, 2176, 2688, 3200
FOX_BIAS_TERMS = 3


def _three_terms(x):
    hi = x.astype(BF16)
    r = x - hi.astype(F32)
    mid = r.astype(BF16)
    return hi, mid, (r - mid.astype(F32)).astype(BF16)


def _cumulative(logf, carry_s, c0_ref, restart):
    tm = logf.shape[0]

    @pl.when(restart)
    def _():
        carry_s[...] = jnp.broadcast_to(c0_ref[0], carry_s.shape)

    r = lax.broadcasted_iota(I32, (tm, tm), 0)
    cidx = lax.broadcasted_iota(I32, (tm, tm), 1)
    upto = jnp.where(cidx <= r, 1.0, 0.0).astype(BF16)
    c = carry_s[0:1, :] + sum(jnp.dot(upto, t, preferred_element_type=F32) for t in _three_terms(logf))
    carry_s[...] = jnp.broadcast_to(c[tm - 1:tm, :], carry_s.shape)
    return c


def _fox_key_bias(c):
    r = lax.broadcasted_iota(I32, (LANES, C_HEADS * LANES), 0)
    cidx = lax.broadcasted_iota(I32, (LANES, C_HEADS * LANES), 1)
    placed = 0.0
    for j, term in enumerate(_three_terms(-LOG2E * c)):
        put = jnp.where((cidx == r * LANES + HEAD_DIM + j) & (r < C_HEADS), 1.0, 0.0).astype(BF16)
        placed = placed + jnp.dot(term, put, preferred_element_type=F32)
    return placed


def _store_fox_keys(out_ref, kc, c):
    bias = _lane_tiles(_fox_key_bias(c))
    for p, pair in enumerate(_lane_tiles(kc)):
        for u, tile in enumerate(_split_pair(pair)):
            h = 2 * p + u
            out_ref[:, h * LANES:(h + 1) * LANES] = (tile + bias[h]).astype(out_ref.dtype)


def _odd_proj_kernel(x_ref, g_ref, w_ref, bf_ref, c0_ref,
                     qc_o, kc_o, vc_o, qd_o, kd_o, vd_o, kc_s, vc_s, kd_s, vd_s, logf_s, carry_s,
                     *, tiles_per_seq):
    h = _rmsnorm_rows(x_ref[...], g_ref[...]).astype(BF16)
    p = jnp.dot(h, w_ref[...], preferred_element_type=F32)
    scale = HEAD_DIM ** -0.5 * LOG2E
    ones_col = _one_hot_row(HEAD_DIM)

    f = p[:, _O_F:_O_QD] + bf_ref[...]
    logf = jnp.minimum(f, 0.0) - jnp.log1p(jnp.exp(-jnp.abs(f)))
    logf = jnp.where(_lane_index(logf.shape) < C_HEADS, logf, 0.0)
    logf_s[...] = logf[:, :C_HEADS]
    c = _cumulative(logf, carry_s, c0_ref, pl.program_id(0) % tiles_per_seq == 0)

    kc, vc, kd, vd = p[:, _O_KC:_O_VC], p[:, _O_VC:_O_F], p[:, _O_KD:_O_VD], p[:, _O_VD:_O_END]
    kc_s[...], vc_s[...], kd_s[...], vd_s[...] = kc, vc, kd, vd
    _store_heads(qc_o, p[:, _O_QC:_O_KC], scale=scale,
                 extra=_one_hot_row(*range(HEAD_DIM, HEAD_DIM + FOX_BIAS_TERMS)))
    _store_fox_keys(kc_o, kc, c)
    _store_heads(vc_o, vc, extra=ones_col)
    _store_heads(qd_o, p[:, _O_QD:_O_KD], scale=scale)
    _store_heads(kd_o, kd)
    _store_heads(vd_o, vd)


def _odd_weights(w_in):
    cuts = np.cumsum(ODD_SPLIT)[:-1].tolist()
    qc, kc, vc, fc, qd, kd, vd = jnp.split(w_in, cuts, axis=1)
    w = jnp.concatenate([qc, kc, vc, _pad_cols(fc, LANES), qd, kd, vd], axis=1)
    assert w.shape[1] == _O_END
    return w.astype(BF16)


def _odd_proj(x, g, w, b_f, c0, *, seq_len, tm=256):
    m, d = x.shape
    tm = min(tm, seq_len)
    assert seq_len % tm == 0
    tiles_per_seq = seq_len // tm
    row = lambda cols: pl.BlockSpec((tm, cols), lambda i: (i, 0))
    const = lambda a: pl.BlockSpec(a.shape, lambda i: (0,) * a.ndim)
    g2 = g.reshape(1, d)
    bf2 = _pad_cols(b_f.reshape(1, C_HEADS), LANES)
    wide, flat = C_HEADS * LANES, C_HEADS * HEAD_DIM
    outs = [(wide, BF16)] * 6 + [(flat, F32)] * 4 + [(C_HEADS, F32)]
    return pl.pallas_call(
        functools.partial(_odd_proj_kernel, tiles_per_seq=tiles_per_seq),
        grid=(m // tm,),
        in_specs=[row(d), const(g2), const(w), const(bf2),
                  pl.BlockSpec((1, 1, LANES), lambda i: (i // tiles_per_seq, 0, 0))],
        out_specs=[row(c) for c, _ in outs],
        out_shape=[jax.ShapeDtypeStruct((m, c), t) for c, t in outs],
        scratch_shapes=[pltpu.VMEM((8, LANES), F32)],
        compiler_params=_params("arbitrary"),
        name="odd_proj",
    )(x, g2, w, bf2, c0)


def _fox_past_kernel(k_ref, logf_ref, c0_ref, k_o, cend_o, carry_s, *, tiles_per_seq):
    logf = jnp.pad(logf_ref[...], ((0, 0), (0, LANES - C_HEADS)))
    c = _cumulative(logf, carry_s, c0_ref, pl.program_id(0) % tiles_per_seq == 0)
    _store_fox_keys(k_o, k_ref[...], c)
    cend_o[0] = c[c.shape[0] - 1:, :]


def _fox_past(k, logf, *, seq_len, tm=256):
    m = k.shape[0]
    nseq = m // seq_len
    tm = min(tm, seq_len)
    tiles_per_seq = seq_len // tm
    row = lambda cols: pl.BlockSpec((tm, cols), lambda i: (i, 0))
    per_seq = pl.BlockSpec((1, 1, LANES), lambda i: (i // tiles_per_seq, 0, 0))
    return pl.pallas_call(
        functools.partial(_fox_past_kernel, tiles_per_seq=tiles_per_seq),
        grid=(m // tm,),
        in_specs=[row(C_HEADS * HEAD_DIM), row(C_HEADS), per_seq],
        out_specs=[row(C_HEADS * LANES), per_seq],
        out_shape=[jax.ShapeDtypeStruct((m, C_HEADS * LANES), BF16),
                   jax.ShapeDtypeStruct((nseq, 1, LANES), F32)],
        scratch_shapes=[pltpu.VMEM((8, LANES), F32)],
        compiler_params=_params("arbitrary"),
        name="fox_past",
    )(k, logf, jnp.zeros((nseq, 1, LANES), F32))


def _head_lanes(parts, ones_col=False):
    bsz, s_len, heads = parts[0].shape[:3]
    parts = [p.astype(BF16) for p in parts]
    used = sum(p.shape[-1] for p in parts)
    if ones_col:
        parts.append(jnp.ones((bsz, s_len, heads, 1), BF16))
        used += 1
    parts.append(jnp.zeros((bsz, s_len, heads, LANES - used), BF16))
    return jnp.concatenate(parts, axis=-1).reshape(bsz, s_len, heads * LANES)


def _keys(past, new, sk_pad):
    rows = new if past is None else jnp.concatenate([past, new], axis=1)
    pad = sk_pad - rows.shape[1]
    return jnp.pad(rows, ((0, 0), (0, pad), (0, 0))) if pad else rows


def _tiles(s_len, past_len):
    tq = min(256, s_len)
    tk, group = (256, 4) if past_len == 0 else (128, 3)
    return tq, tk, group, _round_up(past_len + s_len, tk * group)


def _even_mixer(x, h_gain, tables, past, weights, g_bq, g_bkv, w_o):
    bsz, s_len, d = x.shape
    m = bsz * s_len
    past_len = 0 if past is None else past[0].shape[1]
    sk = past_len + s_len
    tq, tk, group, sk_pad = _tiles(s_len, past_len)
    outs = _even_proj(x.reshape(m, d), h_gain, weights, g_bq, g_bkv, tables, seq_len=s_len)
    qa, ka, va, qi, ki, wi, qb, kb, vb, ka_s, va_s, ki_s, lat_s, kr_s = [
        o.reshape(bsz, s_len, -1) for o in outs]
    new_rows = (ka_s.reshape(bsz, s_len, A_KV_HEADS, HEAD_DIM), va_s.reshape(bsz, s_len, A_KV_HEADS, HEAD_DIM),
                ki_s, lat_s, kr_s)
    if past is None:
        pa = (None,) * 5
    else:
        c_k, c_v, c_ki, c_lat, c_kr = past
        kv = _proj(c_lat.reshape(bsz * past_len, KV_RANK), jnp.ones((KV_RANK,), F32), weights[2], norm=False,
                   tm=past_len).reshape(bsz, past_len, 2 * B_HEADS, LANES)
        kr_t = jnp.pad(c_kr, ((0, 0), (0, 0), (_B_ROPE_LANE, LANES - _B_ROPE_LANE - B_ROPE)))[:, :, None, :]
        ones_t = jnp.zeros((LANES,), F32).at[HEAD_DIM].set(1.0)
        pa = (_head_lanes([c_k]), _head_lanes([c_v], ones_col=True), c_ki.astype(BF16),
              (kv[:, :, :B_HEADS] + kr_t).astype(BF16).reshape(bsz, past_len, -1),
              (kv[:, :, B_HEADS:] + ones_t).astype(BF16).reshape(bsz, past_len, -1))

    out_a = _dsa_attention(qa, _keys(pa[0], ka, sk_pad), _keys(pa[1], va, sk_pad), qi, _keys(pa[2], ki, sk_pad),
                           wi, q_off=past_len, sk=sk, tq=tq, tk=tk, group=group)
    out_b = _softmax_attention(qb, _keys(pa[3], kb, sk_pad), _keys(pa[4], vb, sk_pad),
                               heads=B_HEADS, mask_kind="chunk", q_off=past_len, sk=sk, tq=tq, tk=tk,
                               group=group, name="mla_attention")
    mix = jnp.concatenate([out_a, out_b], axis=-1).reshape(m, -1)
    y = _matmul_residual(mix, w_o, x.reshape(m, d)).reshape(bsz, s_len, d)
    return y, new_rows


def _odd_mixer(x, h_gain, past, w, b_f, w_o):
    bsz, s_len, d = x.shape
    m = bsz * s_len
    past_len = 0 if past is None else past[0].shape[1]
    sk = past_len + s_len
    tq, tk, group, sk_pad = _tiles(s_len, past_len)
    if past is None:
        pa = (None,) * 4
        c0 = jnp.zeros((bsz, 1, LANES), F32)
    else:
        c_k, c_v, c_logf, d_k, d_v = past
        kc_past, c0 = _fox_past(c_k.reshape(bsz * past_len, -1), c_logf.reshape(bsz * past_len, C_HEADS),
                                seq_len=past_len)
        pa = (kc_past.reshape(bsz, past_len, -1), _head_lanes([c_v], ones_col=True),
              _head_lanes([d_k]), _head_lanes([d_v]))
    outs = _odd_proj(x.reshape(m, d), h_gain, w, b_f, c0, seq_len=s_len)
    qc, kc, vc, qd, kd, vd, kc_s, vc_s, kd_s, vd_s, logf = [o.reshape(bsz, s_len, -1) for o in outs]
    heads = lambda a: a.reshape(bsz, s_len, -1, HEAD_DIM)
    new_rows = (heads(kc_s), heads(vc_s), logf, heads(kd_s), heads(vd_s))

    out_c = _softmax_attention(qc, _keys(pa[0], kc, sk_pad), _keys(pa[1], vc, sk_pad),
                               heads=C_HEADS, mask_kind="causal", q_off=past_len, sk=sk, tq=tq, tk=tk,
                               group=group, name="fox_attention")
    out_d = _sb_attention(qd, _keys(pa[2], kd, sk_pad), _keys(pa[3], vd, sk_pad),
                          q_off=past_len, sk=sk, tq=tq, tk=tk)
    mix = jnp.concatenate([out_c, out_d], axis=-1).reshape(m, -1)
    y = _matmul_residual(mix, w_o, x.reshape(m, d)).reshape(bsz, s_len, d)
    return y, new_rows


def _final_norm_kernel(x_ref, g_ref, o_ref):
    x = x_ref[...]
    ms = jnp.mean(x * x, axis=-1, keepdims=True)
    o_ref[...] = x * lax.rsqrt(ms + EPS) * g_ref[...]


def _final_norm(x, g, *, tm=512):
    m, d = x.shape
    tm = min(tm, m)
    return pl.pallas_call(
        _final_norm_kernel,
        grid=(m // tm,),
        in_specs=[pl.BlockSpec((tm, d), lambda i: (i, 0)), pl.BlockSpec((1, d), lambda i: (0, 0))],
        out_specs=pl.BlockSpec((tm, d), lambda i: (i, 0)),
        out_shape=jax.ShapeDtypeStruct((m, d), F32),
        compiler_params=_params("parallel"),
        name="final_norm",
    )(x, g.reshape(1, d).astype(F32))


def _trunk(x, q_pos, caches, params, even_w, odd_w):
    (g_mix, g_ffn, g_final, w_in_even, g_b_q, g_b_kv, w_b_uq, w_b_ukv, w_o_even,
     w_in_odd, b_forget, w_o_odd, w_up, w_conv, b_conv, w_down) = params
    bsz, s_len, d = x.shape
    depth = g_mix.shape[0]
    even_rows = [[] for _ in range(5)]
    odd_rows = [[] for _ in range(5)]
    conv_rows = []
    rows = max(s_len, min(512, bsz * s_len))
    tables = (_rope_tables(q_pos, rows, period=HEAD_DIM, start=0, n_rot=ROT_DIM),
              _rope_tables(q_pos, rows, period=IDX_DIM, start=0, n_rot=IDX_ROT),
              _rope_tables(q_pos, rows, period=LANES, start=_B_ROPE_LANE, n_rot=B_ROPE))
    for l in range(depth):
        j = l // 2
        if l % 2 == 0:
            past = None if caches is None else tuple(c[j] for c in caches[0:5])
            x, rows = _even_mixer(x, g_mix[l], tables, past, even_w[j], g_b_q[j], g_b_kv[j], w_o_even[j])
            for lst, r in zip(even_rows, rows):
                lst.append(r)
        else:
            past = None if caches is None else tuple(c[j] for c in caches[5:10])
            x, rows = _odd_mixer(x, g_mix[l], past, odd_w[j], b_forget[j], w_o_odd[j])
            for lst, r in zip(odd_rows, rows):
                lst.append(r)
        state = jnp.zeros((bsz, CONV_W - 1, D_FF), F32) if caches is None else caches[10][l]
        y, new_buf = _conv_ffn(x.reshape(bsz * s_len, d), g_ffn[l], w_up[l], w_conv[l], b_conv[l], w_down[l],
                               state, seq_len=s_len, tm=1024)
        x = y.reshape(bsz, s_len, d)
        conv_rows.append(new_buf)
    out = _final_norm(x.reshape(bsz * s_len, d), g_final).reshape(bsz, s_len, d)
    states = [jnp.stack(r, axis=0) for r in even_rows + odd_rows] + [jnp.stack(conv_rows, axis=0)]
    return out, states


def kernel(x_prompt, x_sample, cache_a_k, cache_a_v, cache_a_idx_k, cache_b_latent, cache_b_rope,
           cache_c_k, cache_c_v, cache_c_logf, cache_d_k, cache_d_v, state_ffn_conv,
           g_mix, g_ffn, g_final, w_in_even, g_b_q, g_b_kv, w_b_uq, w_b_ukv, w_o_even,
           w_in_odd, b_forget, w_o_odd, w_up, w_conv, b_conv, w_down):
    params = (g_mix, g_ffn, g_final, w_in_even, g_b_q, g_b_kv, w_b_uq, w_b_ukv, w_o_even,
              w_in_odd, b_forget, w_o_odd, w_up, w_conv, b_conv, w_down)
    caches = (cache_a_k, cache_a_v, cache_a_idx_k, cache_b_latent, cache_b_rope,
              cache_c_k, cache_c_v, cache_c_logf, cache_d_k, cache_d_v, state_ffn_conv)
    past_len = cache_a_k.shape[2]
    pos_prompt = jnp.arange(x_prompt.shape[1], dtype=I32)
    pos_sample = past_len + jnp.arange(x_sample.shape[1], dtype=I32)
    even_w = [_even_weights(w_in_even[j], w_b_uq[j], w_b_ukv[j]) for j in range(w_in_even.shape[0])]
    odd_w = [_odd_weights(w_in_odd[j]) for j in range(w_in_odd.shape[0])]
    y_prompt, p_states = _trunk(x_prompt, pos_prompt, None, params, even_w, odd_w)
    y_sample, s_states = _trunk(x_sample, pos_sample, caches, params, even_w, odd_w)
    return (y_prompt, y_sample, *p_states, *s_states)
```

```python
import functools

import jax
import jax.numpy as jnp
import numpy as np
from jax import lax
from jax.experimental import pallas as pl
from jax.experimental.pallas import tpu as pltpu

F32 = jnp.float32
BF16 = jnp.bfloat16
I32 = jnp.int32

CHUNK = 64
ROPE_THETA = 500000.0
EPS = 1e-6
HEAD_DIM = 64
ROT_DIM = HEAD_DIM // 4
A_HEADS = 8
A_KV_HEADS = 2
IDX_HEADS = 8
IDX_DIM = 32
IDX_ROT = IDX_DIM // 4
TOPK_MAX = 256
B_HEADS = 8
Q_RANK = 256
KV_RANK = 128
B_NOPE = 64
B_ROPE = 32
B_VDIM = 64
C_HEADS = 8
D_HEADS = 8
D_FF = 2816
CONV_W = 3

EVEN_SPLIT = [A_HEADS * HEAD_DIM, A_KV_HEADS * HEAD_DIM, A_KV_HEADS * HEAD_DIM,
              IDX_HEADS * IDX_DIM, IDX_DIM, IDX_HEADS, Q_RANK, KV_RANK, B_ROPE]
ODD_SPLIT = [C_HEADS * HEAD_DIM] * 3 + [C_HEADS] + [D_HEADS * HEAD_DIM] * 3

LANES = 128
VMEM_LIMIT_BYTES = 56 * 1024 * 1024
MASKED = -1e30
INT_MIN = -2 ** 31
LOG2E = 1.4426950408889634
F32_EXP2_UNDERFLOW = -152.0

_NT = (((1,), (1,)), ((), ()))


def _params(*sem):
    return pltpu.CompilerParams(dimension_semantics=sem, vmem_limit_bytes=VMEM_LIMIT_BYTES)


def _round_up(n, m):
    return (n + m - 1) // m * m


def _chunk_of(pos):
    return jnp.right_shift(pos, CHUNK.bit_length() - 1)


def _proj_kernel(x_ref, g_ref, w_ref, *out_refs, norm, emit_h):
    x = x_ref[...]
    if norm:
        ms = jnp.mean(x * x, axis=-1, keepdims=True)
        x = x * lax.rsqrt(ms + EPS) * g_ref[...]
    out_refs[0][...] = jnp.dot(x.astype(BF16), w_ref[...], preferred_element_type=F32)
    if emit_h:
        out_refs[1][...] = x


def _proj(x, g, w, *, norm=True, emit_h=False, tm=512):
    m, k = x.shape
    n = w.shape[1]
    tm = min(tm, m)
    assert m % tm == 0
    out_shape = [jax.ShapeDtypeStruct((m, n), F32)]
    out_specs = [pl.BlockSpec((tm, n), lambda i: (i, 0))]
    if emit_h:
        out_shape.append(jax.ShapeDtypeStruct((m, k), F32))
        out_specs.append(pl.BlockSpec((tm, k), lambda i: (i, 0)))
    res = pl.pallas_call(
        functools.partial(_proj_kernel, norm=norm, emit_h=emit_h),
        grid=(m // tm,),
        in_specs=[pl.BlockSpec((tm, k), lambda i: (i, 0)),
                  pl.BlockSpec((1, k), lambda i: (0, 0)),
                  pl.BlockSpec((k, n), lambda i: (0, 0))],
        out_specs=out_specs,
        out_shape=out_shape,
        compiler_params=_params("parallel"),
        name="proj",
    )(x, g.reshape(1, k).astype(F32), w.astype(BF16))
    return res if emit_h else res[0]


def _matmul_residual_kernel(a_ref, w_ref, r_ref, o_ref):
    o_ref[...] = r_ref[...] + jnp.dot(a_ref[...], w_ref[...], preferred_element_type=F32)


def _matmul_residual(a, w, res, *, tm=512):
    m, k = a.shape
    n = w.shape[1]
    tm = min(tm, m)
    assert m % tm == 0
    return pl.pallas_call(
        _matmul_residual_kernel,
        grid=(m // tm,),
        in_specs=[pl.BlockSpec((tm, k), lambda i: (i, 0)),
                  pl.BlockSpec((k, n), lambda i: (0, 0)),
                  pl.BlockSpec((tm, n), lambda i: (i, 0))],
        out_specs=pl.BlockSpec((tm, n), lambda i: (i, 0)),
        out_shape=jax.ShapeDtypeStruct((m, n), F32),
        compiler_params=_params("parallel"),
        name="matmul_residual",
    )(a, w.astype(BF16), res)


def _conv_ffn_kernel(x_ref, g_ref, wg_ref, wu_ref, wc_ref, bc_ref, wd_ref, st_ref,
                     y_ref, ns_ref, h_s, acc_s, carry_s, *, tiles_per_seq, tm, tf):
    i = pl.program_id(0)
    c = pl.program_id(1)
    nc = pl.num_programs(1)
    cols = pl.ds(pl.multiple_of(c * tf, tf), tf)

    @pl.when(c == 0)
    def _():
        x = x_ref[...]
        ms = jnp.mean(x * x, axis=-1, keepdims=True)
        h_s[...] = (x * lax.rsqrt(ms + EPS) * g_ref[...]).astype(BF16)
        acc_s[...] = jnp.zeros_like(acc_s)

    h = h_s[...]
    gate = jnp.dot(h, wg_ref[...], preferred_element_type=F32)
    up = jnp.dot(h, wu_ref[...], preferred_element_type=F32)

    @pl.when(i % tiles_per_seq == 0)
    def _():
        carry_s[c] = st_ref[0, :, cols]

    prev = carry_s[c]
    row = lax.broadcasted_iota(I32, gate.shape, 0)
    g1 = pltpu.roll(gate, 1, 0)
    g1 = jnp.where(row == 0, prev[1:2, :], g1)
    g2 = pltpu.roll(gate, 2, 0)
    g2 = jnp.where(row == 0, prev[0:1, :], jnp.where(row == 1, prev[1:2, :], g2))
    wc = wc_ref[...]
    gc = bc_ref[...] + g2 * wc[0:1, :]
    gc = gc + g1 * wc[1:2, :]
    gc = gc + gate * wc[2:3, :]
    act = gc * jax.nn.sigmoid(gc) * up
    acc_s[...] += jnp.dot(act.astype(BF16), wd_ref[...], preferred_element_type=F32)

    tail = gate[tm - 2:tm, :]
    carry_s[c] = tail
    ns_ref[0, :, cols] = tail

    @pl.when(c == nc - 1)
    def _():
        y_ref[...] = x_ref[...] + acc_s[...]


def _conv_ffn(x, g, w_up, w_conv, b_conv, w_down, state, *, seq_len, tm, tf=256):
    m, d = x.shape
    nseq = m // seq_len
    tm = min(tm, seq_len)
    assert seq_len % tm == 0 and D_FF % tf == 0
    tiles_per_seq = seq_len // tm
    nc = D_FF // tf
    wg = w_up[:, :D_FF].astype(BF16)
    wu = w_up[:, D_FF:].astype(BF16)
    state_spec = pl.BlockSpec((1, CONV_W - 1, D_FF), lambda i, c: (i // tiles_per_seq, 0, 0))
    y, ns = pl.pallas_call(
        functools.partial(_conv_ffn_kernel, tiles_per_seq=tiles_per_seq, tm=tm, tf=tf),
        grid=(m // tm, nc),
        in_specs=[pl.BlockSpec((tm, d), lambda i, c: (i, 0)),
                  pl.BlockSpec((1, d), lambda i, c: (0, 0)),
                  pl.BlockSpec((d, tf), lambda i, c: (0, c)),
                  pl.BlockSpec((d, tf), lambda i, c: (0, c)),
                  pl.BlockSpec((CONV_W, tf), lambda i, c: (0, c)),
                  pl.BlockSpec((1, tf), lambda i, c: (0, c)),
                  pl.BlockSpec((tf, d), lambda i, c: (c, 0)),
                  state_spec],
        out_specs=[pl.BlockSpec((tm, d), lambda i, c: (i, 0)), state_spec],
        out_shape=[jax.ShapeDtypeStruct((m, d), F32),
                   jax.ShapeDtypeStruct((nseq, CONV_W - 1, D_FF), F32)],
        scratch_shapes=[pltpu.VMEM((tm, d), BF16),
                        pltpu.VMEM((tm, d), F32),
                        pltpu.VMEM((nc, CONV_W - 1, tf), F32)],
        compiler_params=_params("arbitrary", "arbitrary"),
        name="conv_ffn",
    )(x, g.reshape(1, d).astype(F32), wg, wu, w_conv.astype(F32), b_conv.reshape(1, D_FF).astype(F32),
      w_down.astype(BF16), state.astype(F32))
    return y, ns


def _attn_call(kernel, inputs, in_specs, out_cols, *, bsz, sq, tq, scratch=(), name):
    return pl.pallas_call(
        kernel,
        grid=(bsz, sq // tq),
        in_specs=in_specs,
        out_specs=pl.BlockSpec((1, tq, out_cols), lambda b, i: (b, i, 0)),
        out_shape=jax.ShapeDtypeStruct((bsz, sq, out_cols), BF16),
        scratch_shapes=list(scratch),
        compiler_params=_params("parallel", "arbitrary"),
        name=name,
    )(*inputs)


def _q_spec(tq, cols):
    return pl.BlockSpec((1, tq, cols), lambda b, i: (b, i, 0))


def _k_spec(rows, cols):
    return pl.BlockSpec((1, rows, cols), lambda b, i: (b, 0, 0))


def _tile_geometry(i, *, tq, span, q_off):
    q_lo = q_off + i * tq
    return q_lo, q_lo // span


def _lane_tiles(x):
    return [x[:, j * LANES:(j + 1) * LANES] for j in range(x.shape[1] // LANES)]


def _visible(kind, q_lo, ks, *, tq, tk, sk):
    q_pos = q_lo + lax.broadcasted_iota(I32, (tq, 1), 0)
    k_pos = ks + lax.broadcasted_iota(I32, (1, tk), 1)
    if kind == "chunk":
        return (_chunk_of(k_pos) <= _chunk_of(q_pos)) & (k_pos < sk)
    if kind == "causal":
        return k_pos <= q_pos
    assert kind == "strict"
    return k_pos < q_pos


HEADS_PER_STEP = 2


def _span_cases(i, body, *, tq, span, q_off, n_tiles):
    g_own = (q_off + i * tq) // span
    for g in sorted({(q_off + t * tq) // span for t in range(n_tiles)}):
        pl.when(g_own == g)(functools.partial(body, g + 1))


def _softmax_heads(qs, k_ref, v_ref, s_scr, kcols, vcols, *, tq, tk, blocks, own_from, own_mask,
                   bias_ref=None):
    row_max = []
    for slot, (q, kcol) in enumerate(zip(qs, kcols)):
        mx = jnp.full((tq, LANES), MASKED, F32)
        for b in range(blocks):
            ks = b * tk
            s = lax.dot_general(q, k_ref[0, ks:ks + tk, kcol], _NT, preferred_element_type=F32)
            if bias_ref is not None:
                s = s + bias_ref[:, ks:ks + tk]
            if own_mask is not None and b >= own_from:
                s = jnp.where(own_mask(ks), s, MASKED)
            s_scr[slot, :, ks:ks + tk] = s
            for t in _lane_tiles(s):
                mx = jnp.maximum(mx, t)
        row_max.append(jnp.broadcast_to(jnp.max(mx, axis=1, keepdims=True), (tq, LANES)))
    outs = []
    for slot, (m, vcol) in enumerate(zip(row_max, vcols)):
        acc = jnp.zeros((tq, LANES), F32)
        for b in range(blocks):
            ks = b * tk
            p = jnp.concatenate([jnp.exp2(t - m) for t in _lane_tiles(s_scr[slot, :, ks:ks + tk])], axis=1)
            acc = acc + jnp.dot(p.astype(BF16), v_ref[0, ks:ks + tk, vcol], preferred_element_type=F32)
        outs.append(acc[:, :HEAD_DIM] / acc[:, HEAD_DIM:HEAD_DIM + 1])
    return outs


def _lane_tile_at(index):
    return pl.ds(pl.multiple_of(index * LANES, LANES), LANES)


def _softmax_attn_kernel(q_ref, k_ref, v_ref, o_ref, s_scr, *, heads, mask_kind, tq, tk, group, q_off, sk,
                         n_tiles):
    i = pl.program_id(1)
    own_mask = functools.partial(_visible, mask_kind, q_off + i * tq, tq=tq, tk=tk, sk=sk)

    def attend(n_spans):
        def step(hp, _):
            cols = [_lane_tile_at(hp * HEADS_PER_STEP + u) for u in range(HEADS_PER_STEP)]
            outs = _softmax_heads([q_ref[0, :, c] for c in cols], k_ref, v_ref, s_scr, cols, cols,
                                  tq=tq, tk=tk, blocks=n_spans * group, own_from=(n_spans - 1) * group,
                                  own_mask=own_mask)
            o_ref[0, :, _lane_tile_at(hp)] = jnp.concatenate(outs, axis=1).astype(o_ref.dtype)
            return 0
        lax.fori_loop(0, heads // HEADS_PER_STEP, step, 0)

    _span_cases(i, attend, tq=tq, span=tk * group, q_off=q_off, n_tiles=n_tiles)


def _check_tiling(sk_pad, *, tq, tk, group, q_off):
    assert (tk * group) % tq == 0 and q_off % tq == 0 and sk_pad % (tk * group) == 0 and tk % LANES == 0
    assert HEADS_PER_STEP * HEAD_DIM == LANES


def _softmax_attention(q, k, v, *, heads, mask_kind, q_off, sk, tq, tk, group, name):
    bsz, sq, _ = q.shape
    sk_pad = k.shape[1]
    _check_tiling(sk_pad, tq=tq, tk=tk, group=group, q_off=q_off)
    kern = functools.partial(_softmax_attn_kernel, heads=heads, mask_kind=mask_kind,
                             tq=tq, tk=tk, group=group, q_off=q_off, sk=sk, n_tiles=sq // tq)
    cols = heads * LANES
    return _attn_call(kern, (q, k, v), [_q_spec(tq, cols), _k_spec(sk_pad, cols), _k_spec(sk_pad, cols)],
                      heads * HEAD_DIM, bsz=bsz, sq=sq, tq=tq,
                      scratch=(pltpu.VMEM((HEADS_PER_STEP, tq, sk_pad), F32),), name=name)


def _split_bf16(x):
    hi = x.astype(BF16)
    lo = (x - hi.astype(F32)).astype(BF16)
    return hi, lo


SB_HEADS_PER_LOOP = 2


def _sb_kernel(q_ref, k_ref, v_ref, o_ref, *, tq, tk, q_off, sk):
    q_lo, n_before = _tile_geometry(pl.program_id(1), tq=tq, span=tk, q_off=q_off)
    r = lax.broadcasted_iota(I32, (tk, tk), 0)
    cidx = lax.broadcasted_iota(I32, (tk, tk), 1)
    later = jnp.where(r > cidx, 1.0, 0.0).astype(BF16)

    def block(q, h, kb, run, acc, mask):
        ks = pl.multiple_of(kb * tk, tk)
        k = k_ref[0, pl.ds(ks, tk), h * LANES:(h + 1) * LANES]
        v = v_ref[0, pl.ds(ks, tk), h * LANES:(h + 1) * LANES]
        z = lax.dot_general(q, k, _NT, preferred_element_type=F32)
        log_beta = jnp.minimum(z, 0.0) - jnp.log2(1.0 + jnp.exp2(-jnp.abs(z)))
        log_1m = log_beta - z
        if mask is not None:
            log_1m = jnp.where(mask, log_1m, 0.0)
        hi, lo = _split_bf16(log_1m)
        after = (jnp.dot(hi, later, preferred_element_type=F32)
                 + jnp.dot(lo, later, preferred_element_type=F32))
        a = jnp.exp2(log_beta + after + run)
        if mask is not None:
            a = jnp.where(mask, a, 0.0)
        acc = acc + jnp.dot(a.astype(BF16), v, preferred_element_type=F32)
        run = run + after[:, 0:1] + log_1m[:, 0:1]
        return run, acc

    own = _visible("strict", q_lo, n_before * tk, tq=tq, tk=tk, sk=sk)
    for h0 in range(0, D_HEADS, SB_HEADS_PER_LOOP):
        hs = range(h0, h0 + SB_HEADS_PER_LOOP)
        qs = [q_ref[0, :, h * LANES:(h + 1) * LANES] for h in hs]
        state = []
        for h, q in zip(hs, qs):
            state.extend(block(q, h, n_before, jnp.zeros((tq, 1), F32), jnp.zeros((tq, LANES), F32), own))

        def alive(state):
            top = functools.reduce(jnp.maximum, state[0::2])
            return jnp.max(top) > F32_EXP2_UNDERFLOW

        def cond(c):
            return (c[0] < n_before) & c[1]

        def body(c, hs=hs, qs=qs):
            j, _, state = c
            new = []
            for n, (h, q) in enumerate(zip(hs, qs)):
                new.extend(block(q, h, n_before - 1 - j, state[2 * n], state[2 * n + 1], None))
            return j + 1, alive(new), tuple(new)

        _, _, state = lax.while_loop(cond, body, (jnp.int32(0), alive(state), tuple(state)))
        for n, h in enumerate(hs):
            o_ref[0, :, h * HEAD_DIM:(h + 1) * HEAD_DIM] = state[2 * n + 1][:, :HEAD_DIM].astype(o_ref.dtype)


def _sb_attention(q, k, v, *, q_off, sk, tq, tk):
    bsz, sq, _ = q.shape
    sk_pad = k.shape[1]
    _check_tiling(sk_pad, tq=tq, tk=tk, group=1, q_off=q_off)
    kern = functools.partial(_sb_kernel, tq=tq, tk=tk, q_off=q_off, sk=sk)
    cols = D_HEADS * LANES
    return _attn_call(kern, (q, k, v), [_q_spec(tq, cols), _k_spec(sk_pad, cols), _k_spec(sk_pad, cols)],
                      D_HEADS * HEAD_DIM, bsz=bsz, sq=sq, tq=tq, name="sb_attention")


def _sortable_key(score):
    bits = lax.bitcast_convert_type(score, I32)
    return jnp.where(bits < 0, bits ^ 0x7FFFFFFF, bits)


KEY_OF_NEG_INF = -0x7F800001


def _dsa_kernel(q_ref, k_ref, v_ref, qi_ref, ki_ref, wi_ref, o_ref, key_s, bias_s, s_scr, w_s,
                *, tq, tk, group, q_off, sk, topk, n_tiles):
    i = pl.program_id(1)
    nt = tk // LANES
    own_visible = functools.partial(_visible, "chunk", q_off + i * tq, tq=tq, tk=tk, sk=sk)
    kf = float(topk)

    for h in range(IDX_HEADS):
        w_s[h] = jnp.broadcast_to(wi_ref[0, :, h:h + 1], (tq, LANES))

    def select_and_attend(n_spans):
        blocks, own_from = n_spans * group, (n_spans - 1) * group
        starts = [b * tk for b in range(blocks)]

        def own(ks, value, masked):
            return jnp.where(own_visible(ks), value, masked) if ks >= own_from * tk else value

        qis = [qi_ref[0, :, h * IDX_DIM:(h + 1) * IDX_DIM] for h in range(IDX_HEADS)]
        for ks in starts:
            ki = ki_ref[0, ks:ks + tk, :]
            total = jnp.zeros((tq, tk), F32)
            for h in range(IDX_HEADS):
                sc = lax.dot_general(qis[h], ki, _NT, preferred_element_type=F32)
                total = total + jnp.maximum(sc, 0.0) * jnp.concatenate([w_s[h]] * nt, axis=1)
            key_s[:, ks:ks + tk] = _sortable_key(own(ks, total, -jnp.inf))

        def count(hit):
            cnt = jnp.zeros((tq, LANES), F32)
            for ks in starts:
                for t in _lane_tiles(key_s[:, ks:ks + tk]):
                    cnt = cnt + jnp.where(hit(t), 1.0, 0.0)
            return jnp.sum(cnt, axis=1, keepdims=True)

        def count_ge(cand):
            cand = jnp.broadcast_to(cand, (tq, LANES))
            return count(lambda t: t >= cand)

        lowest = jnp.full((tq, 1), INT_MIN, I32)
        zero = jnp.zeros((tq, 1), I32)
        n_zero = count_ge(zero)
        ok = n_zero >= kf
        thr = jnp.where(ok, zero, lowest)
        n_thr = jnp.where(ok, n_zero, float(blocks * tk))

        def bit_step(it, c):
            thr, n_thr = c
            cand = thr | jnp.left_shift(jnp.int32(1), 30 - it)
            n_cand = count_ge(cand)
            ok = n_cand >= kf
            return jnp.where(ok, cand, thr), jnp.where(ok, n_cand, n_thr)

        thr, n_thr = lax.fori_loop(0, 31, bit_step, (thr, n_thr))
        thr_b = jnp.broadcast_to(thr, (tq, LANES))

        tied = jnp.max(jnp.where(thr > KEY_OF_NEG_INF, n_thr, 0.0)) > kf

        @pl.when(jnp.logical_not(tied))
        def _():
            for ks in starts:
                tiles = _lane_tiles(key_s[:, ks:ks + tk])
                bias = jnp.concatenate([jnp.where(t >= thr_b, 0.0, MASKED) for t in tiles], axis=1)
                bias_s[:, ks:ks + tk] = own(ks, bias, MASKED)

        @pl.when(tied)
        def _():
            r = lax.broadcasted_iota(I32, (tk, tk), 0)
            cidx = lax.broadcasted_iota(I32, (tk, tk), 1)
            upto = jnp.where(r <= cidx, 1.0, 0.0).astype(BF16)
            need = kf - count(lambda t: t > thr_b)
            seen = jnp.zeros((tq, 1), F32)
            for ks in starts:
                key = key_s[:, ks:ks + tk]
                tie = jnp.where(key == thr, 1.0, 0.0)
                rank = seen + jnp.dot(tie.astype(BF16), upto, preferred_element_type=F32)
                sel = (key > thr) | ((key == thr) & (rank <= need))
                bias_s[:, ks:ks + tk] = own(ks, jnp.where(sel, 0.0, MASKED), MASKED)
                seen = seen + jnp.sum(tie, axis=1, keepdims=True)

        def step(hp, _):
            heads = [hp * HEADS_PER_STEP + u for u in range(HEADS_PER_STEP)]
            kv = [_lane_tile_at(hd // (A_HEADS // A_KV_HEADS)) for hd in heads]
            outs = _softmax_heads([q_ref[0, :, _lane_tile_at(hd)] for hd in heads], k_ref, v_ref, s_scr,
                                  kv, kv, tq=tq, tk=tk, blocks=blocks, own_from=own_from, own_mask=None,
                                  bias_ref=bias_s)
            o_ref[0, :, _lane_tile_at(hp)] = jnp.concatenate(outs, axis=1).astype(o_ref.dtype)
            return 0
        lax.fori_loop(0, A_HEADS // HEADS_PER_STEP, step, 0)

    _span_cases(i, select_and_attend, tq=tq, span=tk * group, q_off=q_off, n_tiles=n_tiles)


def _dsa_attention(q, k, v, qi, ki, wi, *, q_off, sk, tq, tk, group):
    bsz, sq, _ = q.shape
    sk_pad = k.shape[1]
    _check_tiling(sk_pad, tq=tq, tk=tk, group=group, q_off=q_off)
    topk = min(TOPK_MAX, sk // 4)
    kern = functools.partial(_dsa_kernel, tq=tq, tk=tk, group=group, q_off=q_off, sk=sk, topk=topk,
                             n_tiles=sq // tq)
    kv_cols = A_KV_HEADS * LANES
    return _attn_call(kern, (q, k, v, qi, ki, wi),
                      [_q_spec(tq, A_HEADS * LANES), _k_spec(sk_pad, kv_cols), _k_spec(sk_pad, kv_cols),
                       _q_spec(tq, IDX_HEADS * IDX_DIM), _k_spec(sk_pad, IDX_DIM), _q_spec(tq, IDX_HEADS)],
                      A_HEADS * HEAD_DIM, bsz=bsz, sq=sq, tq=tq,
                      scratch=(pltpu.VMEM((tq, sk_pad), I32), pltpu.VMEM((tq, sk_pad), F32),
                               pltpu.VMEM((HEADS_PER_STEP, tq, sk_pad), F32),
                               pltpu.VMEM((IDX_HEADS, tq, LANES), F32)),
                      name="dsa_attention")


def _rmsnorm_rows(x, g):
    ms = jnp.mean(x * x, axis=-1, keepdims=True)
    return x * lax.rsqrt(ms + EPS) * g


def _lane_index(shape):
    return lax.broadcasted_iota(I32, shape, len(shape) - 1)


def _one_hot_row(*lanes):
    lane = _lane_index((1, LANES))
    hit = functools.reduce(jnp.logical_or, [lane == l for l in lanes])
    return jnp.where(hit, 1.0, 0.0)


def _rotate(tile, tables, half):
    cos, sin_up, sin_dn = tables
    return tile * cos + pltpu.roll(tile, half, 1) * sin_up + pltpu.roll(tile, LANES - half, 1) * sin_dn


def _split_pair(pair):
    low = _lane_index(pair.shape) < HEAD_DIM
    return jnp.where(low, pair, 0.0), jnp.where(low, pltpu.roll(pair, HEAD_DIM, 1), 0.0)


def _store_heads(out_ref, compact, *, scale=None, extra=None):
    for p, pair in enumerate(_lane_tiles(compact)):
        for u, tile in enumerate(_split_pair(pair)):
            if scale is not None:
                tile = tile * scale
            if extra is not None:
                tile = tile + extra
            h = 2 * p + u
            out_ref[:, h * LANES:(h + 1) * LANES] = tile.astype(out_ref.dtype)


_E_QA, _E_KA, _E_VA, _E_QI, _E_KI, _E_WI, _E_CQ, _E_CKV, _E_KR, _E_END = (
    0, 512, 640, 768, 1024, 1152, 1280, 1536, 1664, 1792)
_B_ROPE_LANE = B_NOPE


def _even_proj_kernel(x_ref, g_ref, w_ref, gq_ref, wuq_ref, gkv_ref, wukv_ref, ta_ref, ti_ref, tb_ref,
                      qa_o, ka_o, va_o, qi_o, ki_o, wi_o, qb_o, kb_o, vb_o,
                      ka_s, va_s, ki_s, lat_s, kr_s):
    h = _rmsnorm_rows(x_ref[...], g_ref[...]).astype(BF16)
    p = jnp.dot(h, w_ref[...], preferred_element_type=F32)
    ta = (ta_ref[0], ta_ref[1], ta_ref[2])
    ti = (ti_ref[0], ti_ref[1], ti_ref[2])
    tb = (tb_ref[0], tb_ref[1], tb_ref[2])
    ones_col = _one_hot_row(HEAD_DIM)

    qa = jnp.concatenate([_rotate(t, ta, ROT_DIM // 2) for t in _lane_tiles(p[:, _E_QA:_E_KA])], axis=1)
    _store_heads(qa_o, qa, scale=HEAD_DIM ** -0.5 * LOG2E)
    ka = _rotate(p[:, _E_KA:_E_VA], ta, ROT_DIM // 2)
    ka_s[...] = ka
    _store_heads(ka_o, ka)
    va = p[:, _E_VA:_E_QI]
    va_s[...] = va
    _store_heads(va_o, va, extra=ones_col)
    qi = jnp.concatenate([_rotate(t, ti, IDX_ROT // 2) for t in _lane_tiles(p[:, _E_QI:_E_KI])], axis=1)
    qi_o[...] = (qi * IDX_DIM ** -0.5).astype(qi_o.dtype)
    ki = _rotate(p[:, _E_KI:_E_WI], ti, IDX_ROT // 2)[:, :IDX_DIM]
    ki_s[...] = ki
    ki_o[...] = ki.astype(ki_o.dtype)
    wi_o[...] = p[:, _E_WI:_E_WI + IDX_HEADS] * IDX_HEADS ** -0.5

    cq = _rmsnorm_rows(p[:, _E_CQ:_E_CKV], gq_ref[...]).astype(BF16)
    qb = jnp.dot(cq, wuq_ref[...], preferred_element_type=F32)
    scale_b = (B_NOPE + B_ROPE) ** -0.5 * LOG2E
    for hd, t in enumerate(_lane_tiles(qb)):
        qb_o[:, hd * LANES:(hd + 1) * LANES] = (_rotate(t, tb, B_ROPE // 2) * scale_b).astype(qb_o.dtype)
    lat = _rmsnorm_rows(p[:, _E_CKV:_E_KR], gkv_ref[...])
    lat_s[...] = lat
    kv = jnp.dot(lat.astype(BF16), wukv_ref[...], preferred_element_type=F32)
    kr = _rotate(p[:, _E_KR:_E_END], tb, B_ROPE // 2)
    kr_s[...] = kr[:, _B_ROPE_LANE:_B_ROPE_LANE + B_ROPE]
    tiles = _lane_tiles(kv)
    for hd in range(B_HEADS):
        kb_o[:, hd * LANES:(hd + 1) * LANES] = (tiles[hd] + kr).astype(kb_o.dtype)
        vb_o[:, hd * LANES:(hd + 1) * LANES] = (tiles[B_HEADS + hd] + ones_col).astype(vb_o.dtype)


def _pad_cols(w, width):
    return jnp.pad(w, ((0, 0), (0, width - w.shape[1])))


def _even_weights(w_in, w_uq, w_ukv):
    cuts = np.cumsum(EVEN_SPLIT)[:-1].tolist()
    qa, ka, va, qi, ki, wi, cq, ckv, kr = jnp.split(w_in, cuts, axis=1)
    kr = jnp.pad(kr, ((0, 0), (_B_ROPE_LANE, LANES - _B_ROPE_LANE - B_ROPE)))
    w = jnp.concatenate([qa, ka, va, qi, _pad_cols(ki, LANES), _pad_cols(wi, LANES), cq, ckv, kr], axis=1)
    d = w_in.shape[0]
    wuq = jnp.pad(w_uq.reshape(Q_RANK, B_HEADS, B_NOPE + B_ROPE),
                  ((0, 0), (0, 0), (0, LANES - B_NOPE - B_ROPE))).reshape(Q_RANK, B_HEADS * LANES)
    kvw = w_ukv.reshape(KV_RANK, B_HEADS, B_NOPE + B_VDIM)
    pad = lambda a: jnp.pad(a, ((0, 0), (0, 0), (0, LANES - a.shape[2]))).reshape(KV_RANK, B_HEADS * LANES)
    wukv = jnp.concatenate([pad(kvw[:, :, :B_NOPE]), pad(kvw[:, :, B_NOPE:])], axis=1)
    assert w.shape == (d, _E_END)
    return w.astype(BF16), wuq.astype(BF16), wukv.astype(BF16)


def _rope_tables(pos, rows, *, period, start, n_rot):
    half = n_rot // 2
    inv = ROPE_THETA ** (-jnp.arange(half, dtype=F32) * 2.0 / n_rot)
    ang = pos.astype(F32)[:, None] * inv[None, :]
    cos, sin = jnp.cos(ang), jnp.sin(ang)
    off = np.arange(LANES) % period - start
    idx = np.where((off >= 0) & (off < n_rot), off % half, 0)
    lower = (off >= 0) & (off < half)
    upper = (off >= half) & (off < n_rot)
    cos_t = jnp.where(lower | upper, cos[:, idx], 1.0)
    sin_up = jnp.where(upper, sin[:, idx], 0.0)
    sin_dn = jnp.where(lower, -sin[:, idx], 0.0)
    tabs = jnp.stack([cos_t, sin_up, sin_dn])
    return jnp.tile(tabs, (1, rows // pos.shape[0], 1))


def _even_proj(x, g, weights, g_bq, g_bkv, tables, *, seq_len, tm=512):
    m, d = x.shape
    w, wuq, wukv = weights
    tm = min(tm, m)
    table_rows = tables[0].shape[1]
    assert m % tm == 0 and table_rows % tm == 0 and (seq_len % tm == 0 or tm % seq_len == 0)
    tblocks = table_rows // tm
    row = lambda cols: pl.BlockSpec((tm, cols), lambda i: (i, 0))
    const = lambda a: pl.BlockSpec(a.shape, lambda i: (0,) * a.ndim)
    tspec = pl.BlockSpec((3, tm, LANES), lambda i: (0, i % tblocks, 0))
    g2, gq2, gkv2 = g.reshape(1, d), g_bq.reshape(1, Q_RANK), g_bkv.reshape(1, KV_RANK)
    kv_cols = A_KV_HEADS * HEAD_DIM
    outs = [(A_HEADS * LANES, BF16), (A_KV_HEADS * LANES, BF16), (A_KV_HEADS * LANES, BF16),
            (IDX_HEADS * IDX_DIM, BF16), (IDX_DIM, BF16), (IDX_HEADS, F32),
            (B_HEADS * LANES, BF16), (B_HEADS * LANES, BF16), (B_HEADS * LANES, BF16),
            (kv_cols, F32), (kv_cols, F32), (IDX_DIM, F32), (KV_RANK, F32), (B_ROPE, F32)]
    return pl.pallas_call(
        _even_proj_kernel,
        grid=(m // tm,),
        in_specs=[row(d), const(g2), const(w), const(gq2), const(wuq), const(gkv2), const(wukv),
                  tspec, tspec, tspec],
        out_specs=[row(c) for c, _ in outs],
        out_shape=[jax.ShapeDtypeStruct((m, c), t) for c, t in outs],
        compiler_params=_params("parallel"),
        name="even_proj",
    )(x, g2, w, gq2, wuq, gkv2, wukv, *tables)


_O_QC, _O_KC, _O_VC, _O_F, _O_QD, _O_KD, _O_VD, _O_END = 0, 512, 1024, 1536, 1664, 2176, 2688, 3200
FOX_BIAS_TERMS = 3


def _three_terms(x):
    hi = x.astype(BF16)
    r = x - hi.astype(F32)
    mid = r.astype(BF16)
    return hi, mid, (r - mid.astype(F32)).astype(BF16)


def _cumulative(logf, carry_s, c0_ref, restart):
    tm = logf.shape[0]

    @pl.when(restart)
    def _():
        carry_s[...] = jnp.broadcast_to(c0_ref[0], carry_s.shape)

    r = lax.broadcasted_iota(I32, (tm, tm), 0)
    cidx = lax.broadcasted_iota(I32, (tm, tm), 1)
    upto = jnp.where(cidx <= r, 1.0, 0.0).astype(BF16)
    c = carry_s[0:1, :] + sum(jnp.dot(upto, t, preferred_element_type=F32) for t in _three_terms(logf))
    carry_s[...] = jnp.broadcast_to(c[tm - 1:tm, :], carry_s.shape)
    return c


def _fox_key_bias(c):
    r = lax.broadcasted_iota(I32, (LANES, C_HEADS * LANES), 0)
    cidx = lax.broadcasted_iota(I32, (LANES, C_HEADS * LANES), 1)
    placed = 0.0
    for j, term in enumerate(_three_terms(-LOG2E * c)):
        put = jnp.where((cidx == r * LANES + HEAD_DIM + j) & (r < C_HEADS), 1.0, 0.0).astype(BF16)
        placed = placed + jnp.dot(term, put, preferred_element_type=F32)
    return placed


def _store_fox_keys(out_ref, kc, c):
    bias = _lane_tiles(_fox_key_bias(c))
    for p, pair in enumerate(_lane_tiles(kc)):
        for u, tile in enumerate(_split_pair(pair)):
            h = 2 * p + u
            out_ref[:, h * LANES:(h + 1) * LANES] = (tile + bias[h]).astype(out_ref.dtype)


def _odd_proj_kernel(x_ref, g_ref, w_ref, bf_ref, c0_ref,
                     qc_o, kc_o, vc_o, qd_o, kd_o, vd_o, kc_s, vc_s, kd_s, vd_s, logf_s, carry_s,
                     *, tiles_per_seq):
    h = _rmsnorm_rows(x_ref[...], g_ref[...]).astype(BF16)
    p = jnp.dot(h, w_ref[...], preferred_element_type=F32)
    scale = HEAD_DIM ** -0.5 * LOG2E
    ones_col = _one_hot_row(HEAD_DIM)

    f = p[:, _O_F:_O_QD] + bf_ref[...]
    logf = jnp.minimum(f, 0.0) - jnp.log1p(jnp.exp(-jnp.abs(f)))
    logf = jnp.where(_lane_index(logf.shape) < C_HEADS, logf, 0.0)
    logf_s[...] = logf[:, :C_HEADS]
    c = _cumulative(logf, carry_s, c0_ref, pl.program_id(0) % tiles_per_seq == 0)

    kc, vc, kd, vd = p[:, _O_KC:_O_VC], p[:, _O_VC:_O_F], p[:, _O_KD:_O_VD], p[:, _O_VD:_O_END]
    kc_s[...], vc_s[...], kd_s[...], vd_s[...] = kc, vc, kd, vd
    _store_heads(qc_o, p[:, _O_QC:_O_KC], scale=scale,
                 extra=_one_hot_row(*range(HEAD_DIM, HEAD_DIM + FOX_BIAS_TERMS)))
    _store_fox_keys(kc_o, kc, c)
    _store_heads(vc_o, vc, extra=ones_col)
    _store_heads(qd_o, p[:, _O_QD:_O_KD], scale=scale)
    _store_heads(kd_o, kd)
    _store_heads(vd_o, vd)


def _odd_weights(w_in):
    cuts = np.cumsum(ODD_SPLIT)[:-1].tolist()
    qc, kc, vc, fc, qd, kd, vd = jnp.split(w_in, cuts, axis=1)
    w = jnp.concatenate([qc, kc, vc, _pad_cols(fc, LANES), qd, kd, vd], axis=1)
    assert w.shape[1] == _O_END
    return w.astype(BF16)


def _odd_proj(x, g, w, b_f, c0, *, seq_len, tm=256):
    m, d = x.shape
    tm = min(tm, seq_len)
    assert seq_len % tm == 0
    tiles_per_seq = seq_len // tm
    row = lambda cols: pl.BlockSpec((tm, cols), lambda i: (i, 0))
    const = lambda a: pl.BlockSpec(a.shape, lambda i: (0,) * a.ndim)
    g2 = g.reshape(1, d)
    bf2 = _pad_cols(b_f.reshape(1, C_HEADS), LANES)
    wide, flat = C_HEADS * LANES, C_HEADS * HEAD_DIM
    outs = [(wide, BF16)] * 6 + [(flat, F32)] * 4 + [(C_HEADS, F32)]
    return pl.pallas_call(
        functools.partial(_odd_proj_kernel, tiles_per_seq=tiles_per_seq),
        grid=(m // tm,),
        in_specs=[row(d), const(g2), const(w), const(bf2),
                  pl.BlockSpec((1, 1, LANES), lambda i: (i // tiles_per_seq, 0, 0))],
        out_specs=[row(c) for c, _ in outs],
        out_shape=[jax.ShapeDtypeStruct((m, c), t) for c, t in outs],
        scratch_shapes=[pltpu.VMEM((8, LANES), F32)],
        compiler_params=_params("arbitrary"),
        name="odd_proj",
    )(x, g2, w, bf2, c0)


def _fox_past_kernel(k_ref, logf_ref, c0_ref, k_o, cend_o, carry_s, *, tiles_per_seq):
    logf = jnp.pad(logf_ref[...], ((0, 0), (0, LANES - C_HEADS)))
    c = _cumulative(logf, carry_s, c0_ref, pl.program_id(0) % tiles_per_seq == 0)
    _store_fox_keys(k_o, k_ref[...], c)
    cend_o[0] = c[c.shape[0] - 1:, :]


def _fox_past(k, logf, *, seq_len, tm=256):
    m = k.shape[0]
    nseq = m // seq_len
    tm = min(tm, seq_len)
    tiles_per_seq = seq_len // tm
    row = lambda cols: pl.BlockSpec((tm, cols), lambda i: (i, 0))
    per_seq = pl.BlockSpec((1, 1, LANES), lambda i: (i // tiles_per_seq, 0, 0))
    return pl.pallas_call(
        functools.partial(_fox_past_kernel, tiles_per_seq=tiles_per_seq),
        grid=(m // tm,),
        in_specs=[row(C_HEADS * HEAD_DIM), row(C_HEADS), per_seq],
        out_specs=[row(C_HEADS * LANES), per_seq],
        out_shape=[jax.ShapeDtypeStruct((m, C_HEADS * LANES), BF16),
                   jax.ShapeDtypeStruct((nseq, 1, LANES), F32)],
        scratch_shapes=[pltpu.VMEM((8, LANES), F32)],
        compiler_params=_params("arbitrary"),
        name="fox_past",
    )(k, logf, jnp.zeros((nseq, 1, LANES), F32))


def _head_lanes(parts, ones_col=False):
    bsz, s_len, heads = parts[0].shape[:3]
    parts = [p.astype(BF16) for p in parts]
    used = sum(p.shape[-1] for p in parts)
    if ones_col:
        parts.append(jnp.ones((bsz, s_len, heads, 1), BF16))
        used += 1
    parts.append(jnp.zeros((bsz, s_len, heads, LANES - used), BF16))
    return jnp.concatenate(parts, axis=-1).reshape(bsz, s_len, heads * LANES)


def _keys(past, new, sk_pad):
    rows = new if past is None else jnp.concatenate([past, new], axis=1)
    pad = sk_pad - rows.shape[1]
    return jnp.pad(rows, ((0, 0), (0, pad), (0, 0))) if pad else rows


def _tiles(s_len, past_len):
    tq = min(256, s_len)
    tk, group = (256, 4) if past_len == 0 else (128, 3)
    return tq, tk, group, _round_up(past_len + s_len, tk * group)


def _even_mixer(x, h_gain, tables, past, weights, g_bq, g_bkv, w_o):
    bsz, s_len, d = x.shape
    m = bsz * s_len
    past_len = 0 if past is None else past[0].shape[1]
    sk = past_len + s_len
    tq, tk, group, sk_pad = _tiles(s_len, past_len)
    outs = _even_proj(x.reshape(m, d), h_gain, weights, g_bq, g_bkv, tables, seq_len=s_len)
    qa, ka, va, qi, ki, wi, qb, kb, vb, ka_s, va_s, ki_s, lat_s, kr_s = [
        o.reshape(bsz, s_len, -1) for o in outs]
    new_rows = (ka_s.reshape(bsz, s_len, A_KV_HEADS, HEAD_DIM), va_s.reshape(bsz, s_len, A_KV_HEADS, HEAD_DIM),
                ki_s, lat_s, kr_s)
    if past is None:
        pa = (None,) * 5
    else:
        c_k, c_v, c_ki, c_lat, c_kr = past
        kv = _proj(c_lat.reshape(bsz * past_len, KV_RANK), jnp.ones((KV_RANK,), F32), weights[2], norm=False,
                   tm=past_len).reshape(bsz, past_len, 2 * B_HEADS, LANES)
        kr_t = jnp.pad(c_kr, ((0, 0), (0, 0), (_B_ROPE_LANE, LANES - _B_ROPE_LANE - B_ROPE)))[:, :, None, :]
        ones_t = jnp.zeros((LANES,), F32).at[HEAD_DIM].set(1.0)
        pa = (_head_lanes([c_k]), _head_lanes([c_v], ones_col=True), c_ki.astype(BF16),
              (kv[:, :, :B_HEADS] + kr_t).astype(BF16).reshape(bsz, past_len, -1),
              (kv[:, :, B_HEADS:] + ones_t).astype(BF16).reshape(bsz, past_len, -1))

    out_a = _dsa_attention(qa, _keys(pa[0], ka, sk_pad), _keys(pa[1], va, sk_pad), qi, _keys(pa[2], ki, sk_pad),
                           wi, q_off=past_len, sk=sk, tq=tq, tk=tk, group=group)
    out_b = _softmax_attention(qb, _keys(pa[3], kb, sk_pad), _keys(pa[4], vb, sk_pad),
                               heads=B_HEADS, mask_kind="chunk", q_off=past_len, sk=sk, tq=tq, tk=tk,
                               group=group, name="mla_attention")
    mix = jnp.concatenate([out_a, out_b], axis=-1).reshape(m, -1)
    y = _matmul_residual(mix, w_o, x.reshape(m, d)).reshape(bsz, s_len, d)
    return y, new_rows


def _odd_mixer(x, h_gain, past, w, b_f, w_o):
    bsz, s_len, d = x.shape
    m = bsz * s_len
    past_len = 0 if past is None else past[0].shape[1]
    sk = past_len + s_len
    tq, tk, group, sk_pad = _tiles(s_len, past_len)
    if past is None:
        pa = (None,) * 4
        c0 = jnp.zeros((bsz, 1, LANES), F32)
    else:
        c_k, c_v, c_logf, d_k, d_v = past
        kc_past, c0 = _fox_past(c_k.reshape(bsz * past_len, -1), c_logf.reshape(bsz * past_len, C_HEADS),
                                seq_len=past_len)
        pa = (kc_past.reshape(bsz, past_len, -1), _head_lanes([c_v], ones_col=True),
              _head_lanes([d_k]), _head_lanes([d_v]))
    outs = _odd_proj(x.reshape(m, d), h_gain, w, b_f, c0, seq_len=s_len)
    qc, kc, vc, qd, kd, vd, kc_s, vc_s, kd_s, vd_s, logf = [o.reshape(bsz, s_len, -1) for o in outs]
    heads = lambda a: a.reshape(bsz, s_len, -1, HEAD_DIM)
    new_rows = (heads(kc_s), heads(vc_s), logf, heads(kd_s), heads(vd_s))

    out_c = _softmax_attention(qc, _keys(pa[0], kc, sk_pad), _keys(pa[1], vc, sk_pad),
                               heads=C_HEADS, mask_kind="causal", q_off=past_len, sk=sk, tq=tq, tk=tk,
                               group=group, name="fox_attention")
    out_d = _sb_attention(qd, _keys(pa[2], kd, sk_pad), _keys(pa[3], vd, sk_pad),
                          q_off=past_len, sk=sk, tq=tq, tk=tk)
    mix = jnp.concatenate([out_c, out_d], axis=-1).reshape(m, -1)
    y = _matmul_residual(mix, w_o, x.reshape(m, d)).reshape(bsz, s_len, d)
    return y, new_rows


def _final_norm_kernel(x_ref, g_ref, o_ref):
    x = x_ref[...]
    ms = jnp.mean(x * x, axis=-1, keepdims=True)
    o_ref[...] = x * lax.rsqrt(ms + EPS) * g_ref[...]


def _final_norm(x, g, *, tm=512):
    m, d = x.shape
    tm = min(tm, m)
    return pl.pallas_call(
        _final_norm_kernel,
        grid=(m // tm,),
        in_specs=[pl.BlockSpec((tm, d), lambda i: (i, 0)), pl.BlockSpec((1, d), lambda i: (0, 0))],
        out_specs=pl.BlockSpec((tm, d), lambda i: (i, 0)),
        out_shape=jax.ShapeDtypeStruct((m, d), F32),
        compiler_params=_params("parallel"),
        name="final_norm",
    )(x, g.reshape(1, d).astype(F32))


def _trunk(x, q_pos, caches, params, even_w, odd_w):
    (g_mix, g_ffn, g_final, w_in_even, g_b_q, g_b_kv, w_b_uq, w_b_ukv, w_o_even,
     w_in_odd, b_forget, w_o_odd, w_up, w_conv, b_conv, w_down) = params
    bsz, s_len, d = x.shape
    depth = g_mix.shape[0]
    even_rows = [[] for _ in range(5)]
    odd_rows = [[] for _ in range(5)]
    conv_rows = []
    rows = max(s_len, min(512, bsz * s_len))
    tables = (_rope_tables(q_pos, rows, period=HEAD_DIM, start=0, n_rot=ROT_DIM),
              _rope_tables(q_pos, rows, period=IDX_DIM, start=0, n_rot=IDX_ROT),
              _rope_tables(q_pos, rows, period=LANES, start=_B_ROPE_LANE, n_rot=B_ROPE))
    for l in range(depth):
        j = l // 2
        if l % 2 == 0:
            past = None if caches is None else tuple(c[j] for c in caches[0:5])
            x, rows = _even_mixer(x, g_mix[l], tables, past, even_w[j], g_b_q[j], g_b_kv[j], w_o_even[j])
            for lst, r in zip(even_rows, rows):
                lst.append(r)
        else:
            past = None if caches is None else tuple(c[j] for c in caches[5:10])
            x, rows = _odd_mixer(x, g_mix[l], past, odd_w[j], b_forget[j], w_o_odd[j])
            for lst, r in zip(odd_rows, rows):
                lst.append(r)
        state = jnp.zeros((bsz, CONV_W - 1, D_FF), F32) if caches is None else caches[10][l]
        y, new_buf = _conv_ffn(x.reshape(bsz * s_len, d), g_ffn[l], w_up[l], w_conv[l], b_conv[l], w_down[l],
                               state, seq_len=s_len, tm=1024)
        x = y.reshape(bsz, s_len, d)
        conv_rows.append(new_buf)
    out = _final_norm(x.reshape(bsz * s_len, d), g_final).reshape(bsz, s_len, d)
    states = [jnp.stack(r, axis=0) for r in even_rows + odd_rows] + [jnp.stack(conv_rows, axis=0)]
    return out, states


def kernel(x_prompt, x_sample, cache_a_k, cache_a_v, cache_a_idx_k, cache_b_latent, cache_b_rope,
           cache_c_k, cache_c_v, cache_c_logf, cache_d_k, cache_d_v, state_ffn_conv,
           g_mix, g_ffn, g_final, w_in_even, g_b_q, g_b_kv, w_b_uq, w_b_ukv, w_o_even,
           w_in_odd, b_forget, w_o_odd, w_up, w_conv, b_conv, w_down):
    params = (g_mix, g_ffn, g_final, w_in_even, g_b_q, g_b_kv, w_b_uq, w_b_ukv, w_o_even,
              w_in_odd, b_forget, w_o_odd, w_up, w_conv, b_conv, w_down)
    caches = (cache_a_k, cache_a_v, cache_a_idx_k, cache_b_latent, cache_b_rope,
              cache_c_k, cache_c_v, cache_c_logf, cache_d_k, cache_d_v, state_ffn_conv)
    past_len = cache_a_k.shape[2]
    pos_prompt = jnp.arange(x_prompt.shape[1], dtype=I32)
    pos_sample = past_len + jnp.arange(x_sample.shape[1], dtype=I32)
    even_w = [_even_weights(w_in_even[j], w_b_uq[j], w_b_ukv[j]) for j in range(w_in_even.shape[0])]
    odd_w = [_odd_weights(w_in_odd[j]) for j in range(w_in_odd.shape[0])]
    y_prompt, p_states = _trunk(x_prompt, pos_prompt, None, params, even_w, odd_w)
    y_sample, s_states = _trunk(x_sample, pos_sample, caches, params, even_w, odd_w)
    return (y_prompt, y_sample, *p_states, *s_states)
```

```python
import functools

import jax
import jax.numpy as jnp
import numpy as np
from jax import lax
from jax.experimental import pallas as pl
from jax.experimental.pallas import tpu as pltpu

F32 = jnp.float32
BF16 = jnp.bfloat16
I32 = jnp.int32
I16 = jnp.int16

CHUNK = 64
ROPE_THETA = 500000.0
EPS = 1e-6
HEAD_DIM = 64
ROT_DIM = HEAD_DIM // 4
A_HEADS = 8
A_KV_HEADS = 2
IDX_HEADS = 8
IDX_DIM = 32
IDX_ROT = IDX_DIM // 4
TOPK_MAX = 256
B_HEADS = 8
Q_RANK = 256
KV_RANK = 128
B_NOPE = 64
B_ROPE = 32
B_VDIM = 64
C_HEADS = 8
D_HEADS = 8
D_FF = 2816
CONV_W = 3

EVEN_SPLIT = [A_HEADS * HEAD_DIM, A_KV_HEADS * HEAD_DIM, A_KV_HEADS * HEAD_DIM,
              IDX_HEADS * IDX_DIM, IDX_DIM, IDX_HEADS, Q_RANK, KV_RANK, B_ROPE]
ODD_SPLIT = [C_HEADS * HEAD_DIM] * 3 + [C_HEADS] + [D_HEADS * HEAD_DIM] * 3

LANES = 128
VMEM_LIMIT_BYTES = 56 * 1024 * 1024
MASKED = -1e30
INT_MIN = -2 ** 31
LOG2E = 1.4426950408889634
F32_EXP2_UNDERFLOW = -152.0

_NT = (((1,), (1,)), ((), ()))


def _params(*sem):
    return pltpu.CompilerParams(dimension_semantics=sem, vmem_limit_bytes=VMEM_LIMIT_BYTES)


def _round_up(n, m):
    return (n + m - 1) // m * m


def _chunk_of(pos):
    return jnp.right_shift(pos, CHUNK.bit_length() - 1)


def _proj_kernel(x_ref, g_ref, w_ref, *out_refs, norm, emit_h):
    x = x_ref[...]
    if norm:
        ms = jnp.mean(x * x, axis=-1, keepdims=True)
        x = x * lax.rsqrt(ms + EPS) * g_ref[...]
    out_refs[0][...] = jnp.dot(x.astype(BF16), w_ref[...], preferred_element_type=F32)
    if emit_h:
        out_refs[1][...] = x


def _proj(x, g, w, *, norm=True, emit_h=False, tm=512):
    m, k = x.shape
    n = w.shape[1]
    tm = min(tm, m)
    assert m % tm == 0
    out_shape = [jax.ShapeDtypeStruct((m, n), F32)]
    out_specs = [pl.BlockSpec((tm, n), lambda i: (i, 0))]
    if emit_h:
        out_shape.append(jax.ShapeDtypeStruct((m, k), F32))
        out_specs.append(pl.BlockSpec((tm, k), lambda i: (i, 0)))
    res = pl.pallas_call(
        functools.partial(_proj_kernel, norm=norm, emit_h=emit_h),
        grid=(m // tm,),
        in_specs=[pl.BlockSpec((tm, k), lambda i: (i, 0)),
                  pl.BlockSpec((1, k), lambda i: (0, 0)),
                  pl.BlockSpec((k, n), lambda i: (0, 0))],
        out_specs=out_specs,
        out_shape=out_shape,
        compiler_params=_params("parallel"),
        name="proj",
    )(x, g.reshape(1, k).astype(F32), w.astype(BF16))
    return res if emit_h else res[0]


def _matmul_residual_kernel(a_ref, w_ref, r_ref, o_ref):
    o_ref[...] = r_ref[...] + jnp.dot(a_ref[...], w_ref[...], preferred_element_type=F32)


def _matmul_residual(a, w, res, *, tm=512):
    m, k = a.shape
    n = w.shape[1]
    tm = min(tm, m)
    assert m % tm == 0
    return pl.pallas_call(
        _matmul_residual_kernel,
        grid=(m // tm,),
        in_specs=[pl.BlockSpec((tm, k), lambda i: (i, 0)),
                  pl.BlockSpec((k, n), lambda i: (0, 0)),
                  pl.BlockSpec((tm, n), lambda i: (i, 0))],
        out_specs=pl.BlockSpec((tm, n), lambda i: (i, 0)),
        out_shape=jax.ShapeDtypeStruct((m, n), F32),
        compiler_params=_params("parallel"),
        name="matmul_residual",
    )(a, w.astype(BF16), res)


def _conv_ffn_kernel(x_ref, g_ref, wg_ref, wu_ref, wc_ref, bc_ref, wd_ref, st_ref,
                     y_ref, ns_ref, h_s, acc_s, carry_s, *, tiles_per_seq, seqs, tm, tf):
    i = pl.program_id(0)
    c = pl.program_id(1)
    nc = pl.num_programs(1)
    cols = pl.ds(pl.multiple_of(c * tf, tf), tf)
    rows_per = tm // seqs

    @pl.when(c == 0)
    def _():
        x = x_ref[...]
        ms = jnp.mean(x * x, axis=-1, keepdims=True)
        h_s[...] = (x * lax.rsqrt(ms + EPS) * g_ref[...]).astype(BF16)
        acc_s[...] = jnp.zeros_like(acc_s)

    h = h_s[...]
    gate = jnp.dot(h, wg_ref[...], preferred_element_type=F32)
    up = jnp.dot(h, wu_ref[...], preferred_element_type=F32)

    if seqs == 1:
        @pl.when(i % tiles_per_seq == 0)
        def _():
            carry_s[c] = st_ref[0, :, cols]
        before = [carry_s[c]]
    else:
        before = [st_ref[s, :, cols] for s in range(seqs)]
    row = lax.broadcasted_iota(I32, gate.shape, 0)
    g1 = pltpu.roll(gate, 1, 0)
    g2 = pltpu.roll(gate, 2, 0)
    for s, prev in enumerate(before):
        first = s * rows_per
        g1 = jnp.where(row == first, prev[1:2, :], g1)
        g2 = jnp.where(row == first, prev[0:1, :], jnp.where(row == first + 1, prev[1:2, :], g2))
    wc = wc_ref[...]
    gc = bc_ref[...] + g2 * wc[0:1, :]
    gc = gc + g1 * wc[1:2, :]
    gc = gc + gate * wc[2:3, :]
    act = gc * jax.nn.sigmoid(gc) * up
    acc_s[...] += jnp.dot(act.astype(BF16), wd_ref[...], preferred_element_type=F32)

    for s in range(seqs):
        ns_ref[s, :, cols] = gate[(s + 1) * rows_per - 2:(s + 1) * rows_per, :]
    if seqs == 1:
        carry_s[c] = gate[tm - 2:tm, :]

    @pl.when(c == nc - 1)
    def _():
        y_ref[...] = x_ref[...] + acc_s[...]


def _conv_ffn(x, g, w_up, w_conv, b_conv, w_down, state, *, seq_len, tm, tf=256):
    m, d = x.shape
    nseq = m // seq_len
    tm = min(tm, m)
    assert m % tm == 0 and D_FF % tf == 0 and (seq_len % tm == 0 or tm % seq_len == 0)
    tiles_per_seq = max(1, seq_len // tm)
    seqs = max(1, tm // seq_len)
    nc = D_FF // tf
    wg = w_up[:, :D_FF].astype(BF16)
    wu = w_up[:, D_FF:].astype(BF16)
    state_spec = pl.BlockSpec((seqs, CONV_W - 1, D_FF), lambda i, c: (i // tiles_per_seq, 0, 0))
    y, ns = pl.pallas_call(
        functools.partial(_conv_ffn_kernel, tiles_per_seq=tiles_per_seq, seqs=seqs, tm=tm, tf=tf),
        grid=(m // tm, nc),
        in_specs=[pl.BlockSpec((tm, d), lambda i, c: (i, 0)),
                  pl.BlockSpec((1, d), lambda i, c: (0, 0)),
                  pl.BlockSpec((d, tf), lambda i, c: (0, c)),
                  pl.BlockSpec((d, tf), lambda i, c: (0, c)),
                  pl.BlockSpec((CONV_W, tf), lambda i, c: (0, c)),
                  pl.BlockSpec((1, tf), lambda i, c: (0, c)),
                  pl.BlockSpec((tf, d), lambda i, c: (c, 0)),
                  state_spec],
        out_specs=[pl.BlockSpec((tm, d), lambda i, c: (i, 0)), state_spec],
        out_shape=[jax.ShapeDtypeStruct((m, d), F32),
                   jax.ShapeDtypeStruct((nseq, CONV_W - 1, D_FF), F32)],
        scratch_shapes=[pltpu.VMEM((tm, d), BF16),
                        pltpu.VMEM((tm, d), F32),
                        pltpu.VMEM((nc, CONV_W - 1, tf), F32)],
        compiler_params=_params("arbitrary", "arbitrary"),
        name="conv_ffn",
    )(x, g.reshape(1, d).astype(F32), wg, wu, w_conv.astype(F32), b_conv.reshape(1, D_FF).astype(F32),
      w_down.astype(BF16), state.astype(F32))
    return y, ns


def _attn_call(kernel, inputs, in_specs, out_cols, *, bsz, sq, tq, scratch=(), name):
    return pl.pallas_call(
        kernel,
        grid=(bsz, sq // tq),
        in_specs=in_specs,
        out_specs=pl.BlockSpec((1, tq, out_cols), lambda b, i: (b, i, 0)),
        out_shape=jax.ShapeDtypeStruct((bsz, sq, out_cols), BF16),
        scratch_shapes=list(scratch),
        compiler_params=_params("parallel", "arbitrary"),
        name=name,
    )(*inputs)


def _q_spec(tq, cols):
    return pl.BlockSpec((1, tq, cols), lambda b, i: (b, i, 0))


def _k_spec(rows, cols):
    return pl.BlockSpec((1, rows, cols), lambda b, i: (b, 0, 0))


def _tile_geometry(i, *, tq, span, q_off):
    q_lo = q_off + i * tq
    return q_lo, q_lo // span


def _lane_tiles(x):
    return [x[:, j * LANES:(j + 1) * LANES] for j in range(x.shape[1] // LANES)]


def _visible(kind, q_lo, ks, *, tq, tk, sk):
    q_pos = q_lo + lax.broadcasted_iota(I32, (tq, 1), 0)
    k_pos = ks + lax.broadcasted_iota(I32, (1, tk), 1)
    if kind == "chunk":
        return (_chunk_of(k_pos) <= _chunk_of(q_pos)) & (k_pos < sk)
    if kind == "causal":
        return k_pos <= q_pos
    assert kind == "strict"
    return k_pos < q_pos


HEADS_PER_STEP = 2


def _span_cases(i, body, *, tq, span, q_off, n_tiles):
    g_own = (q_off + i * tq) // span
    for g in sorted({(q_off + t * tq) // span for t in range(n_tiles)}):
        pl.when(g_own == g)(functools.partial(body, g + 1))


def _softmax_heads(qs, k_ref, v_ref, s_scr, kcols, vcols, *, tq, tk, blocks, own_from, own_mask,
                   bias_ref=None):
    row_max = []
    for slot, (q, kcol) in enumerate(zip(qs, kcols)):
        mx = jnp.full((tq, LANES), MASKED, F32)
        for b in range(blocks):
            ks = b * tk
            s = lax.dot_general(q, k_ref[0, ks:ks + tk, kcol], _NT, preferred_element_type=F32)
            if bias_ref is not None:
                s = s + bias_ref[:, ks:ks + tk]
            if own_mask is not None and b >= own_from:
                s = jnp.where(own_mask(ks), s, MASKED)
            s_scr[slot, :, ks:ks + tk] = s
            for t in _lane_tiles(s):
                mx = jnp.maximum(mx, t)
        row_max.append(jnp.broadcast_to(jnp.max(mx, axis=1, keepdims=True), (tq, LANES)))
    outs = []
    for slot, (m, vcol) in enumerate(zip(row_max, vcols)):
        acc = jnp.zeros((tq, LANES), F32)
        for b in range(blocks):
            ks = b * tk
            p = jnp.concatenate([jnp.exp2(t - m) for t in _lane_tiles(s_scr[slot, :, ks:ks + tk])], axis=1)
            acc = acc + jnp.dot(p.astype(BF16), v_ref[0, ks:ks + tk, vcol], preferred_element_type=F32)
        outs.append(acc[:, :HEAD_DIM] / acc[:, HEAD_DIM:HEAD_DIM + 1])
    return outs


def _lane_tile_at(index):
    return pl.ds(pl.multiple_of(index * LANES, LANES), LANES)


def _softmax_attn_kernel(q_ref, k_ref, v_ref, o_ref, s_scr, *, heads, mask_kind, tq, tk, group, q_off, sk,
                         n_tiles):
    i = pl.program_id(1)
    own_mask = functools.partial(_visible, mask_kind, q_off + i * tq, tq=tq, tk=tk, sk=sk)

    def attend(n_spans):
        def step(hp, _):
            cols = [_lane_tile_at(hp * HEADS_PER_STEP + u) for u in range(HEADS_PER_STEP)]
            outs = _softmax_heads([q_ref[0, :, c] for c in cols], k_ref, v_ref, s_scr, cols, cols,
                                  tq=tq, tk=tk, blocks=n_spans * group, own_from=(n_spans - 1) * group,
                                  own_mask=own_mask)
            o_ref[0, :, _lane_tile_at(hp)] = jnp.concatenate(outs, axis=1).astype(o_ref.dtype)
            return 0
        lax.fori_loop(0, heads // HEADS_PER_STEP, step, 0)

    _span_cases(i, attend, tq=tq, span=tk * group, q_off=q_off, n_tiles=n_tiles)


def _check_tiling(sk_pad, *, tq, tk, group, q_off):
    assert (tk * group) % tq == 0 and q_off % tq == 0 and sk_pad % (tk * group) == 0 and tk % LANES == 0
    assert HEADS_PER_STEP * HEAD_DIM == LANES


def _softmax_attention(q, k, v, *, heads, mask_kind, q_off, sk, tq, tk, group, name):
    bsz, sq, _ = q.shape
    sk_pad = k.shape[1]
    _check_tiling(sk_pad, tq=tq, tk=tk, group=group, q_off=q_off)
    kern = functools.partial(_softmax_attn_kernel, heads=heads, mask_kind=mask_kind,
                             tq=tq, tk=tk, group=group, q_off=q_off, sk=sk, n_tiles=sq // tq)
    cols = heads * LANES
    return _attn_call(kern, (q, k, v), [_q_spec(tq, cols), _k_spec(sk_pad, cols), _k_spec(sk_pad, cols)],
                      heads * HEAD_DIM, bsz=bsz, sq=sq, tq=tq,
                      scratch=(pltpu.VMEM((HEADS_PER_STEP, tq, sk_pad), F32),), name=name)


def _split_bf16(x):
    hi = x.astype(BF16)
    lo = (x - hi.astype(F32)).astype(BF16)
    return hi, lo


SB_HEADS_PER_LOOP = 2


def _sb_kernel(q_ref, k_ref, v_ref, o_ref, *, tq, tk, q_off, sk):
    q_lo, n_before = _tile_geometry(pl.program_id(1), tq=tq, span=tk, q_off=q_off)
    r = lax.broadcasted_iota(I32, (tk, tk), 0)
    cidx = lax.broadcasted_iota(I32, (tk, tk), 1)
    later = jnp.where(r > cidx, 1.0, 0.0).astype(BF16)

    def block(q, h, kb, run, acc, mask):
        ks = pl.multiple_of(kb * tk, tk)
        k = k_ref[0, pl.ds(ks, tk), h * LANES:(h + 1) * LANES]
        v = v_ref[0, pl.ds(ks, tk), h * LANES:(h + 1) * LANES]
        z = lax.dot_general(q, k, _NT, preferred_element_type=F32)
        log_beta = jnp.minimum(z, 0.0) - jnp.log2(1.0 + jnp.exp2(-jnp.abs(z)))
        log_1m = log_beta - z
        if mask is not None:
            log_1m = jnp.where(mask, log_1m, 0.0)
        hi, lo = _split_bf16(log_1m)
        after = (jnp.dot(hi, later, preferred_element_type=F32)
                 + jnp.dot(lo, later, preferred_element_type=F32))
        a = jnp.exp2(log_beta + after + run)
        if mask is not None:
            a = jnp.where(mask, a, 0.0)
        acc = acc + jnp.dot(a.astype(BF16), v, preferred_element_type=F32)
        run = run + after[:, 0:1] + log_1m[:, 0:1]
        return run, acc

    own = _visible("strict", q_lo, n_before * tk, tq=tq, tk=tk, sk=sk)
    for h0 in range(0, D_HEADS, SB_HEADS_PER_LOOP):
        hs = range(h0, h0 + SB_HEADS_PER_LOOP)
        qs = [q_ref[0, :, h * LANES:(h + 1) * LANES] for h in hs]
        state = []
        for h, q in zip(hs, qs):
            state.extend(block(q, h, n_before, jnp.zeros((tq, 1), F32), jnp.zeros((tq, LANES), F32), own))

        def alive(state):
            top = functools.reduce(jnp.maximum, state[0::2])
            return jnp.max(top) > F32_EXP2_UNDERFLOW

        def cond(c):
            return (c[0] < n_before) & c[1]

        def body(c, hs=hs, qs=qs):
            j, _, state = c
            new = []
            for n, (h, q) in enumerate(zip(hs, qs)):
                new.extend(block(q, h, n_before - 1 - j, state[2 * n], state[2 * n + 1], None))
            return j + 1, alive(new), tuple(new)

        _, _, state = lax.while_loop(cond, body, (jnp.int32(0), alive(state), tuple(state)))
        for n, h in enumerate(hs):
            o_ref[0, :, h * HEAD_DIM:(h + 1) * HEAD_DIM] = state[2 * n + 1][:, :HEAD_DIM].astype(o_ref.dtype)


def _sb_attention(q, k, v, *, q_off, sk, tq, tk):
    bsz, sq, _ = q.shape
    sk_pad = k.shape[1]
    _check_tiling(sk_pad, tq=tq, tk=tk, group=1, q_off=q_off)
    kern = functools.partial(_sb_kernel, tq=tq, tk=tk, q_off=q_off, sk=sk)
    cols = D_HEADS * LANES
    return _attn_call(kern, (q, k, v), [_q_spec(tq, cols), _k_spec(sk_pad, cols), _k_spec(sk_pad, cols)],
                      D_HEADS * HEAD_DIM, bsz=bsz, sq=sq, tq=tq, name="sb_attention")


def _sortable_key(score):
    bits = lax.bitcast_convert_type(score, I32)
    return jnp.where(bits < 0, bits ^ 0x7FFFFFFF, bits)


KEY_OF_NEG_INF = -0x7F800001


def _dsa_kernel(q_ref, k_ref, v_ref, qi_ref, ki_ref, wi_ref, o_ref, key_s, hi_s, lo_s, bias_s, s_scr, w_s,
                *, tq, tk, group, q_off, sk, topk, n_tiles):
    i = pl.program_id(1)
    nt = tk // LANES
    own_visible = functools.partial(_visible, "chunk", q_off + i * tq, tq=tq, tk=tk, sk=sk)
    kf = float(topk)

    for h in range(IDX_HEADS):
        w_s[h] = jnp.broadcast_to(wi_ref[0, :, h:h + 1], (tq, LANES))

    def select_and_attend(n_spans):
        blocks, own_from = n_spans * group, (n_spans - 1) * group
        starts = [b * tk for b in range(blocks)]

        def own(ks, value, masked):
            return jnp.where(own_visible(ks), value, masked) if ks >= own_from * tk else value

        qis = [qi_ref[0, :, h * IDX_DIM:(h + 1) * IDX_DIM] for h in range(IDX_HEADS)]
        for ks in starts:
            ki = ki_ref[0, ks:ks + tk, :]
            total = jnp.zeros((tq, tk), F32)
            for h in range(IDX_HEADS):
                sc = lax.dot_general(qis[h], ki, _NT, preferred_element_type=F32)
                total = total + jnp.maximum(sc, 0.0) * jnp.concatenate([w_s[h]] * nt, axis=1)
            key_s[:, ks:ks + tk] = _sortable_key(own(ks, total, -jnp.inf))

        def count16(ref, cand):
            cand = jnp.broadcast_to(cand.astype(I16), (tq, LANES))
            cnt = jnp.zeros((tq, LANES), I16)
            for ks in starts:
                for t in _lane_tiles(ref[:, ks:ks + tk]):
                    cnt = cnt + jnp.where(t >= cand, jnp.int16(1), jnp.int16(0))
            return jnp.sum(cnt.astype(F32), axis=1, keepdims=True)

        def search16(ref, n_lowest):
            lowest = jnp.full((tq, 1), -2 ** 15, I32)
            zero = jnp.zeros((tq, 1), I32)
            n_zero = count16(ref, zero)
            ok = n_zero >= kf
            t, n_t = jnp.where(ok, zero, lowest), jnp.where(ok, n_zero, n_lowest)

            def bit_step(it, c):
                t, n_t = c
                cand = t | jnp.left_shift(jnp.int32(1), 14 - it)
                n_cand = count16(ref, cand)
                ok = n_cand >= kf
                return jnp.where(ok, cand, t), jnp.where(ok, n_cand, n_t)

            return lax.fori_loop(0, 15, bit_step, (t, n_t))

        for ks in starts:
            key = key_s[:, ks:ks + tk]
            hi_s[:, ks:ks + tk] = jnp.right_shift(key, 16).astype(I16)
        t_hi, n_hi = search16(hi_s, float(blocks * tk))
        t_hi16 = jnp.broadcast_to(t_hi.astype(I16), (tq, LANES))
        for ks in starts:
            key = key_s[:, ks:ks + tk]
            low = ((key & 0xFFFF) - 2 ** 15).astype(I16)
            tiles = zip(_lane_tiles(hi_s[:, ks:ks + tk]), _lane_tiles(low))
            lo_s[:, ks:ks + tk] = jnp.concatenate(
                [jnp.where(h > t_hi16, jnp.int16(2 ** 15 - 1), jnp.where(h < t_hi16, jnp.int16(-2 ** 15), l))
                 for h, l in tiles], axis=1)
        t_lo, n_thr = search16(lo_s, n_hi)
        thr = jnp.left_shift(t_hi, 16) | (t_lo + 2 ** 15)
        thr_b = jnp.broadcast_to(thr, (tq, LANES))

        tied = jnp.max(jnp.where(thr > KEY_OF_NEG_INF, n_thr, 0.0)) > kf

        @pl.when(jnp.logical_not(tied))
        def _():
            for ks in starts:
                tiles = _lane_tiles(key_s[:, ks:ks + tk])
                bias = jnp.concatenate([jnp.where(t >= thr_b, 0.0, MASKED) for t in tiles], axis=1)
                bias_s[:, ks:ks + tk] = own(ks, bias, MASKED)

        @pl.when(tied)
        def _():
            r = lax.broadcasted_iota(I32, (tk, tk), 0)
            cidx = lax.broadcasted_iota(I32, (tk, tk), 1)
            upto = jnp.where(r <= cidx, 1.0, 0.0).astype(BF16)
            above = jnp.zeros((tq, LANES), F32)
            for ks in starts:
                for t in _lane_tiles(key_s[:, ks:ks + tk]):
                    above = above + jnp.where(t > thr_b, 1.0, 0.0)
            need = kf - jnp.sum(above, axis=1, keepdims=True)
            seen = jnp.zeros((tq, 1), F32)
            for ks in starts:
                key = key_s[:, ks:ks + tk]
                tie = jnp.where(key == thr, 1.0, 0.0)
                rank = seen + jnp.dot(tie.astype(BF16), upto, preferred_element_type=F32)
                sel = (key > thr) | ((key == thr) & (rank <= need))
                bias_s[:, ks:ks + tk] = own(ks, jnp.where(sel, 0.0, MASKED), MASKED)
                seen = seen + jnp.sum(tie, axis=1, keepdims=True)

        def step(hp, _):
            heads = [hp * HEADS_PER_STEP + u for u in range(HEADS_PER_STEP)]
            kv = [_lane_tile_at(hd // (A_HEADS // A_KV_HEADS)) for hd in heads]
            outs = _softmax_heads([q_ref[0, :, _lane_tile_at(hd)] for hd in heads], k_ref, v_ref, s_scr,
                                  kv, kv, tq=tq, tk=tk, blocks=blocks, own_from=own_from, own_mask=None,
                                  bias_ref=bias_s)
            o_ref[0, :, _lane_tile_at(hp)] = jnp.concatenate(outs, axis=1).astype(o_ref.dtype)
            return 0
        lax.fori_loop(0, A_HEADS // HEADS_PER_STEP, step, 0)

    _span_cases(i, select_and_attend, tq=tq, span=tk * group, q_off=q_off, n_tiles=n_tiles)


def _dsa_attention(q, k, v, qi, ki, wi, *, q_off, sk, tq, tk, group):
    bsz, sq, _ = q.shape
    sk_pad = k.shape[1]
    _check_tiling(sk_pad, tq=tq, tk=tk, group=group, q_off=q_off)
    topk = min(TOPK_MAX, sk // 4)
    kern = functools.partial(_dsa_kernel, tq=tq, tk=tk, group=group, q_off=q_off, sk=sk, topk=topk,
                             n_tiles=sq // tq)
    kv_cols = A_KV_HEADS * LANES
    return _attn_call(kern, (q, k, v, qi, ki, wi),
                      [_q_spec(tq, A_HEADS * LANES), _k_spec(sk_pad, kv_cols), _k_spec(sk_pad, kv_cols),
                       _q_spec(tq, IDX_HEADS * IDX_DIM), _k_spec(sk_pad, IDX_DIM), _q_spec(tq, IDX_HEADS)],
                      A_HEADS * HEAD_DIM, bsz=bsz, sq=sq, tq=tq,
                      scratch=(pltpu.VMEM((tq, sk_pad), I32), pltpu.VMEM((tq, sk_pad), I16),
                               pltpu.VMEM((tq, sk_pad), I16), pltpu.VMEM((tq, sk_pad), F32),
                               pltpu.VMEM((HEADS_PER_STEP, tq, sk_pad), F32),
                               pltpu.VMEM((IDX_HEADS, tq, LANES), F32)),
                      name="dsa_attention")


def _rmsnorm_rows(x, g):
    ms = jnp.mean(x * x, axis=-1, keepdims=True)
    return x * lax.rsqrt(ms + EPS) * g


def _lane_index(shape):
    return lax.broadcasted_iota(I32, shape, len(shape) - 1)


def _one_hot_row(*lanes):
    lane = _lane_index((1, LANES))
    hit = functools.reduce(jnp.logical_or, [lane == l for l in lanes])
    return jnp.where(hit, 1.0, 0.0)


def _rotate(tile, tables, half):
    cos, sin_up, sin_dn = tables
    return tile * cos + pltpu.roll(tile, half, 1) * sin_up + pltpu.roll(tile, LANES - half, 1) * sin_dn


def _split_pair(pair):
    low = _lane_index(pair.shape) < HEAD_DIM
    return jnp.where(low, pair, 0.0), jnp.where(low, pltpu.roll(pair, HEAD_DIM, 1), 0.0)


def _store_heads(out_ref, compact, *, scale=None, extra=None):
    for p, pair in enumerate(_lane_tiles(compact)):
        for u, tile in enumerate(_split_pair(pair)):
            if scale is not None:
                tile = tile * scale
            if extra is not None:
                tile = tile + extra
            h = 2 * p + u
            out_ref[:, h * LANES:(h + 1) * LANES] = tile.astype(out_ref.dtype)


_E_QA, _E_KA, _E_VA, _E_QI, _E_KI, _E_WI, _E_CQ, _E_CKV, _E_KR, _E_END = (
    0, 512, 640, 768, 1024, 1152, 1280, 1536, 1664, 1792)
_B_ROPE_LANE = B_NOPE


def _even_proj_kernel(x_ref, g_ref, w_ref, gq_ref, wuq_ref, gkv_ref, wukv_ref, ta_ref, ti_ref, tb_ref,
                      qa_o, ka_o, va_o, qi_o, ki_o, wi_o, qb_o, kb_o, vb_o,
                      ka_s, va_s, ki_s, lat_s, kr_s):
    h = _rmsnorm_rows(x_ref[...], g_ref[...]).astype(BF16)
    p = jnp.dot(h, w_ref[...], preferred_element_type=F32)
    ta = (ta_ref[0], ta_ref[1], ta_ref[2])
    ti = (ti_ref[0], ti_ref[1], ti_ref[2])
    tb = (tb_ref[0], tb_ref[1], tb_ref[2])
    ones_col = _one_hot_row(HEAD_DIM)

    qa = jnp.concatenate([_rotate(t, ta, ROT_DIM // 2) for t in _lane_tiles(p[:, _E_QA:_E_KA])], axis=1)
    _store_heads(qa_o, qa, scale=HEAD_DIM ** -0.5 * LOG2E)
    ka = _rotate(p[:, _E_KA:_E_VA], ta, ROT_DIM // 2)
    ka_s[...] = ka
    _store_heads(ka_o, ka)
    va = p[:, _E_VA:_E_QI]
    va_s[...] = va
    _store_heads(va_o, va, extra=ones_col)
    qi = jnp.concatenate([_rotate(t, ti, IDX_ROT // 2) for t in _lane_tiles(p[:, _E_QI:_E_KI])], axis=1)
    qi_o[...] = (qi * IDX_DIM ** -0.5).astype(qi_o.dtype)
    ki = _rotate(p[:, _E_KI:_E_WI], ti, IDX_ROT // 2)[:, :IDX_DIM]
    ki_s[...] = ki
    ki_o[...] = ki.astype(ki_o.dtype)
    wi_o[...] = p[:, _E_WI:_E_WI + IDX_HEADS] * IDX_HEADS ** -0.5

    cq = _rmsnorm_rows(p[:, _E_CQ:_E_CKV], gq_ref[...]).astype(BF16)
    qb = jnp.dot(cq, wuq_ref[...], preferred_element_type=F32)
    scale_b = (B_NOPE + B_ROPE) ** -0.5 * LOG2E
    for hd, t in enumerate(_lane_tiles(qb)):
        qb_o[:, hd * LANES:(hd + 1) * LANES] = (_rotate(t, tb, B_ROPE // 2) * scale_b).astype(qb_o.dtype)
    lat = _rmsnorm_rows(p[:, _E_CKV:_E_KR], gkv_ref[...])
    lat_s[...] = lat
    kv = jnp.dot(lat.astype(BF16), wukv_ref[...], preferred_element_type=F32)
    kr = _rotate(p[:, _E_KR:_E_END], tb, B_ROPE // 2)
    kr_s[...] = kr[:, _B_ROPE_LANE:_B_ROPE_LANE + B_ROPE]
    tiles = _lane_tiles(kv)
    for hd in range(B_HEADS):
        kb_o[:, hd * LANES:(hd + 1) * LANES] = (tiles[hd] + kr).astype(kb_o.dtype)
        vb_o[:, hd * LANES:(hd + 1) * LANES] = (tiles[B_HEADS + hd] + ones_col).astype(vb_o.dtype)


def _pad_cols(w, width):
    return jnp.pad(w, ((0, 0), (0, width - w.shape[1])))


def _even_weights(w_in, w_uq, w_ukv):
    cuts = np.cumsum(EVEN_SPLIT)[:-1].tolist()
    qa, ka, va, qi, ki, wi, cq, ckv, kr = jnp.split(w_in, cuts, axis=1)
    kr = jnp.pad(kr, ((0, 0), (_B_ROPE_LANE, LANES - _B_ROPE_LANE - B_ROPE)))
    w = jnp.concatenate([qa, ka, va, qi, _pad_cols(ki, LANES), _pad_cols(wi, LANES), cq, ckv, kr], axis=1)
    d = w_in.shape[0]
    wuq = jnp.pad(w_uq.reshape(Q_RANK, B_HEADS, B_NOPE + B_ROPE),
                  ((0, 0), (0, 0), (0, LANES - B_NOPE - B_ROPE))).reshape(Q_RANK, B_HEADS * LANES)
    kvw = w_ukv.reshape(KV_RANK, B_HEADS, B_NOPE + B_VDIM)
    pad = lambda a: jnp.pad(a, ((0, 0), (0, 0), (0, LANES - a.shape[2]))).reshape(KV_RANK, B_HEADS * LANES)
    wukv = jnp.concatenate([pad(kvw[:, :, :B_NOPE]), pad(kvw[:, :, B_NOPE:])], axis=1)
    assert w.shape == (d, _E_END)
    return w.astype(BF16), wuq.astype(BF16), wukv.astype(BF16)


def _rope_tables(pos, rows, *, period, start, n_rot):
    half = n_rot // 2
    inv = ROPE_THETA ** (-jnp.arange(half, dtype=F32) * 2.0 / n_rot)
    ang = pos.astype(F32)[:, None] * inv[None, :]
    cos, sin = jnp.cos(ang), jnp.sin(ang)
    off = np.arange(LANES) % period - start
    idx = np.where((off >= 0) & (off < n_rot), off % half, 0)
    lower = (off >= 0) & (off < half)
    upper = (off >= half) & (off < n_rot)
    cos_t = jnp.where(lower | upper, cos[:, idx], 1.0)
    sin_up = jnp.where(upper, sin[:, idx], 0.0)
    sin_dn = jnp.where(lower, -sin[:, idx], 0.0)
    tabs = jnp.stack([cos_t, sin_up, sin_dn])
    return jnp.tile(tabs, (1, rows // pos.shape[0], 1))


def _even_proj(x, g, weights, g_bq, g_bkv, tables, *, seq_len, tm=512):
    m, d = x.shape
    w, wuq, wukv = weights
    tm = min(tm, m)
    table_rows = tables[0].shape[1]
    assert m % tm == 0 and table_rows % tm == 0 and (seq_len % tm == 0 or tm % seq_len == 0)
    tblocks = table_rows // tm
    row = lambda cols: pl.BlockSpec((tm, cols), lambda i: (i, 0))
    const = lambda a: pl.BlockSpec(a.shape, lambda i: (0,) * a.ndim)
    tspec = pl.BlockSpec((3, tm, LANES), lambda i: (0, i % tblocks, 0))
    g2, gq2, gkv2 = g.reshape(1, d), g_bq.reshape(1, Q_RANK), g_bkv.reshape(1, KV_RANK)
    kv_cols = A_KV_HEADS * HEAD_DIM
    outs = [(A_HEADS * LANES, BF16), (A_KV_HEADS * LANES, BF16), (A_KV_HEADS * LANES, BF16),
            (IDX_HEADS * IDX_DIM, BF16), (IDX_DIM, BF16), (IDX_HEADS, F32),
            (B_HEADS * LANES, BF16), (B_HEADS * LANES, BF16), (B_HEADS * LANES, BF16),
            (kv_cols, F32), (kv_cols, F32), (IDX_DIM, F32), (KV_RANK, F32), (B_ROPE, F32)]
    return pl.pallas_call(
        _even_proj_kernel,
        grid=(m // tm,),
        in_specs=[row(d), const(g2), const(w), const(gq2), const(wuq), const(gkv2), const(wukv),
                  tspec, tspec, tspec],
        out_specs=[row(c) for c, _ in outs],
        out_shape=[jax.ShapeDtypeStruct((m, c), t) for c, t in outs],
        compiler_params=_params("parallel"),
        name="even_proj",
    )(x, g2, w, gq2, wuq, gkv2, wukv, *tables)


_O_QC, _O_KC, _O_VC, _O_F, _O_QD, _O_KD, _O_VD, _O_END = 0, 512, 1024, 1536, 1664, 2176, 2688, 3200
FOX_BIAS_TERMS = 3


def _three_terms(x):
    hi = x.astype(BF16)
    r = x - hi.astype(F32)
    mid = r.astype(BF16)
    return hi, mid, (r - mid.astype(F32)).astype(BF16)


def _cumulative(logf, carry_s, c0_ref, restart):
    tm = logf.shape[0]

    @pl.when(restart)
    def _():
        carry_s[...] = jnp.broadcast_to(c0_ref[0], carry_s.shape)

    r = lax.broadcasted_iota(I32, (tm, tm), 0)
    cidx = lax.broadcasted_iota(I32, (tm, tm), 1)
    upto = jnp.where(cidx <= r, 1.0, 0.0).astype(BF16)
    c = carry_s[0:1, :] + sum(jnp.dot(upto, t, preferred_element_type=F32) for t in _three_terms(logf))
    carry_s[...] = jnp.broadcast_to(c[tm - 1:tm, :], carry_s.shape)
    return c


def _fox_key_bias(c):
    r = lax.broadcasted_iota(I32, (LANES, C_HEADS * LANES), 0)
    cidx = lax.broadcasted_iota(I32, (LANES, C_HEADS * LANES), 1)
    placed = 0.0
    for j, term in enumerate(_three_terms(-LOG2E * c)):
        put = jnp.where((cidx == r * LANES + HEAD_DIM + j) & (r < C_HEADS), 1.0, 0.0).astype(BF16)
        placed = placed + jnp.dot(term, put, preferred_element_type=F32)
    return placed


def _store_fox_keys(out_ref, kc, c):
    bias = _lane_tiles(_fox_key_bias(c))
    for p, pair in enumerate(_lane_tiles(kc)):
        for u, tile in enumerate(_split_pair(pair)):
            h = 2 * p + u
            out_ref[:, h * LANES:(h + 1) * LANES] = (tile + bias[h]).astype(out_ref.dtype)


def _odd_proj_kernel(x_ref, g_ref, w_ref, bf_ref, c0_ref,
                     qc_o, kc_o, vc_o, qd_o, kd_o, vd_o, kc_s, vc_s, kd_s, vd_s, logf_s, carry_s,
                     *, tiles_per_seq):
    h = _rmsnorm_rows(x_ref[...], g_ref[...]).astype(BF16)
    p = jnp.dot(h, w_ref[...], preferred_element_type=F32)
    scale = HEAD_DIM ** -0.5 * LOG2E
    ones_col = _one_hot_row(HEAD_DIM)

    f = p[:, _O_F:_O_QD] + bf_ref[...]
    logf = jnp.minimum(f, 0.0) - jnp.log1p(jnp.exp(-jnp.abs(f)))
    logf = jnp.where(_lane_index(logf.shape) < C_HEADS, logf, 0.0)
    logf_s[...] = logf[:, :C_HEADS]
    c = _cumulative(logf, carry_s, c0_ref, pl.program_id(0) % tiles_per_seq == 0)

    kc, vc, kd, vd = p[:, _O_KC:_O_VC], p[:, _O_VC:_O_F], p[:, _O_KD:_O_VD], p[:, _O_VD:_O_END]
    kc_s[...], vc_s[...], kd_s[...], vd_s[...] = kc, vc, kd, vd
    _store_heads(qc_o, p[:, _O_QC:_O_KC], scale=scale,
                 extra=_one_hot_row(*range(HEAD_DIM, HEAD_DIM + FOX_BIAS_TERMS)))
    _store_fox_keys(kc_o, kc, c)
    _store_heads(vc_o, vc, extra=ones_col)
    _store_heads(qd_o, p[:, _O_QD:_O_KD], scale=scale)
    _store_heads(kd_o, kd)
    _store_heads(vd_o, vd)


def _odd_weights(w_in):
    cuts = np.cumsum(ODD_SPLIT)[:-1].tolist()
    qc, kc, vc, fc, qd, kd, vd = jnp.split(w_in, cuts, axis=1)
    w = jnp.concatenate([qc, kc, vc, _pad_cols(fc, LANES), qd, kd, vd], axis=1)
    assert w.shape[1] == _O_END
    return w.astype(BF16)


def _odd_proj(x, g, w, b_f, c0, *, seq_len, tm=256):
    m, d = x.shape
    tm = min(tm, seq_len)
    assert seq_len % tm == 0
    tiles_per_seq = seq_len // tm
    row = lambda cols: pl.BlockSpec((tm, cols), lambda i: (i, 0))
    const = lambda a: pl.BlockSpec(a.shape, lambda i: (0,) * a.ndim)
    g2 = g.reshape(1, d)
    bf2 = _pad_cols(b_f.reshape(1, C_HEADS), LANES)
    wide, flat = C_HEADS * LANES, C_HEADS * HEAD_DIM
    outs = [(wide, BF16)] * 6 + [(flat, F32)] * 4 + [(C_HEADS, F32)]
    return pl.pallas_call(
        functools.partial(_odd_proj_kernel, tiles_per_seq=tiles_per_seq),
        grid=(m // tm,),
        in_specs=[row(d), const(g2), const(w), const(bf2),
                  pl.BlockSpec((1, 1, LANES), lambda i: (i // tiles_per_seq, 0, 0))],
        out_specs=[row(c) for c, _ in outs],
        out_shape=[jax.ShapeDtypeStruct((m, c), t) for c, t in outs],
        scratch_shapes=[pltpu.VMEM((8, LANES), F32)],
        compiler_params=_params("arbitrary"),
        name="odd_proj",
    )(x, g2, w, bf2, c0)


def _fox_past_kernel(k_ref, logf_ref, c0_ref, k_o, cend_o, carry_s, *, tiles_per_seq):
    logf = jnp.pad(logf_ref[...], ((0, 0), (0, LANES - C_HEADS)))
    c = _cumulative(logf, carry_s, c0_ref, pl.program_id(0) % tiles_per_seq == 0)
    _store_fox_keys(k_o, k_ref[...], c)
    cend_o[0] = c[c.shape[0] - 1:, :]


def _fox_past(k, logf, *, seq_len, tm=256):
    m = k.shape[0]
    nseq = m // seq_len
    tm = min(tm, seq_len)
    tiles_per_seq = seq_len // tm
    row = lambda cols: pl.BlockSpec((tm, cols), lambda i: (i, 0))
    per_seq = pl.BlockSpec((1, 1, LANES), lambda i: (i // tiles_per_seq, 0, 0))
    return pl.pallas_call(
        functools.partial(_fox_past_kernel, tiles_per_seq=tiles_per_seq),
        grid=(m // tm,),
        in_specs=[row(C_HEADS * HEAD_DIM), row(C_HEADS), per_seq],
        out_specs=[row(C_HEADS * LANES), per_seq],
        out_shape=[jax.ShapeDtypeStruct((m, C_HEADS * LANES), BF16),
                   jax.ShapeDtypeStruct((nseq, 1, LANES), F32)],
        scratch_shapes=[pltpu.VMEM((8, LANES), F32)],
        compiler_params=_params("arbitrary"),
        name="fox_past",
    )(k, logf, jnp.zeros((nseq, 1, LANES), F32))


def _head_lanes(parts, ones_col=False):
    bsz, s_len, heads = parts[0].shape[:3]
    parts = [p.astype(BF16) for p in parts]
    used = sum(p.shape[-1] for p in parts)
    if ones_col:
        parts.append(jnp.ones((bsz, s_len, heads, 1), BF16))
        used += 1
    parts.append(jnp.zeros((bsz, s_len, heads, LANES - used), BF16))
    return jnp.concatenate(parts, axis=-1).reshape(bsz, s_len, heads * LANES)


def _keys(past, new, sk_pad):
    rows = new if past is None else jnp.concatenate([past, new], axis=1)
    pad = sk_pad - rows.shape[1]
    return jnp.pad(rows, ((0, 0), (0, pad), (0, 0))) if pad else rows


def _tiles(s_len, past_len):
    tq = min(256, s_len)
    tk, group = (256, 4) if past_len == 0 else (128, 3)
    return tq, tk, group, _round_up(past_len + s_len, tk * group)


def _fine_group(group):
    return group // 2 if group % 2 == 0 else group


def _even_mixer(x, h_gain, tables, past, weights, g_bq, g_bkv, w_o):
    bsz, s_len, d = x.shape
    m = bsz * s_len
    past_len = 0 if past is None else past[0].shape[1]
    sk = past_len + s_len
    tq, tk, group, sk_pad = _tiles(s_len, past_len)
    outs = _even_proj(x.reshape(m, d), h_gain, weights, g_bq, g_bkv, tables, seq_len=s_len)
    qa, ka, va, qi, ki, wi, qb, kb, vb, ka_s, va_s, ki_s, lat_s, kr_s = [
        o.reshape(bsz, s_len, -1) for o in outs]
    new_rows = (ka_s.reshape(bsz, s_len, A_KV_HEADS, HEAD_DIM), va_s.reshape(bsz, s_len, A_KV_HEADS, HEAD_DIM),
                ki_s, lat_s, kr_s)
    if past is None:
        pa = (None,) * 5
    else:
        c_k, c_v, c_ki, c_lat, c_kr = past
        kv = _proj(c_lat.reshape(bsz * past_len, KV_RANK), jnp.ones((KV_RANK,), F32), weights[2], norm=False,
                   tm=past_len).reshape(bsz, past_len, 2 * B_HEADS, LANES)
        kr_t = jnp.pad(c_kr, ((0, 0), (0, 0), (_B_ROPE_LANE, LANES - _B_ROPE_LANE - B_ROPE)))[:, :, None, :]
        ones_t = jnp.zeros((LANES,), F32).at[HEAD_DIM].set(1.0)
        pa = (_head_lanes([c_k]), _head_lanes([c_v], ones_col=True), c_ki.astype(BF16),
              (kv[:, :, :B_HEADS] + kr_t).astype(BF16).reshape(bsz, past_len, -1),
              (kv[:, :, B_HEADS:] + ones_t).astype(BF16).reshape(bsz, past_len, -1))

    out_a = _dsa_attention(qa, _keys(pa[0], ka, sk_pad), _keys(pa[1], va, sk_pad), qi, _keys(pa[2], ki, sk_pad),
                           wi, q_off=past_len, sk=sk, tq=tq, tk=tk, group=group)
    out_b = _softmax_attention(qb, _keys(pa[3], kb, sk_pad), _keys(pa[4], vb, sk_pad),
                               heads=B_HEADS, mask_kind="chunk", q_off=past_len, sk=sk, tq=tq, tk=tk,
                               group=_fine_group(group), name="mla_attention")
    mix = jnp.concatenate([out_a, out_b], axis=-1).reshape(m, -1)
    y = _matmul_residual(mix, w_o, x.reshape(m, d)).reshape(bsz, s_len, d)
    return y, new_rows


def _odd_mixer(x, h_gain, past, w, b_f, w_o):
    bsz, s_len, d = x.shape
    m = bsz * s_len
    past_len = 0 if past is None else past[0].shape[1]
    sk = past_len + s_len
    tq, tk, group, sk_pad = _tiles(s_len, past_len)
    if past is None:
        pa = (None,) * 4
        c0 = jnp.zeros((bsz, 1, LANES), F32)
    else:
        c_k, c_v, c_logf, d_k, d_v = past
        kc_past, c0 = _fox_past(c_k.reshape(bsz * past_len, -1), c_logf.reshape(bsz * past_len, C_HEADS),
                                seq_len=past_len)
        pa = (kc_past.reshape(bsz, past_len, -1), _head_lanes([c_v], ones_col=True),
              _head_lanes([d_k]), _head_lanes([d_v]))
    outs = _odd_proj(x.reshape(m, d), h_gain, w, b_f, c0, seq_len=s_len)
    qc, kc, vc, qd, kd, vd, kc_s, vc_s, kd_s, vd_s, logf = [o.reshape(bsz, s_len, -1) for o in outs]
    heads = lambda a: a.reshape(bsz, s_len, -1, HEAD_DIM)
    new_rows = (heads(kc_s), heads(vc_s), logf, heads(kd_s), heads(vd_s))

    out_c = _softmax_attention(qc, _keys(pa[0], kc, sk_pad), _keys(pa[1], vc, sk_pad),
                               heads=C_HEADS, mask_kind="causal", q_off=past_len, sk=sk, tq=tq, tk=tk,
                               group=_fine_group(group), name="fox_attention")
    out_d = _sb_attention(qd, _keys(pa[2], kd, sk_pad), _keys(pa[3], vd, sk_pad),
                          q_off=past_len, sk=sk, tq=tq, tk=tk)
    mix = jnp.concatenate([out_c, out_d], axis=-1).reshape(m, -1)
    y = _matmul_residual(mix, w_o, x.reshape(m, d)).reshape(bsz, s_len, d)
    return y, new_rows


def _final_norm_kernel(x_ref, g_ref, o_ref):
    x = x_ref[...]
    ms = jnp.mean(x * x, axis=-1, keepdims=True)
    o_ref[...] = x * lax.rsqrt(ms + EPS) * g_ref[...]


def _final_norm(x, g, *, tm=512):
    m, d = x.shape
    tm = min(tm, m)
    return pl.pallas_call(
        _final_norm_kernel,
        grid=(m // tm,),
        in_specs=[pl.BlockSpec((tm, d), lambda i: (i, 0)), pl.BlockSpec((1, d), lambda i: (0, 0))],
        out_specs=pl.BlockSpec((tm, d), lambda i: (i, 0)),
        out_shape=jax.ShapeDtypeStruct((m, d), F32),
        compiler_params=_params("parallel"),
        name="final_norm",
    )(x, g.reshape(1, d).astype(F32))


def _trunk(x, q_pos, caches, params, even_w, odd_w):
    (g_mix, g_ffn, g_final, w_in_even, g_b_q, g_b_kv, w_b_uq, w_b_ukv, w_o_even,
     w_in_odd, b_forget, w_o_odd, w_up, w_conv, b_conv, w_down) = params
    bsz, s_len, d = x.shape
    depth = g_mix.shape[0]
    even_rows = [[] for _ in range(5)]
    odd_rows = [[] for _ in range(5)]
    conv_rows = []
    rows = max(s_len, min(512, bsz * s_len))
    tables = (_rope_tables(q_pos, rows, period=HEAD_DIM, start=0, n_rot=ROT_DIM),
              _rope_tables(q_pos, rows, period=IDX_DIM, start=0, n_rot=IDX_ROT),
              _rope_tables(q_pos, rows, period=LANES, start=_B_ROPE_LANE, n_rot=B_ROPE))
    for l in range(depth):
        j = l // 2
        if l % 2 == 0:
            past = None if caches is None else tuple(c[j] for c in caches[0:5])
            x, rows = _even_mixer(x, g_mix[l], tables, past, even_w[j], g_b_q[j], g_b_kv[j], w_o_even[j])
            for lst, r in zip(even_rows, rows):
                lst.append(r)
        else:
            past = None if caches is None else tuple(c[j] for c in caches[5:10])
            x, rows = _odd_mixer(x, g_mix[l], past, odd_w[j], b_forget[j], w_o_odd[j])
            for lst, r in zip(odd_rows, rows):
                lst.append(r)
        state = jnp.zeros((bsz, CONV_W - 1, D_FF), F32) if caches is None else caches[10][l]
        y, new_buf = _conv_ffn(x.reshape(bsz * s_len, d), g_ffn[l], w_up[l], w_conv[l], b_conv[l], w_down[l],
                               state, seq_len=s_len, tm=1024)
        x = y.reshape(bsz, s_len, d)
        conv_rows.append(new_buf)
    out = _final_norm(x.reshape(bsz * s_len, d), g_final).reshape(bsz, s_len, d)
    states = [jnp.stack(r, axis=0) for r in even_rows + odd_rows] + [jnp.stack(conv_rows, axis=0)]
    return out, states


def kernel(x_prompt, x_sample, cache_a_k, cache_a_v, cache_a_idx_k, cache_b_latent, cache_b_rope,
           cache_c_k, cache_c_v, cache_c_logf, cache_d_k, cache_d_v, state_ffn_conv,
           g_mix, g_ffn, g_final, w_in_even, g_b_q, g_b_kv, w_b_uq, w_b_ukv, w_o_even,
           w_in_odd, b_forget, w_o_odd, w_up, w_conv, b_conv, w_down):
    params = (g_mix, g_ffn, g_final, w_in_even, g_b_q, g_b_kv, w_b_uq, w_b_ukv, w_o_even,
              w_in_odd, b_forget, w_o_odd, w_up, w_conv, b_conv, w_down)
    caches = (cache_a_k, cache_a_v, cache_a_idx_k, cache_b_latent, cache_b_rope,
              cache_c_k, cache_c_v, cache_c_logf, cache_d_k, cache_d_v, state_ffn_conv)
    past_len = cache_a_k.shape[2]
    pos_prompt = jnp.arange(x_prompt.shape[1], dtype=I32)
    pos_sample = past_len + jnp.arange(x_sample.shape[1], dtype=I32)
    even_w = [_even_weights(w_in_even[j], w_b_uq[j], w_b_ukv[j]) for j in range(w_in_even.shape[0])]
    odd_w = [_odd_weights(w_in_odd[j]) for j in range(w_in_odd.shape[0])]
    y_prompt, p_states = _trunk(x_prompt, pos_prompt, None, params, even_w, odd_w)
    y_sample, s_states = _trunk(x_sample, pos_sample, caches, params, even_w, odd_w)
    return (y_prompt, y_sample, *p_states, *s_states)
```

```python
import functools

import jax
import jax.numpy as jnp
import numpy as np
from jax import lax
from jax.experimental import pallas as pl
from jax.experimental.pallas import tpu as pltpu

F32 = jnp.float32
BF16 = jnp.bfloat16
I32 = jnp.int32

CHUNK = 64
ROPE_THETA = 500000.0
EPS = 1e-6
HEAD_DIM = 64
ROT_DIM = HEAD_DIM // 4
A_HEADS = 8
A_KV_HEADS = 2
IDX_HEADS = 8
IDX_DIM = 32
IDX_ROT = IDX_DIM // 4
TOPK_MAX = 256
B_HEADS = 8
Q_RANK = 256
KV_RANK = 128
B_NOPE = 64
B_ROPE = 32
B_VDIM = 64
C_HEADS = 8
D_HEADS = 8
D_FF = 2816
CONV_W = 3

EVEN_SPLIT = [A_HEADS * HEAD_DIM, A_KV_HEADS * HEAD_DIM, A_KV_HEADS * HEAD_DIM,
              IDX_HEADS * IDX_DIM, IDX_DIM, IDX_HEADS, Q_RANK, KV_RANK, B_ROPE]
ODD_SPLIT = [C_HEADS * HEAD_DIM] * 3 + [C_HEADS] + [D_HEADS * HEAD_DIM] * 3

LANES = 128
VMEM_LIMIT_BYTES = 56 * 1024 * 1024
MASKED = -1e30
INT_MIN = -2 ** 31
LOG2E = 1.4426950408889634
F32_EXP2_UNDERFLOW = -152.0

_NT = (((1,), (1,)), ((), ()))


def _params(*sem):
    return pltpu.CompilerParams(dimension_semantics=sem, vmem_limit_bytes=VMEM_LIMIT_BYTES)


def _round_up(n, m):
    return (n + m - 1) // m * m


def _chunk_of(pos):
    return jnp.right_shift(pos, CHUNK.bit_length() - 1)


def _proj_kernel(x_ref, g_ref, w_ref, *out_refs, norm, emit_h):
    x = x_ref[...]
    if norm:
        ms = jnp.mean(x * x, axis=-1, keepdims=True)
        x = x * lax.rsqrt(ms + EPS) * g_ref[...]
    out_refs[0][...] = jnp.dot(x.astype(BF16), w_ref[...], preferred_element_type=F32)
    if emit_h:
        out_refs[1][...] = x


def _proj(x, g, w, *, norm=True, emit_h=False, tm=512):
    m, k = x.shape
    n = w.shape[1]
    tm = min(tm, m)
    assert m % tm == 0
    out_shape = [jax.ShapeDtypeStruct((m, n), F32)]
    out_specs = [pl.BlockSpec((tm, n), lambda i: (i, 0))]
    if emit_h:
        out_shape.append(jax.ShapeDtypeStruct((m, k), F32))
        out_specs.append(pl.BlockSpec((tm, k), lambda i: (i, 0)))
    res = pl.pallas_call(
        functools.partial(_proj_kernel, norm=norm, emit_h=emit_h),
        grid=(m // tm,),
        in_specs=[pl.BlockSpec((tm, k), lambda i: (i, 0)),
                  pl.BlockSpec((1, k), lambda i: (0, 0)),
                  pl.BlockSpec((k, n), lambda i: (0, 0))],
        out_specs=out_specs,
        out_shape=out_shape,
        compiler_params=_params("parallel"),
        name="proj",
    )(x, g.reshape(1, k).astype(F32), w.astype(BF16))
    return res if emit_h else res[0]


def _matmul_residual_kernel(a_ref, w_ref, r_ref, o_ref):
    o_ref[...] = r_ref[...] + jnp.dot(a_ref[...], w_ref[...], preferred_element_type=F32)


def _matmul_residual(a, w, res, *, tm=512):
    m, k = a.shape
    n = w.shape[1]
    tm = min(tm, m)
    assert m % tm == 0
    return pl.pallas_call(
        _matmul_residual_kernel,
        grid=(m // tm,),
        in_specs=[pl.BlockSpec((tm, k), lambda i: (i, 0)),
                  pl.BlockSpec((k, n), lambda i: (0, 0)),
                  pl.BlockSpec((tm, n), lambda i: (i, 0))],
        out_specs=pl.BlockSpec((tm, n), lambda i: (i, 0)),
        out_shape=jax.ShapeDtypeStruct((m, n), F32),
        compiler_params=_params("parallel"),
        name="matmul_residual",
    )(a, w.astype(BF16), res)


def _conv_ffn_kernel(x_ref, g_ref, wg_ref, wu_ref, wc_ref, bc_ref, wd_ref, st_ref,
                     y_ref, ns_ref, h_s, acc_s, carry_s, *, tiles_per_seq, seqs, tm, tf):
    i = pl.program_id(0)
    c = pl.program_id(1)
    nc = pl.num_programs(1)
    cols = pl.ds(pl.multiple_of(c * tf, tf), tf)
    rows_per = tm // seqs

    @pl.when(c == 0)
    def _():
        x = x_ref[...]
        ms = jnp.mean(x * x, axis=-1, keepdims=True)
        h_s[...] = (x * lax.rsqrt(ms + EPS) * g_ref[...]).astype(BF16)
        acc_s[...] = jnp.zeros_like(acc_s)

    h = h_s[...]
    gate = jnp.dot(h, wg_ref[...], preferred_element_type=F32)
    up = jnp.dot(h, wu_ref[...], preferred_element_type=F32)

    if seqs == 1:
        @pl.when(i % tiles_per_seq == 0)
        def _():
            carry_s[c] = st_ref[0, :, cols]
        before = [carry_s[c]]
    else:
        before = [st_ref[s, :, cols] for s in range(seqs)]
    row = lax.broadcasted_iota(I32, gate.shape, 0)
    g1 = pltpu.roll(gate, 1, 0)
    g2 = pltpu.roll(gate, 2, 0)
    for s, prev in enumerate(before):
        first = s * rows_per
        g1 = jnp.where(row == first, prev[1:2, :], g1)
        g2 = jnp.where(row == first, prev[0:1, :], jnp.where(row == first + 1, prev[1:2, :], g2))
    wc = wc_ref[...]
    gc = bc_ref[...] + g2 * wc[0:1, :]
    gc = gc + g1 * wc[1:2, :]
    gc = gc + gate * wc[2:3, :]
    act = gc * jax.nn.sigmoid(gc) * up
    acc_s[...] += jnp.dot(act.astype(BF16), wd_ref[...], preferred_element_type=F32)

    for s in range(seqs):
        ns_ref[s, :, cols] = gate[(s + 1) * rows_per - 2:(s + 1) * rows_per, :]
    if seqs == 1:
        carry_s[c] = gate[tm - 2:tm, :]

    @pl.when(c == nc - 1)
    def _():
        y_ref[...] = x_ref[...] + acc_s[...]


def _conv_ffn(x, g, w_up, w_conv, b_conv, w_down, state, *, seq_len, tm, tf=256):
    m, d = x.shape
    nseq = m // seq_len
    tm = min(tm, m)
    assert m % tm == 0 and D_FF % tf == 0 and (seq_len % tm == 0 or tm % seq_len == 0)
    tiles_per_seq = max(1, seq_len // tm)
    seqs = max(1, tm // seq_len)
    nc = D_FF // tf
    wg = w_up[:, :D_FF].astype(BF16)
    wu = w_up[:, D_FF:].astype(BF16)
    state_spec = pl.BlockSpec((seqs, CONV_W - 1, D_FF), lambda i, c: (i // tiles_per_seq, 0, 0))
    y, ns = pl.pallas_call(
        functools.partial(_conv_ffn_kernel, tiles_per_seq=tiles_per_seq, seqs=seqs, tm=tm, tf=tf),
        grid=(m // tm, nc),
        in_specs=[pl.BlockSpec((tm, d), lambda i, c: (i, 0)),
                  pl.BlockSpec((1, d), lambda i, c: (0, 0)),
                  pl.BlockSpec((d, tf), lambda i, c: (0, c)),
                  pl.BlockSpec((d, tf), lambda i, c: (0, c)),
                  pl.BlockSpec((CONV_W, tf), lambda i, c: (0, c)),
                  pl.BlockSpec((1, tf), lambda i, c: (0, c)),
                  pl.BlockSpec((tf, d), lambda i, c: (c, 0)),
                  state_spec],
        out_specs=[pl.BlockSpec((tm, d), lambda i, c: (i, 0)), state_spec],
        out_shape=[jax.ShapeDtypeStruct((m, d), F32),
                   jax.ShapeDtypeStruct((nseq, CONV_W - 1, D_FF), F32)],
        scratch_shapes=[pltpu.VMEM((tm, d), BF16),
                        pltpu.VMEM((tm, d), F32),
                        pltpu.VMEM((nc, CONV_W - 1, tf), F32)],
        compiler_params=_params("arbitrary", "arbitrary"),
        name="conv_ffn",
    )(x, g.reshape(1, d).astype(F32), wg, wu, w_conv.astype(F32), b_conv.reshape(1, D_FF).astype(F32),
      w_down.astype(BF16), state.astype(F32))
    return y, ns


def _attn_call(kernel, inputs, in_specs, out_cols, *, bsz, sq, tq, scratch=(), name):
    return pl.pallas_call(
        kernel,
        grid=(bsz, sq // tq),
        in_specs=in_specs,
        out_specs=pl.BlockSpec((1, tq, out_cols), lambda b, i: (b, i, 0)),
        out_shape=jax.ShapeDtypeStruct((bsz, sq, out_cols), BF16),
        scratch_shapes=list(scratch),
        compiler_params=_params("parallel", "arbitrary"),
        name=name,
    )(*inputs)


def _q_spec(tq, cols):
    return pl.BlockSpec((1, tq, cols), lambda b, i: (b, i, 0))


def _k_spec(rows, cols):
    return pl.BlockSpec((1, rows, cols), lambda b, i: (b, 0, 0))


def _tile_geometry(i, *, tq, span, q_off):
    q_lo = q_off + i * tq
    return q_lo, q_lo // span


def _lane_tiles(x):
    return [x[:, j * LANES:(j + 1) * LANES] for j in range(x.shape[1] // LANES)]


def _visible(kind, q_lo, ks, *, tq, tk, sk):
    q_pos = q_lo + lax.broadcasted_iota(I32, (tq, 1), 0)
    k_pos = ks + lax.broadcasted_iota(I32, (1, tk), 1)
    if kind == "chunk":
        return (_chunk_of(k_pos) <= _chunk_of(q_pos)) & (k_pos < sk)
    if kind == "causal":
        return k_pos <= q_pos
    assert kind == "strict"
    return k_pos < q_pos


HEADS_PER_STEP = 2


def _span_cases(i, body, *, tq, span, q_off, n_tiles):
    g_own = (q_off + i * tq) // span
    for g in sorted({(q_off + t * tq) // span for t in range(n_tiles)}):
        pl.when(g_own == g)(functools.partial(body, g + 1))


def _softmax_heads(qs, k_ref, v_ref, s_scr, kcols, vcols, *, tq, tk, blocks, own_from, own_mask,
                   bias_ref=None):
    row_max = []
    for slot, (q, kcol) in enumerate(zip(qs, kcols)):
        mx = jnp.full((tq, LANES), MASKED, F32)
        for b in range(blocks):
            ks = b * tk
            s = lax.dot_general(q, k_ref[0, ks:ks + tk, kcol], _NT, preferred_element_type=F32)
            if bias_ref is not None:
                s = s + bias_ref[:, ks:ks + tk]
            if own_mask is not None and b >= own_from:
                s = jnp.where(own_mask(ks), s, MASKED)
            s_scr[slot, :, ks:ks + tk] = s
            for t in _lane_tiles(s):
                mx = jnp.maximum(mx, t)
        row_max.append(jnp.broadcast_to(jnp.max(mx, axis=1, keepdims=True), (tq, LANES)))
    outs = []
    for slot, (m, vcol) in enumerate(zip(row_max, vcols)):
        acc = jnp.zeros((tq, LANES), F32)
        for b in range(blocks):
            ks = b * tk
            p = jnp.concatenate([jnp.exp2(t - m) for t in _lane_tiles(s_scr[slot, :, ks:ks + tk])], axis=1)
            acc = acc + jnp.dot(p.astype(BF16), v_ref[0, ks:ks + tk, vcol], preferred_element_type=F32)
        outs.append(acc[:, :HEAD_DIM] / acc[:, HEAD_DIM:HEAD_DIM + 1])
    return outs


def _lane_tile_at(index):
    return pl.ds(pl.multiple_of(index * LANES, LANES), LANES)


def _softmax_attn_kernel(q_ref, k_ref, v_ref, o_ref, s_scr, *, heads, mask_kind, tq, tk, group, q_off, sk,
                         n_tiles):
    i = pl.program_id(1)
    own_mask = functools.partial(_visible, mask_kind, q_off + i * tq, tq=tq, tk=tk, sk=sk)

    def attend(n_spans):
        def step(hp, _):
            cols = [_lane_tile_at(hp * HEADS_PER_STEP + u) for u in range(HEADS_PER_STEP)]
            outs = _softmax_heads([q_ref[0, :, c] for c in cols], k_ref, v_ref, s_scr, cols, cols,
                                  tq=tq, tk=tk, blocks=n_spans * group, own_from=(n_spans - 1) * group,
                                  own_mask=own_mask)
            o_ref[0, :, _lane_tile_at(hp)] = jnp.concatenate(outs, axis=1).astype(o_ref.dtype)
            return 0
        lax.fori_loop(0, heads // HEADS_PER_STEP, step, 0)

    _span_cases(i, attend, tq=tq, span=tk * group, q_off=q_off, n_tiles=n_tiles)


def _check_tiling(sk_pad, *, tq, tk, group, q_off):
    assert (tk * group) % tq == 0 and q_off % tq == 0 and sk_pad % (tk * group) == 0 and tk % LANES == 0
    assert HEADS_PER_STEP * HEAD_DIM == LANES


def _softmax_attention(q, k, v, *, heads, mask_kind, q_off, sk, tq, tk, group, name):
    bsz, sq, _ = q.shape
    sk_pad = k.shape[1]
    _check_tiling(sk_pad, tq=tq, tk=tk, group=group, q_off=q_off)
    kern = functools.partial(_softmax_attn_kernel, heads=heads, mask_kind=mask_kind,
                             tq=tq, tk=tk, group=group, q_off=q_off, sk=sk, n_tiles=sq // tq)
    cols = heads * LANES
    return _attn_call(kern, (q, k, v), [_q_spec(tq, cols), _k_spec(sk_pad, cols), _k_spec(sk_pad, cols)],
                      heads * HEAD_DIM, bsz=bsz, sq=sq, tq=tq,
                      scratch=(pltpu.VMEM((HEADS_PER_STEP, tq, sk_pad), F32),), name=name)


def _split_bf16(x):
    hi = x.astype(BF16)
    lo = (x - hi.astype(F32)).astype(BF16)
    return hi, lo


SB_HEADS_PER_LOOP = 2


def _sb_kernel(q_ref, k_ref, v_ref, o_ref, *, tq, tk, q_off, sk):
    q_lo, n_before = _tile_geometry(pl.program_id(1), tq=tq, span=tk, q_off=q_off)
    r = lax.broadcasted_iota(I32, (tk, tk), 0)
    cidx = lax.broadcasted_iota(I32, (tk, tk), 1)
    later = jnp.where(r > cidx, 1.0, 0.0).astype(BF16)

    def block(q, h, kb, run, acc, mask):
        ks = pl.multiple_of(kb * tk, tk)
        k = k_ref[0, pl.ds(ks, tk), h * LANES:(h + 1) * LANES]
        v = v_ref[0, pl.ds(ks, tk), h * LANES:(h + 1) * LANES]
        z = lax.dot_general(q, k, _NT, preferred_element_type=F32)
        log_beta = jnp.minimum(z, 0.0) - jnp.log2(1.0 + jnp.exp2(-jnp.abs(z)))
        log_1m = log_beta - z
        if mask is not None:
            log_1m = jnp.where(mask, log_1m, 0.0)
        hi, lo = _split_bf16(log_1m)
        after = (jnp.dot(hi, later, preferred_element_type=F32)
                 + jnp.dot(lo, later, preferred_element_type=F32))
        a = jnp.exp2(log_beta + after + run)
        if mask is not None:
            a = jnp.where(mask, a, 0.0)
        acc = acc + jnp.dot(a.astype(BF16), v, preferred_element_type=F32)
        run = run + after[:, 0:1] + log_1m[:, 0:1]
        return run, acc

    own = _visible("strict", q_lo, n_before * tk, tq=tq, tk=tk, sk=sk)
    for h0 in range(0, D_HEADS, SB_HEADS_PER_LOOP):
        hs = range(h0, h0 + SB_HEADS_PER_LOOP)
        qs = [q_ref[0, :, h * LANES:(h + 1) * LANES] for h in hs]
        state = []
        for h, q in zip(hs, qs):
            state.extend(block(q, h, n_before, jnp.zeros((tq, 1), F32), jnp.zeros((tq, LANES), F32), own))

        def alive(state):
            top = functools.reduce(jnp.maximum, state[0::2])
            return jnp.max(top) > F32_EXP2_UNDERFLOW

        def cond(c):
            return (c[0] < n_before) & c[1]

        def body(c, hs=hs, qs=qs):
            j, _, state = c
            new = []
            for n, (h, q) in enumerate(zip(hs, qs)):
                new.extend(block(q, h, n_before - 1 - j, state[2 * n], state[2 * n + 1], None))
            return j + 1, alive(new), tuple(new)

        _, _, state = lax.while_loop(cond, body, (jnp.int32(0), alive(state), tuple(state)))
        for n, h in enumerate(hs):
            o_ref[0, :, h * HEAD_DIM:(h + 1) * HEAD_DIM] = state[2 * n + 1][:, :HEAD_DIM].astype(o_ref.dtype)


def _sb_attention(q, k, v, *, q_off, sk, tq, tk):
    bsz, sq, _ = q.shape
    sk_pad = k.shape[1]
    _check_tiling(sk_pad, tq=tq, tk=tk, group=1, q_off=q_off)
    kern = functools.partial(_sb_kernel, tq=tq, tk=tk, q_off=q_off, sk=sk)
    cols = D_HEADS * LANES
    return _attn_call(kern, (q, k, v), [_q_spec(tq, cols), _k_spec(sk_pad, cols), _k_spec(sk_pad, cols)],
                      D_HEADS * HEAD_DIM, bsz=bsz, sq=sq, tq=tq, name="sb_attention")


def _sortable_key(score):
    bits = lax.bitcast_convert_type(score, I32)
    return jnp.where(bits < 0, bits ^ 0x7FFFFFFF, bits)


KEY_OF_NEG_INF = -0x7F800001


def _dsa_kernel(q_ref, k_ref, v_ref, qi_ref, ki_ref, wi_ref, o_ref, key_s, bias_s, s_scr, w_s,
                *, tq, tk, group, q_off, sk, topk, n_tiles):
    i = pl.program_id(1)
    nt = tk // LANES
    own_visible = functools.partial(_visible, "chunk", q_off + i * tq, tq=tq, tk=tk, sk=sk)
    kf = float(topk)

    for h in range(IDX_HEADS):
        w_s[h] = jnp.broadcast_to(wi_ref[0, :, h:h + 1], (tq, LANES))

    def select_and_attend(n_spans):
        blocks, own_from = n_spans * group, (n_spans - 1) * group
        starts = [b * tk for b in range(blocks)]

        def own(ks, value, masked):
            return jnp.where(own_visible(ks), value, masked) if ks >= own_from * tk else value

        qis = [qi_ref[0, :, h * IDX_DIM:(h + 1) * IDX_DIM] for h in range(IDX_HEADS)]
        for ks in starts:
            ki = ki_ref[0, ks:ks + tk, :]
            total = jnp.zeros((tq, tk), F32)
            for h in range(IDX_HEADS):
                sc = lax.dot_general(qis[h], ki, _NT, preferred_element_type=F32)
                total = total + jnp.maximum(sc, 0.0) * jnp.concatenate([w_s[h]] * nt, axis=1)
            key_s[:, ks:ks + tk] = _sortable_key(own(ks, total, -jnp.inf))

        def count(hit):
            cnt = jnp.zeros((tq, LANES), F32)
            for ks in starts:
                for t in _lane_tiles(key_s[:, ks:ks + tk]):
                    cnt = cnt + jnp.where(hit(t), 1.0, 0.0)
            return jnp.sum(cnt, axis=1, keepdims=True)

        def count_ge(cand):
            cand = jnp.broadcast_to(cand, (tq, LANES))
            return count(lambda t: t >= cand)

        lowest = jnp.full((tq, 1), INT_MIN, I32)
        zero = jnp.zeros((tq, 1), I32)
        n_zero = count_ge(zero)
        ok = n_zero >= kf
        thr = jnp.where(ok, zero, lowest)
        n_thr = jnp.where(ok, n_zero, float(blocks * tk))

        def bit_step(it, c):
            thr, n_thr = c
            cand = thr | jnp.left_shift(jnp.int32(1), 30 - it)
            n_cand = count_ge(cand)
            ok = n_cand >= kf
            return jnp.where(ok, cand, thr), jnp.where(ok, n_cand, n_thr)

        thr, n_thr = lax.fori_loop(0, 31, bit_step, (thr, n_thr))
        thr_b = jnp.broadcast_to(thr, (tq, LANES))

        tied = jnp.max(jnp.where(thr > KEY_OF_NEG_INF, n_thr, 0.0)) > kf

        @pl.when(jnp.logical_not(tied))
        def _():
            for ks in starts:
                tiles = _lane_tiles(key_s[:, ks:ks + tk])
                bias = jnp.concatenate([jnp.where(t >= thr_b, 0.0, MASKED) for t in tiles], axis=1)
                bias_s[:, ks:ks + tk] = own(ks, bias, MASKED)

        @pl.when(tied)
        def _():
            r = lax.broadcasted_iota(I32, (tk, tk), 0)
            cidx = lax.broadcasted_iota(I32, (tk, tk), 1)
            upto = jnp.where(r <= cidx, 1.0, 0.0).astype(BF16)
            need = kf - count(lambda t: t > thr_b)
            seen = jnp.zeros((tq, 1), F32)
            for ks in starts:
                key = key_s[:, ks:ks + tk]
                tie = jnp.where(key == thr, 1.0, 0.0)
                rank = seen + jnp.dot(tie.astype(BF16), upto, preferred_element_type=F32)
                sel = (key > thr) | ((key == thr) & (rank <= need))
                bias_s[:, ks:ks + tk] = own(ks, jnp.where(sel, 0.0, MASKED), MASKED)
                seen = seen + jnp.sum(tie, axis=1, keepdims=True)

        def step(hp, _):
            heads = [hp * HEADS_PER_STEP + u for u in range(HEADS_PER_STEP)]
            kv = [_lane_tile_at(hd // (A_HEADS // A_KV_HEADS)) for hd in heads]
            outs = _softmax_heads([q_ref[0, :, _lane_tile_at(hd)] for hd in heads], k_ref, v_ref, s_scr,
                                  kv, kv, tq=tq, tk=tk, blocks=blocks, own_from=own_from, own_mask=None,
                                  bias_ref=bias_s)
            o_ref[0, :, _lane_tile_at(hp)] = jnp.concatenate(outs, axis=1).astype(o_ref.dtype)
            return 0
        lax.fori_loop(0, A_HEADS // HEADS_PER_STEP, step, 0)

    _span_cases(i, select_and_attend, tq=tq, span=tk * group, q_off=q_off, n_tiles=n_tiles)


def _dsa_attention(q, k, v, qi, ki, wi, *, q_off, sk, tq, tk, group):
    bsz, sq, _ = q.shape
    sk_pad = k.shape[1]
    _check_tiling(sk_pad, tq=tq, tk=tk, group=group, q_off=q_off)
    topk = min(TOPK_MAX, sk // 4)
    kern = functools.partial(_dsa_kernel, tq=tq, tk=tk, group=group, q_off=q_off, sk=sk, topk=topk,
                             n_tiles=sq // tq)
    kv_cols = A_KV_HEADS * LANES
    return _attn_call(kern, (q, k, v, qi, ki, wi),
                      [_q_spec(tq, A_HEADS * LANES), _k_spec(sk_pad, kv_cols), _k_spec(sk_pad, kv_cols),
                       _q_spec(tq, IDX_HEADS * IDX_DIM), _k_spec(sk_pad, IDX_DIM), _q_spec(tq, IDX_HEADS)],
                      A_HEADS * HEAD_DIM, bsz=bsz, sq=sq, tq=tq,
                      scratch=(pltpu.VMEM((tq, sk_pad), I32), pltpu.VMEM((tq, sk_pad), F32),
                               pltpu.VMEM((HEADS_PER_STEP, tq, sk_pad), F32),
                               pltpu.VMEM((IDX_HEADS, tq, LANES), F32)),
                      name="dsa_attention")


def _rmsnorm_rows(x, g):
    ms = jnp.mean(x * x, axis=-1, keepdims=True)
    return x * lax.rsqrt(ms + EPS) * g


def _lane_index(shape):
    return lax.broadcasted_iota(I32, shape, len(shape) - 1)


def _one_hot_row(*lanes):
    lane = _lane_index((1, LANES))
    hit = functools.reduce(jnp.logical_or, [lane == l for l in lanes])
    return jnp.where(hit, 1.0, 0.0)


def _rotate(tile, tables, half):
    cos, sin_up, sin_dn = tables
    return tile * cos + pltpu.roll(tile, half, 1) * sin_up + pltpu.roll(tile, LANES - half, 1) * sin_dn


def _split_pair(pair):
    low = _lane_index(pair.shape) < HEAD_DIM
    return jnp.where(low, pair, 0.0), jnp.where(low, pltpu.roll(pair, HEAD_DIM, 1), 0.0)


def _store_heads(out_ref, compact, *, scale=None, extra=None):
    for p, pair in enumerate(_lane_tiles(compact)):
        for u, tile in enumerate(_split_pair(pair)):
            if scale is not None:
                tile = tile * scale
            if extra is not None:
                tile = tile + extra
            h = 2 * p + u
            out_ref[:, h * LANES:(h + 1) * LANES] = tile.astype(out_ref.dtype)


_E_QA, _E_KA, _E_VA, _E_QI, _E_KI, _E_WI, _E_CQ, _E_CKV, _E_KR, _E_END = (
    0, 512, 640, 768, 1024, 1152, 1280, 1536, 1664, 1792)
_B_ROPE_LANE = B_NOPE


def _even_proj_kernel(x_ref, g_ref, w_ref, gq_ref, wuq_ref, gkv_ref, wukv_ref, ta_ref, ti_ref, tb_ref,
                      qa_o, ka_o, va_o, qi_o, ki_o, wi_o, qb_o, kb_o, vb_o,
                      ka_s, va_s, ki_s, lat_s, kr_s):
    h = _rmsnorm_rows(x_ref[...], g_ref[...]).astype(BF16)
    p = jnp.dot(h, w_ref[...], preferred_element_type=F32)
    ta = (ta_ref[0], ta_ref[1], ta_ref[2])
    ti = (ti_ref[0], ti_ref[1], ti_ref[2])
    tb = (tb_ref[0], tb_ref[1], tb_ref[2])
    ones_col = _one_hot_row(HEAD_DIM)

    qa = jnp.concatenate([_rotate(t, ta, ROT_DIM // 2) for t in _lane_tiles(p[:, _E_QA:_E_KA])], axis=1)
    _store_heads(qa_o, qa, scale=HEAD_DIM ** -0.5 * LOG2E)
    ka = _rotate(p[:, _E_KA:_E_VA], ta, ROT_DIM // 2)
    ka_s[...] = ka
    _store_heads(ka_o, ka)
    va = p[:, _E_VA:_E_QI]
    va_s[...] = va
    _store_heads(va_o, va, extra=ones_col)
    qi = jnp.concatenate([_rotate(t, ti, IDX_ROT // 2) for t in _lane_tiles(p[:, _E_QI:_E_KI])], axis=1)
    qi_o[...] = (qi * IDX_DIM ** -0.5).astype(qi_o.dtype)
    ki = _rotate(p[:, _E_KI:_E_WI], ti, IDX_ROT // 2)[:, :IDX_DIM]
    ki_s[...] = ki
    ki_o[...] = ki.astype(ki_o.dtype)
    wi_o[...] = p[:, _E_WI:_E_WI + IDX_HEADS] * IDX_HEADS ** -0.5

    cq = _rmsnorm_rows(p[:, _E_CQ:_E_CKV], gq_ref[...]).astype(BF16)
    qb = jnp.dot(cq, wuq_ref[...], preferred_element_type=F32)
    scale_b = (B_NOPE + B_ROPE) ** -0.5 * LOG2E
    for hd, t in enumerate(_lane_tiles(qb)):
        qb_o[:, hd * LANES:(hd + 1) * LANES] = (_rotate(t, tb, B_ROPE // 2) * scale_b).astype(qb_o.dtype)
    lat = _rmsnorm_rows(p[:, _E_CKV:_E_KR], gkv_ref[...])
    lat_s[...] = lat
    kv = jnp.dot(lat.astype(BF16), wukv_ref[...], preferred_element_type=F32)
    kr = _rotate(p[:, _E_KR:_E_END], tb, B_ROPE // 2)
    kr_s[...] = kr[:, _B_ROPE_LANE:_B_ROPE_LANE + B_ROPE]
    _store_mla_keys(kb_o, vb_o, kv, kr)


def _store_mla_keys(kb_o, vb_o, kv, kr):
    ones_col = _one_hot_row(HEAD_DIM)
    tiles = _lane_tiles(kv)
    for hd in range(B_HEADS):
        kb_o[:, hd * LANES:(hd + 1) * LANES] = (tiles[hd] + kr).astype(kb_o.dtype)
        vb_o[:, hd * LANES:(hd + 1) * LANES] = (tiles[B_HEADS + hd] + ones_col).astype(vb_o.dtype)


def _mla_past_kernel(lat_ref, kr_ref, wukv_ref, kb_o, vb_o):
    kv = jnp.dot(lat_ref[...].astype(BF16), wukv_ref[...], preferred_element_type=F32)
    rows = kv.shape[0]
    kr = jnp.concatenate([jnp.zeros((rows, _B_ROPE_LANE), F32), kr_ref[...],
                          jnp.zeros((rows, LANES - _B_ROPE_LANE - B_ROPE), F32)], axis=1)
    _store_mla_keys(kb_o, vb_o, kv, kr)


def _mla_past(lat, kr, wukv, *, tm=512):
    m = lat.shape[0]
    tm = min(tm, m)
    assert m % tm == 0
    row = lambda cols: pl.BlockSpec((tm, cols), lambda i: (i, 0))
    wide = B_HEADS * LANES
    return pl.pallas_call(
        _mla_past_kernel,
        grid=(m // tm,),
        in_specs=[row(KV_RANK), row(B_ROPE), pl.BlockSpec(wukv.shape, lambda i: (0, 0))],
        out_specs=[row(wide), row(wide)],
        out_shape=[jax.ShapeDtypeStruct((m, wide), BF16)] * 2,
        compiler_params=_params("parallel"),
        name="mla_past",
    )(lat, kr, wukv)


def _pad_cols(w, width):
    return jnp.pad(w, ((0, 0), (0, width - w.shape[1])))


def _even_weights(w_in, w_uq, w_ukv):
    cuts = np.cumsum(EVEN_SPLIT)[:-1].tolist()
    qa, ka, va, qi, ki, wi, cq, ckv, kr = jnp.split(w_in, cuts, axis=1)
    kr = jnp.pad(kr, ((0, 0), (_B_ROPE_LANE, LANES - _B_ROPE_LANE - B_ROPE)))
    w = jnp.concatenate([qa, ka, va, qi, _pad_cols(ki, LANES), _pad_cols(wi, LANES), cq, ckv, kr], axis=1)
    d = w_in.shape[0]
    wuq = jnp.pad(w_uq.reshape(Q_RANK, B_HEADS, B_NOPE + B_ROPE),
                  ((0, 0), (0, 0), (0, LANES - B_NOPE - B_ROPE))).reshape(Q_RANK, B_HEADS * LANES)
    kvw = w_ukv.reshape(KV_RANK, B_HEADS, B_NOPE + B_VDIM)
    pad = lambda a: jnp.pad(a, ((0, 0), (0, 0), (0, LANES - a.shape[2]))).reshape(KV_RANK, B_HEADS * LANES)
    wukv = jnp.concatenate([pad(kvw[:, :, :B_NOPE]), pad(kvw[:, :, B_NOPE:])], axis=1)
    assert w.shape == (d, _E_END)
    return w.astype(BF16), wuq.astype(BF16), wukv.astype(BF16)


def _rope_tables(pos, rows, *, period, start, n_rot):
    half = n_rot // 2
    inv = ROPE_THETA ** (-jnp.arange(half, dtype=F32) * 2.0 / n_rot)
    ang = pos.astype(F32)[:, None] * inv[None, :]
    cos, sin = jnp.cos(ang), jnp.sin(ang)
    off = np.arange(LANES) % period - start
    idx = np.where((off >= 0) & (off < n_rot), off % half, 0)
    lower = (off >= 0) & (off < half)
    upper = (off >= half) & (off < n_rot)
    cos_t = jnp.where(lower | upper, cos[:, idx], 1.0)
    sin_up = jnp.where(upper, sin[:, idx], 0.0)
    sin_dn = jnp.where(lower, -sin[:, idx], 0.0)
    tabs = jnp.stack([cos_t, sin_up, sin_dn])
    return jnp.tile(tabs, (1, rows // pos.shape[0], 1))


def _even_proj(x, g, weights, g_bq, g_bkv, tables, *, seq_len, tm=512):
    m, d = x.shape
    w, wuq, wukv = weights
    tm = min(tm, m)
    table_rows = tables[0].shape[1]
    assert m % tm == 0 and table_rows % tm == 0 and (seq_len % tm == 0 or tm % seq_len == 0)
    tblocks = table_rows // tm
    row = lambda cols: pl.BlockSpec((tm, cols), lambda i: (i, 0))
    const = lambda a: pl.BlockSpec(a.shape, lambda i: (0,) * a.ndim)
    tspec = pl.BlockSpec((3, tm, LANES), lambda i: (0, i % tblocks, 0))
    g2, gq2, gkv2 = g.reshape(1, d), g_bq.reshape(1, Q_RANK), g_bkv.reshape(1, KV_RANK)
    kv_cols = A_KV_HEADS * HEAD_DIM
    outs = [(A_HEADS * LANES, BF16), (A_KV_HEADS * LANES, BF16), (A_KV_HEADS * LANES, BF16),
            (IDX_HEADS * IDX_DIM, BF16), (IDX_DIM, BF16), (IDX_HEADS, F32),
            (B_HEADS * LANES, BF16), (B_HEADS * LANES, BF16), (B_HEADS * LANES, BF16),
            (kv_cols, F32), (kv_cols, F32), (IDX_DIM, F32), (KV_RANK, F32), (B_ROPE, F32)]
    return pl.pallas_call(
        _even_proj_kernel,
        grid=(m // tm,),
        in_specs=[row(d), const(g2), const(w), const(gq2), const(wuq), const(gkv2), const(wukv),
                  tspec, tspec, tspec],
        out_specs=[row(c) for c, _ in outs],
        out_shape=[jax.ShapeDtypeStruct((m, c), t) for c, t in outs],
        compiler_params=_params("parallel"),
        name="even_proj",
    )(x, g2, w, gq2, wuq, gkv2, wukv, *tables)


_O_QC, _O_KC, _O_VC, _O_F, _O_QD, _O_KD, _O_VD, _O_END = 0, 512, 1024, 1536, 1664, 2176, 2688, 3200
FOX_BIAS_TERMS = 3


def _three_terms(x):
    hi = x.astype(BF16)
    r = x - hi.astype(F32)
    mid = r.astype(BF16)
    return hi, mid, (r - mid.astype(F32)).astype(BF16)


def _cumulative(logf, carry_s, c0_ref, restart):
    tm = logf.shape[0]

    @pl.when(restart)
    def _():
        carry_s[...] = jnp.broadcast_to(c0_ref[0], carry_s.shape)

    r = lax.broadcasted_iota(I32, (tm, tm), 0)
    cidx = lax.broadcasted_iota(I32, (tm, tm), 1)
    upto = jnp.where(cidx <= r, 1.0, 0.0).astype(BF16)
    c = carry_s[0:1, :] + sum(jnp.dot(upto, t, preferred_element_type=F32) for t in _three_terms(logf))
    carry_s[...] = jnp.broadcast_to(c[tm - 1:tm, :], carry_s.shape)
    return c


def _fox_key_bias(c):
    r = lax.broadcasted_iota(I32, (LANES, C_HEADS * LANES), 0)
    cidx = lax.broadcasted_iota(I32, (LANES, C_HEADS * LANES), 1)
    placed = 0.0
    for j, term in enumerate(_three_terms(-LOG2E * c)):
        put = jnp.where((cidx == r * LANES + HEAD_DIM + j) & (r < C_HEADS), 1.0, 0.0).astype(BF16)
        placed = placed + jnp.dot(term, put, preferred_element_type=F32)
    return placed


def _store_fox_keys(out_ref, kc, c):
    bias = _lane_tiles(_fox_key_bias(c))
    for p, pair in enumerate(_lane_tiles(kc)):
        for u, tile in enumerate(_split_pair(pair)):
            h = 2 * p + u
            out_ref[:, h * LANES:(h + 1) * LANES] = (tile + bias[h]).astype(out_ref.dtype)


def _odd_proj_kernel(x_ref, g_ref, w_ref, bf_ref, c0_ref, *refs, tiles_per_seq):
    qc_o, kc_o, vc_o, qd_o, kd_o, vd_o, kc_s, vc_s, kd_s, vd_s, logf_s, carry_s = refs[-12:]
    h = _rmsnorm_rows(x_ref[...], g_ref[...]).astype(BF16)
    p = jnp.dot(h, w_ref[...], preferred_element_type=F32)
    scale = HEAD_DIM ** -0.5 * LOG2E
    ones_col = _one_hot_row(HEAD_DIM)

    f = p[:, _O_F:_O_QD] + bf_ref[...]
    logf = jnp.minimum(f, 0.0) - jnp.log1p(jnp.exp(-jnp.abs(f)))
    logf = jnp.where(_lane_index(logf.shape) < C_HEADS, logf, 0.0)
    logf_s[...] = logf[:, :C_HEADS]
    c = _cumulative(logf, carry_s, c0_ref, pl.program_id(0) % tiles_per_seq == 0)

    kc, vc, kd, vd = p[:, _O_KC:_O_VC], p[:, _O_VC:_O_F], p[:, _O_KD:_O_VD], p[:, _O_VD:_O_END]
    kc_s[...], vc_s[...], kd_s[...], vd_s[...] = kc, vc, kd, vd
    _store_heads(qc_o, p[:, _O_QC:_O_KC], scale=scale,
                 extra=_one_hot_row(*range(HEAD_DIM, HEAD_DIM + FOX_BIAS_TERMS)))
    _store_fox_keys(kc_o, kc, c)
    _store_heads(vc_o, vc, extra=ones_col)
    _store_heads(qd_o, p[:, _O_QD:_O_KD], scale=scale)
    _store_heads(kd_o, kd)
    _store_heads(vd_o, vd)


def _odd_weights(w_in):
    cuts = np.cumsum(ODD_SPLIT)[:-1].tolist()
    qc, kc, vc, fc, qd, kd, vd = jnp.split(w_in, cuts, axis=1)
    w = jnp.concatenate([qc, kc, vc, _pad_cols(fc, LANES), qd, kd, vd], axis=1)
    assert w.shape[1] == _O_END
    return w.astype(BF16)


def _odd_proj(x, g, w, b_f, c0, *, seq_len, layer, n_layers, earlier=None, tm=256):
    m, d = x.shape
    tm = min(tm, seq_len)
    assert seq_len % tm == 0
    tiles_per_seq = seq_len // tm
    row = lambda cols: pl.BlockSpec((tm, cols), lambda i: (i, 0))
    const = lambda a: pl.BlockSpec(a.shape, lambda i: (0,) * a.ndim)
    g2 = g.reshape(1, d)
    bf2 = _pad_cols(b_f.reshape(1, C_HEADS), LANES)
    wide, flat = C_HEADS * LANES, C_HEADS * HEAD_DIM
    plane = pl.BlockSpec((None, tm, flat), lambda i: (layer, i, 0))
    earlier = () if earlier is None else tuple(earlier)
    n_in = 5
    return pl.pallas_call(
        functools.partial(_odd_proj_kernel, tiles_per_seq=tiles_per_seq),
        grid=(m // tm,),
        in_specs=[row(d), const(g2), const(w), const(bf2),
                  pl.BlockSpec((1, 1, LANES), lambda i: (i // tiles_per_seq, 0, 0))]
                 + [pl.BlockSpec(memory_space=pl.ANY)] * len(earlier),
        out_specs=[row(wide)] * 6 + [plane] * 4 + [row(C_HEADS)],
        out_shape=[jax.ShapeDtypeStruct((m, wide), BF16)] * 6
                  + [jax.ShapeDtypeStruct((n_layers, m, flat), F32)] * 4
                  + [jax.ShapeDtypeStruct((m, C_HEADS), F32)],
        input_output_aliases={n_in + k: 6 + k for k in range(len(earlier))},
        scratch_shapes=[pltpu.VMEM((8, LANES), F32)],
        compiler_params=_params("arbitrary"),
        name="odd_proj",
    )(x, g2, w, bf2, c0, *earlier)


def _fox_past_kernel(k_ref, logf_ref, c0_ref, k_o, cend_o, carry_s, *, tiles_per_seq):
    logf = jnp.pad(logf_ref[...], ((0, 0), (0, LANES - C_HEADS)))
    c = _cumulative(logf, carry_s, c0_ref, pl.program_id(0) % tiles_per_seq == 0)
    _store_fox_keys(k_o, k_ref[...], c)
    cend_o[0] = c[c.shape[0] - 1:, :]


def _fox_past(k, logf, *, seq_len, tm=256):
    m = k.shape[0]
    nseq = m // seq_len
    tm = min(tm, seq_len)
    tiles_per_seq = seq_len // tm
    row = lambda cols: pl.BlockSpec((tm, cols), lambda i: (i, 0))
    per_seq = pl.BlockSpec((1, 1, LANES), lambda i: (i // tiles_per_seq, 0, 0))
    return pl.pallas_call(
        functools.partial(_fox_past_kernel, tiles_per_seq=tiles_per_seq),
        grid=(m // tm,),
        in_specs=[row(C_HEADS * HEAD_DIM), row(C_HEADS), per_seq],
        out_specs=[row(C_HEADS * LANES), per_seq],
        out_shape=[jax.ShapeDtypeStruct((m, C_HEADS * LANES), BF16),
                   jax.ShapeDtypeStruct((nseq, 1, LANES), F32)],
        scratch_shapes=[pltpu.VMEM((8, LANES), F32)],
        compiler_params=_params("arbitrary"),
        name="fox_past",
    )(k, logf, jnp.zeros((nseq, 1, LANES), F32))


def _head_lanes(parts, ones_col=False):
    bsz, s_len, heads = parts[0].shape[:3]
    parts = [p.astype(BF16) for p in parts]
    used = sum(p.shape[-1] for p in parts)
    if ones_col:
        parts.append(jnp.ones((bsz, s_len, heads, 1), BF16))
        used += 1
    parts.append(jnp.zeros((bsz, s_len, heads, LANES - used), BF16))
    return jnp.concatenate(parts, axis=-1).reshape(bsz, s_len, heads * LANES)


def _keys(past, new, sk_pad):
    rows = new if past is None else jnp.concatenate([past, new], axis=1)
    pad = sk_pad - rows.shape[1]
    return jnp.pad(rows, ((0, 0), (0, pad), (0, 0))) if pad else rows


def _tiles(s_len, past_len):
    tq = min(256, s_len)
    tk, group = (256, 2) if past_len == 0 else (128, 3)
    return tq, tk, group, _round_up(past_len + s_len, tk * group)


def _even_mixer(x, h_gain, tables, past, weights, g_bq, g_bkv, w_o):
    bsz, s_len, d = x.shape
    m = bsz * s_len
    past_len = 0 if past is None else past[0].shape[1]
    sk = past_len + s_len
    tq, tk, group, sk_pad = _tiles(s_len, past_len)
    outs = _even_proj(x.reshape(m, d), h_gain, weights, g_bq, g_bkv, tables, seq_len=s_len)
    qa, ka, va, qi, ki, wi, qb, kb, vb, ka_s, va_s, ki_s, lat_s, kr_s = [
        o.reshape(bsz, s_len, -1) for o in outs]
    new_rows = (ka_s.reshape(bsz, s_len, A_KV_HEADS, HEAD_DIM), va_s.reshape(bsz, s_len, A_KV_HEADS, HEAD_DIM),
                ki_s, lat_s, kr_s)
    if past is None:
        pa = (None,) * 5
    else:
        c_k, c_v, c_ki, c_lat, c_kr = past
        kb_past, vb_past = _mla_past(c_lat.reshape(bsz * past_len, KV_RANK),
                                     c_kr.reshape(bsz * past_len, B_ROPE), weights[2])
        pa = (_head_lanes([c_k]), _head_lanes([c_v], ones_col=True), c_ki.astype(BF16),
              kb_past.reshape(bsz, past_len, -1), vb_past.reshape(bsz, past_len, -1))

    out_a = _dsa_attention(qa, _keys(pa[0], ka, sk_pad), _keys(pa[1], va, sk_pad), qi, _keys(pa[2], ki, sk_pad),
                           wi, q_off=past_len, sk=sk, tq=tq, tk=tk, group=group)
    out_b = _softmax_attention(qb, _keys(pa[3], kb, sk_pad), _keys(pa[4], vb, sk_pad),
                               heads=B_HEADS, mask_kind="chunk", q_off=past_len, sk=sk, tq=tq, tk=tk,
                               group=group, name="mla_attention")
    mix = jnp.concatenate([out_a, out_b], axis=-1).reshape(m, -1)
    y = _matmul_residual(mix, w_o, x.reshape(m, d)).reshape(bsz, s_len, d)
    return y, new_rows


def _odd_mixer(x, h_gain, past, w, b_f, w_o, *, layer, n_layers, earlier):
    bsz, s_len, d = x.shape
    m = bsz * s_len
    past_len = 0 if past is None else past[0].shape[1]
    sk = past_len + s_len
    tq, tk, group, sk_pad = _tiles(s_len, past_len)
    if past is None:
        pa = (None,) * 4
        c0 = jnp.zeros((bsz, 1, LANES), F32)
    else:
        c_k, c_v, c_logf, d_k, d_v = past
        kc_past, c0 = _fox_past(c_k.reshape(bsz * past_len, -1), c_logf.reshape(bsz * past_len, C_HEADS),
                                seq_len=past_len)
        pa = (kc_past.reshape(bsz, past_len, -1), _head_lanes([c_v], ones_col=True),
              _head_lanes([d_k]), _head_lanes([d_v]))
    outs = _odd_proj(x.reshape(m, d), h_gain, w, b_f, c0, seq_len=s_len, layer=layer, n_layers=n_layers,
                     earlier=earlier)
    qc, kc, vc, qd, kd, vd = [o.reshape(bsz, s_len, -1) for o in outs[:6]]
    state_planes, logf = outs[6:10], outs[10].reshape(bsz, s_len, C_HEADS)

    out_c = _softmax_attention(qc, _keys(pa[0], kc, sk_pad), _keys(pa[1], vc, sk_pad),
                               heads=C_HEADS, mask_kind="causal", q_off=past_len, sk=sk, tq=tq, tk=tk,
                               group=group, name="fox_attention")
    out_d = _sb_attention(qd, _keys(pa[2], kd, sk_pad), _keys(pa[3], vd, sk_pad),
                          q_off=past_len, sk=sk, tq=tq, tk=tk)
    mix = jnp.concatenate([out_c, out_d], axis=-1).reshape(m, -1)
    y = _matmul_residual(mix, w_o, x.reshape(m, d)).reshape(bsz, s_len, d)
    return y, state_planes, logf


def _final_norm_kernel(x_ref, g_ref, o_ref):
    x = x_ref[...]
    ms = jnp.mean(x * x, axis=-1, keepdims=True)
    o_ref[...] = x * lax.rsqrt(ms + EPS) * g_ref[...]


def _final_norm(x, g, *, tm=512):
    m, d = x.shape
    tm = min(tm, m)
    return pl.pallas_call(
        _final_norm_kernel,
        grid=(m // tm,),
        in_specs=[pl.BlockSpec((tm, d), lambda i: (i, 0)), pl.BlockSpec((1, d), lambda i: (0, 0))],
        out_specs=pl.BlockSpec((tm, d), lambda i: (i, 0)),
        out_shape=jax.ShapeDtypeStruct((m, d), F32),
        compiler_params=_params("parallel"),
        name="final_norm",
    )(x, g.reshape(1, d).astype(F32))


def _trunk(x, q_pos, caches, params, even_w, odd_w):
    (g_mix, g_ffn, g_final, w_in_even, g_b_q, g_b_kv, w_b_uq, w_b_ukv, w_o_even,
     w_in_odd, b_forget, w_o_odd, w_up, w_conv, b_conv, w_down) = params
    bsz, s_len, d = x.shape
    depth = g_mix.shape[0]
    even_rows = [[] for _ in range(5)]
    odd_planes, logf_rows = None, []
    n_odd = depth // 2
    conv_rows = []
    rows = max(s_len, min(512, bsz * s_len))
    tables = (_rope_tables(q_pos, rows, period=HEAD_DIM, start=0, n_rot=ROT_DIM),
              _rope_tables(q_pos, rows, period=IDX_DIM, start=0, n_rot=IDX_ROT),
              _rope_tables(q_pos, rows, period=LANES, start=_B_ROPE_LANE, n_rot=B_ROPE))
    for l in range(depth):
        j = l // 2
        if l % 2 == 0:
            past = None if caches is None else tuple(c[j] for c in caches[0:5])
            x, rows = _even_mixer(x, g_mix[l], tables, past, even_w[j], g_b_q[j], g_b_kv[j], w_o_even[j])
            for lst, r in zip(even_rows, rows):
                lst.append(r)
        else:
            past = None if caches is None else tuple(c[j] for c in caches[5:10])
            x, odd_planes, logf = _odd_mixer(x, g_mix[l], past, odd_w[j], b_forget[j], w_o_odd[j],
                                             layer=j, n_layers=n_odd, earlier=odd_planes)
            logf_rows.append(logf)
        state = jnp.zeros((bsz, CONV_W - 1, D_FF), F32) if caches is None else caches[10][l]
        y, new_buf = _conv_ffn(x.reshape(bsz * s_len, d), g_ffn[l], w_up[l], w_conv[l], b_conv[l], w_down[l],
                               state, seq_len=s_len, tm=1024)
        x = y.reshape(bsz, s_len, d)
        conv_rows.append(new_buf)
    out = _final_norm(x.reshape(bsz * s_len, d), g_final).reshape(bsz, s_len, d)
    k_c, v_c, k_d, v_d = [p.reshape(n_odd, bsz, s_len, -1, HEAD_DIM) for p in odd_planes]
    states = ([jnp.stack(r, axis=0) for r in even_rows]
              + [k_c, v_c, jnp.stack(logf_rows, axis=0), k_d, v_d, jnp.stack(conv_rows, axis=0)])
    return out, states


def kernel(x_prompt, x_sample, cache_a_k, cache_a_v, cache_a_idx_k, cache_b_latent, cache_b_rope,
           cache_c_k, cache_c_v, cache_c_logf, cache_d_k, cache_d_v, state_ffn_conv,
           g_mix, g_ffn, g_final, w_in_even, g_b_q, g_b_kv, w_b_uq, w_b_ukv, w_o_even,
           w_in_odd, b_forget, w_o_odd, w_up, w_conv, b_conv, w_down):
    params = (g_mix, g_ffn, g_final, w_in_even, g_b_q, g_b_kv, w_b_uq, w_b_ukv, w_o_even,
              w_in_odd, b_forget, w_o_odd, w_up, w_conv, b_conv, w_down)
    caches = (cache_a_k, cache_a_v, cache_a_idx_k, cache_b_latent, cache_b_rope,
              cache_c_k, cache_c_v, cache_c_logf, cache_d_k, cache_d_v, state_ffn_conv)
    past_len = cache_a_k.shape[2]
    pos_prompt = jnp.arange(x_prompt.shape[1], dtype=I32)
    pos_sample = past_len + jnp.arange(x_sample.shape[1], dtype=I32)
    even_w = [_even_weights(w_in_even[j], w_b_uq[j], w_b_ukv[j]) for j in range(w_in_even.shape[0])]
    odd_w = [_odd_weights(w_in_odd[j]) for j in range(w_in_odd.shape[0])]
    y_prompt, p_states = _trunk(x_prompt, pos_prompt, None, params, even_w, odd_w)
    y_sample, s_states = _trunk(x_sample, pos_sample, caches, params, even_w, odd_w)
    return (y_prompt, y_sample, *p_states, *s_states)
```

```python
import functools

import jax
import jax.numpy as jnp
import numpy as np
from jax import lax
from jax.experimental import pallas as pl
from jax.experimental.pallas import tpu as pltpu

F32 = jnp.float32
BF16 = jnp.bfloat16
I32 = jnp.int32

CHUNK = 64
ROPE_THETA = 500000.0
EPS = 1e-6
HEAD_DIM = 64
ROT_DIM = HEAD_DIM // 4
A_HEADS = 8
A_KV_HEADS = 2
IDX_HEADS = 8
IDX_DIM = 32
IDX_ROT = IDX_DIM // 4
TOPK_MAX = 256
B_HEADS = 8
Q_RANK = 256
KV_RANK = 128
B_NOPE = 64
B_ROPE = 32
B_VDIM = 64
C_HEADS = 8
D_HEADS = 8
D_FF = 2816
CONV_W = 3

EVEN_SPLIT = [A_HEADS * HEAD_DIM, A_KV_HEADS * HEAD_DIM, A_KV_HEADS * HEAD_DIM,
              IDX_HEADS * IDX_DIM, IDX_DIM, IDX_HEADS, Q_RANK, KV_RANK, B_ROPE]
ODD_SPLIT = [C_HEADS * HEAD_DIM] * 3 + [C_HEADS] + [D_HEADS * HEAD_DIM] * 3

LANES = 128
VMEM_LIMIT_BYTES = 56 * 1024 * 1024
MASKED = -1e30
INT_MIN = -2 ** 31
LOG2E = 1.4426950408889634
F32_EXP2_UNDERFLOW = -152.0

_NT = (((1,), (1,)), ((), ()))


def _params(*sem):
    return pltpu.CompilerParams(dimension_semantics=sem, vmem_limit_bytes=VMEM_LIMIT_BYTES)


def _round_up(n, m):
    return (n + m - 1) // m * m


def _chunk_of(pos):
    return jnp.right_shift(pos, CHUNK.bit_length() - 1)


def _proj_kernel(x_ref, g_ref, w_ref, *out_refs, norm, emit_h):
    x = x_ref[...]
    if norm:
        ms = jnp.mean(x * x, axis=-1, keepdims=True)
        x = x * lax.rsqrt(ms + EPS) * g_ref[...]
    out_refs[0][...] = jnp.dot(x.astype(BF16), w_ref[...], preferred_element_type=F32)
    if emit_h:
        out_refs[1][...] = x


def _proj(x, g, w, *, norm=True, emit_h=False, tm=512):
    m, k = x.shape
    n = w.shape[1]
    tm = min(tm, m)
    assert m % tm == 0
    out_shape = [jax.ShapeDtypeStruct((m, n), F32)]
    out_specs = [pl.BlockSpec((tm, n), lambda i: (i, 0))]
    if emit_h:
        out_shape.append(jax.ShapeDtypeStruct((m, k), F32))
        out_specs.append(pl.BlockSpec((tm, k), lambda i: (i, 0)))
    res = pl.pallas_call(
        functools.partial(_proj_kernel, norm=norm, emit_h=emit_h),
        grid=(m // tm,),
        in_specs=[pl.BlockSpec((tm, k), lambda i: (i, 0)),
                  pl.BlockSpec((1, k), lambda i: (0, 0)),
                  pl.BlockSpec((k, n), lambda i: (0, 0))],
        out_specs=out_specs,
        out_shape=out_shape,
        compiler_params=_params("parallel"),
        name="proj",
    )(x, g.reshape(1, k).astype(F32), w.astype(BF16))
    return res if emit_h else res[0]


def _matmul_residual_kernel(a_ref, w_ref, r_ref, o_ref):
    o_ref[...] = r_ref[...] + jnp.dot(a_ref[...], w_ref[...], preferred_element_type=F32)


def _matmul_residual(a, w, res, *, tm=512):
    m, k = a.shape
    n = w.shape[1]
    tm = min(tm, m)
    assert m % tm == 0
    return pl.pallas_call(
        _matmul_residual_kernel,
        grid=(m // tm,),
        in_specs=[pl.BlockSpec((tm, k), lambda i: (i, 0)),
                  pl.BlockSpec((k, n), lambda i: (0, 0)),
                  pl.BlockSpec((tm, n), lambda i: (i, 0))],
        out_specs=pl.BlockSpec((tm, n), lambda i: (i, 0)),
        out_shape=jax.ShapeDtypeStruct((m, n), F32),
        compiler_params=_params("parallel"),
        name="matmul_residual",
    )(a, w.astype(BF16), res)


def _conv_ffn_kernel(x_ref, g_ref, wg_ref, wu_ref, wc_ref, bc_ref, wd_ref, st_ref,
                     y_ref, ns_ref, h_s, acc_s, carry_s, *, tiles_per_seq, seqs, tm, tf):
    i = pl.program_id(0)
    c = pl.program_id(1)
    nc = pl.num_programs(1)
    cols = pl.ds(pl.multiple_of(c * tf, tf), tf)
    rows_per = tm // seqs

    @pl.when(c == 0)
    def _():
        x = x_ref[...]
        ms = jnp.mean(x * x, axis=-1, keepdims=True)
        h_s[...] = (x * lax.rsqrt(ms + EPS) * g_ref[...]).astype(BF16)
        acc_s[...] = jnp.zeros_like(acc_s)

    h = h_s[...]
    gate = jnp.dot(h, wg_ref[...], preferred_element_type=F32)
    up = jnp.dot(h, wu_ref[...], preferred_element_type=F32)

    if seqs == 1:
        @pl.when(i % tiles_per_seq == 0)
        def _():
            carry_s[c] = st_ref[0, :, cols]
        before = [carry_s[c]]
    else:
        before = [st_ref[s, :, cols] for s in range(seqs)]
    row = lax.broadcasted_iota(I32, gate.shape, 0)
    g1 = pltpu.roll(gate, 1, 0)
    g2 = pltpu.roll(gate, 2, 0)
    for s, prev in enumerate(before):
        first = s * rows_per
        g1 = jnp.where(row == first, prev[1:2, :], g1)
        g2 = jnp.where(row == first, prev[0:1, :], jnp.where(row == first + 1, prev[1:2, :], g2))
    wc = wc_ref[...]
    gc = bc_ref[...] + g2 * wc[0:1, :]
    gc = gc + g1 * wc[1:2, :]
    gc = gc + gate * wc[2:3, :]
    act = gc * jax.nn.sigmoid(gc) * up
    acc_s[...] += jnp.dot(act.astype(BF16), wd_ref[...], preferred_element_type=F32)

    for s in range(seqs):
        ns_ref[s, :, cols] = gate[(s + 1) * rows_per - 2:(s + 1) * rows_per, :]
    if seqs == 1:
        carry_s[c] = gate[tm - 2:tm, :]

    @pl.when(c == nc - 1)
    def _():
        y_ref[...] = x_ref[...] + acc_s[...]


def _conv_ffn(x, g, w_up, w_conv, b_conv, w_down, state, *, seq_len, tm, tf=256):
    m, d = x.shape
    nseq = m // seq_len
    tm = min(tm, m)
    assert m % tm == 0 and D_FF % tf == 0 and (seq_len % tm == 0 or tm % seq_len == 0)
    tiles_per_seq = max(1, seq_len // tm)
    seqs = max(1, tm // seq_len)
    nc = D_FF // tf
    wg = w_up[:, :D_FF].astype(BF16)
    wu = w_up[:, D_FF:].astype(BF16)
    state_spec = pl.BlockSpec((seqs, CONV_W - 1, D_FF), lambda i, c: (i // tiles_per_seq, 0, 0))
    y, ns = pl.pallas_call(
        functools.partial(_conv_ffn_kernel, tiles_per_seq=tiles_per_seq, seqs=seqs, tm=tm, tf=tf),
        grid=(m // tm, nc),
        in_specs=[pl.BlockSpec((tm, d), lambda i, c: (i, 0)),
                  pl.BlockSpec((1, d), lambda i, c: (0, 0)),
                  pl.BlockSpec((d, tf), lambda i, c: (0, c)),
                  pl.BlockSpec((d, tf), lambda i, c: (0, c)),
                  pl.BlockSpec((CONV_W, tf), lambda i, c: (0, c)),
                  pl.BlockSpec((1, tf), lambda i, c: (0, c)),
                  pl.BlockSpec((tf, d), lambda i, c: (c, 0)),
                  state_spec],
        out_specs=[pl.BlockSpec((tm, d), lambda i, c: (i, 0)), state_spec],
        out_shape=[jax.ShapeDtypeStruct((m, d), F32),
                   jax.ShapeDtypeStruct((nseq, CONV_W - 1, D_FF), F32)],
        scratch_shapes=[pltpu.VMEM((tm, d), BF16),
                        pltpu.VMEM((tm, d), F32),
                        pltpu.VMEM((nc, CONV_W - 1, tf), F32)],
        compiler_params=_params("arbitrary", "arbitrary"),
        name="conv_ffn",
    )(x, g.reshape(1, d).astype(F32), wg, wu, w_conv.astype(F32), b_conv.reshape(1, D_FF).astype(F32),
      w_down.astype(BF16), state.astype(F32))
    return y, ns


def _attn_call(kernel, inputs, in_specs, out_cols, *, bsz, sq, tq, scratch=(), name):
    return pl.pallas_call(
        kernel,
        grid=(bsz, sq // tq),
        in_specs=in_specs,
        out_specs=pl.BlockSpec((1, tq, out_cols), lambda b, i: (b, i, 0)),
        out_shape=jax.ShapeDtypeStruct((bsz, sq, out_cols), BF16),
        scratch_shapes=list(scratch),
        compiler_params=_params("parallel", "arbitrary"),
        name=name,
    )(*inputs)


def _q_spec(tq, cols):
    return pl.BlockSpec((1, tq, cols), lambda b, i: (b, i, 0))


def _k_spec(rows, cols):
    return pl.BlockSpec((1, rows, cols), lambda b, i: (b, 0, 0))


def _tile_geometry(i, *, tq, span, q_off):
    q_lo = q_off + i * tq
    return q_lo, q_lo // span


def _lane_tiles(x):
    return [x[:, j * LANES:(j + 1) * LANES] for j in range(x.shape[1] // LANES)]


def _visible(kind, q_lo, ks, *, tq, tk, sk):
    q_pos = q_lo + lax.broadcasted_iota(I32, (tq, 1), 0)
    k_pos = ks + lax.broadcasted_iota(I32, (1, tk), 1)
    if kind == "chunk":
        return (_chunk_of(k_pos) <= _chunk_of(q_pos)) & (k_pos < sk)
    if kind == "causal":
        return k_pos <= q_pos
    assert kind == "strict"
    return k_pos < q_pos


HEADS_PER_STEP = 2


def _span_cases(i, body, *, tq, span, q_off, n_tiles):
    g_own = (q_off + i * tq) // span
    for g in sorted({(q_off + t * tq) // span for t in range(n_tiles)}):
        pl.when(g_own == g)(functools.partial(body, g + 1))


def _softmax_heads(qs, k_ref, v_ref, s_scr, kcols, vcols, *, tq, tk, blocks, own_from, own_mask,
                   bias_ref=None):
    row_max = []
    for slot, (q, kcol) in enumerate(zip(qs, kcols)):
        mx = jnp.full((tq, LANES), MASKED, F32)
        for b in range(blocks):
            ks = b * tk
            s = lax.dot_general(q, k_ref[0, ks:ks + tk, kcol], _NT, preferred_element_type=F32)
            if bias_ref is not None:
                s = s + bias_ref[:, ks:ks + tk]
            if own_mask is not None and b >= own_from:
                s = jnp.where(own_mask(ks), s, MASKED)
            s_scr[slot, :, ks:ks + tk] = s
            for t in _lane_tiles(s):
                mx = jnp.maximum(mx, t)
        row_max.append(jnp.broadcast_to(jnp.max(mx, axis=1, keepdims=True), (tq, LANES)))
    outs = []
    for slot, (m, vcol) in enumerate(zip(row_max, vcols)):
        acc = jnp.zeros((tq, LANES), F32)
        for b in range(blocks):
            ks = b * tk
            p = jnp.concatenate([jnp.exp2(t - m) for t in _lane_tiles(s_scr[slot, :, ks:ks + tk])], axis=1)
            acc = acc + jnp.dot(p.astype(BF16), v_ref[0, ks:ks + tk, vcol], preferred_element_type=F32)
        outs.append(acc[:, :HEAD_DIM] / acc[:, HEAD_DIM:HEAD_DIM + 1])
    return outs


def _lane_tile_at(index):
    return pl.ds(pl.multiple_of(index * LANES, LANES), LANES)


def _softmax_attn_kernel(q_ref, k_ref, v_ref, o_ref, s_scr, *, heads, mask_kind, tq, tk, group, q_off, sk,
                         n_tiles):
    i = pl.program_id(1)
    own_mask = functools.partial(_visible, mask_kind, q_off + i * tq, tq=tq, tk=tk, sk=sk)

    def attend(n_spans):
        def step(hp, _):
            cols = [_lane_tile_at(hp * HEADS_PER_STEP + u) for u in range(HEADS_PER_STEP)]
            outs = _softmax_heads([q_ref[0, :, c] for c in cols], k_ref, v_ref, s_scr, cols, cols,
                                  tq=tq, tk=tk, blocks=n_spans * group, own_from=(n_spans - 1) * group,
                                  own_mask=own_mask)
            o_ref[0, :, _lane_tile_at(hp)] = jnp.concatenate(outs, axis=1).astype(o_ref.dtype)
            return 0
        lax.fori_loop(0, heads // HEADS_PER_STEP, step, 0)

    _span_cases(i, attend, tq=tq, span=tk * group, q_off=q_off, n_tiles=n_tiles)


def _check_tiling(sk_pad, *, tq, tk, group, q_off):
    assert (tk * group) % tq == 0 and q_off % tq == 0 and sk_pad % (tk * group) == 0 and tk % LANES == 0
    assert HEADS_PER_STEP * HEAD_DIM == LANES


def _softmax_attention(q, k, v, *, heads, mask_kind, q_off, sk, tq, tk, group, name):
    bsz, sq, _ = q.shape
    sk_pad = k.shape[1]
    _check_tiling(sk_pad, tq=tq, tk=tk, group=group, q_off=q_off)
    kern = functools.partial(_softmax_attn_kernel, heads=heads, mask_kind=mask_kind,
                             tq=tq, tk=tk, group=group, q_off=q_off, sk=sk, n_tiles=sq // tq)
    cols = heads * LANES
    return _attn_call(kern, (q, k, v), [_q_spec(tq, cols), _k_spec(sk_pad, cols), _k_spec(sk_pad, cols)],
                      heads * HEAD_DIM, bsz=bsz, sq=sq, tq=tq,
                      scratch=(pltpu.VMEM((HEADS_PER_STEP, tq, sk_pad), F32),), name=name)


def _split_bf16(x):
    hi = x.astype(BF16)
    lo = (x - hi.astype(F32)).astype(BF16)
    return hi, lo


SB_HEADS_PER_LOOP = 2


def _sb_kernel(q_ref, k_ref, v_ref, o_ref, *, tq, tk, q_off, sk):
    q_lo, n_before = _tile_geometry(pl.program_id(1), tq=tq, span=tk, q_off=q_off)
    r = lax.broadcasted_iota(I32, (tk, tk), 0)
    cidx = lax.broadcasted_iota(I32, (tk, tk), 1)
    later = jnp.where(r > cidx, 1.0, 0.0).astype(BF16)

    def block(q, h, kb, run, acc, mask):
        ks = pl.multiple_of(kb * tk, tk)
        k = k_ref[0, pl.ds(ks, tk), h * LANES:(h + 1) * LANES]
        v = v_ref[0, pl.ds(ks, tk), h * LANES:(h + 1) * LANES]
        z = lax.dot_general(q, k, _NT, preferred_element_type=F32)
        log_beta = jnp.minimum(z, 0.0) - jnp.log2(1.0 + jnp.exp2(-jnp.abs(z)))
        log_1m = log_beta - z
        if mask is not None:
            log_1m = jnp.where(mask, log_1m, 0.0)
        hi, lo = _split_bf16(log_1m)
        after = (jnp.dot(hi, later, preferred_element_type=F32)
                 + jnp.dot(lo, later, preferred_element_type=F32))
        a = jnp.exp2(log_beta + after + run)
        if mask is not None:
            a = jnp.where(mask, a, 0.0)
        acc = acc + jnp.dot(a.astype(BF16), v, preferred_element_type=F32)
        run = run + after[:, 0:1] + log_1m[:, 0:1]
        return run, acc

    own = _visible("strict", q_lo, n_before * tk, tq=tq, tk=tk, sk=sk)
    for h0 in range(0, D_HEADS, SB_HEADS_PER_LOOP):
        hs = range(h0, h0 + SB_HEADS_PER_LOOP)
        qs = [q_ref[0, :, h * LANES:(h + 1) * LANES] for h in hs]
        state = []
        for h, q in zip(hs, qs):
            state.extend(block(q, h, n_before, jnp.zeros((tq, 1), F32), jnp.zeros((tq, LANES), F32), own))

        def alive(state):
            top = functools.reduce(jnp.maximum, state[0::2])
            return jnp.max(top) > F32_EXP2_UNDERFLOW

        def cond(c):
            return (c[0] < n_before) & c[1]

        def body(c, hs=hs, qs=qs):
            j, _, state = c
            new = []
            for n, (h, q) in enumerate(zip(hs, qs)):
                new.extend(block(q, h, n_before - 1 - j, state[2 * n], state[2 * n + 1], None))
            return j + 1, alive(new), tuple(new)

        _, _, state = lax.while_loop(cond, body, (jnp.int32(0), alive(state), tuple(state)))
        for n, h in enumerate(hs):
            o_ref[0, :, h * HEAD_DIM:(h + 1) * HEAD_DIM] = state[2 * n + 1][:, :HEAD_DIM].astype(o_ref.dtype)


def _sb_attention(q, k, v, *, q_off, sk, tq, tk):
    bsz, sq, _ = q.shape
    sk_pad = k.shape[1]
    _check_tiling(sk_pad, tq=tq, tk=tk, group=1, q_off=q_off)
    kern = functools.partial(_sb_kernel, tq=tq, tk=tk, q_off=q_off, sk=sk)
    cols = D_HEADS * LANES
    return _attn_call(kern, (q, k, v), [_q_spec(tq, cols), _k_spec(sk_pad, cols), _k_spec(sk_pad, cols)],
                      D_HEADS * HEAD_DIM, bsz=bsz, sq=sq, tq=tq, name="sb_attention")


def _sortable_key(score):
    bits = lax.bitcast_convert_type(score, I32)
    return jnp.where(bits < 0, bits ^ 0x7FFFFFFF, bits)


KEY_OF_NEG_INF = -0x7F800001


def _dsa_kernel(q_ref, k_ref, v_ref, qi_ref, ki_ref, wi_ref, o_ref, key_s, bias_s, s_scr, w_s,
                *, tq, tk, group, q_off, sk, topk, n_tiles):
    i = pl.program_id(1)
    nt = tk // LANES
    own_visible = functools.partial(_visible, "chunk", q_off + i * tq, tq=tq, tk=tk, sk=sk)
    kf = float(topk)

    for h in range(IDX_HEADS):
        w_s[h] = jnp.broadcast_to(wi_ref[0, :, h:h + 1], (tq, LANES))

    def select_and_attend(n_spans):
        blocks, own_from = n_spans * group, (n_spans - 1) * group
        starts = [b * tk for b in range(blocks)]

        def own(ks, value, masked):
            return jnp.where(own_visible(ks), value, masked) if ks >= own_from * tk else value

        qis = [qi_ref[0, :, h * IDX_DIM:(h + 1) * IDX_DIM] for h in range(IDX_HEADS)]
        for ks in starts:
            ki = ki_ref[0, ks:ks + tk, :]
            total = jnp.zeros((tq, tk), F32)
            for h in range(IDX_HEADS):
                sc = lax.dot_general(qis[h], ki, _NT, preferred_element_type=F32)
                total = total + jnp.maximum(sc, 0.0) * jnp.concatenate([w_s[h]] * nt, axis=1)
            key_s[:, ks:ks + tk] = _sortable_key(own(ks, total, -jnp.inf))

        def count(hit):
            cnt = jnp.zeros((tq, LANES), F32)
            for ks in starts:
                for t in _lane_tiles(key_s[:, ks:ks + tk]):
                    cnt = cnt + jnp.where(hit(t), 1.0, 0.0)
            return jnp.sum(cnt, axis=1, keepdims=True)

        def count_ge(cand):
            cand = jnp.broadcast_to(cand, (tq, LANES))
            return count(lambda t: t >= cand)

        lowest = jnp.full((tq, 1), INT_MIN, I32)
        zero = jnp.zeros((tq, 1), I32)
        n_zero = count_ge(zero)
        ok = n_zero >= kf
        thr = jnp.where(ok, zero, lowest)
        n_thr = jnp.where(ok, n_zero, float(blocks * tk))

        def bit_step(it, c):
            thr, n_thr = c
            cand = thr | jnp.left_shift(jnp.int32(1), 30 - it)
            n_cand = count_ge(cand)
            ok = n_cand >= kf
            return jnp.where(ok, cand, thr), jnp.where(ok, n_cand, n_thr)

        thr, n_thr = lax.fori_loop(0, 31, bit_step, (thr, n_thr))
        thr_b = jnp.broadcast_to(thr, (tq, LANES))

        tied = jnp.max(jnp.where(thr > KEY_OF_NEG_INF, n_thr, 0.0)) > kf

        @pl.when(jnp.logical_not(tied))
        def _():
            for ks in starts:
                tiles = _lane_tiles(key_s[:, ks:ks + tk])
                bias = jnp.concatenate([jnp.where(t >= thr_b, 0.0, MASKED) for t in tiles], axis=1)
                bias_s[:, ks:ks + tk] = own(ks, bias, MASKED)

        @pl.when(tied)
        def _():
            r = lax.broadcasted_iota(I32, (tk, tk), 0)
            cidx = lax.broadcasted_iota(I32, (tk, tk), 1)
            upto = jnp.where(r <= cidx, 1.0, 0.0).astype(BF16)
            need = kf - count(lambda t: t > thr_b)
            seen = jnp.zeros((tq, 1), F32)
            for ks in starts:
                key = key_s[:, ks:ks + tk]
                tie = jnp.where(key == thr, 1.0, 0.0)
                rank = seen + jnp.dot(tie.astype(BF16), upto, preferred_element_type=F32)
                sel = (key > thr) | ((key == thr) & (rank <= need))
                bias_s[:, ks:ks + tk] = own(ks, jnp.where(sel, 0.0, MASKED), MASKED)
                seen = seen + jnp.sum(tie, axis=1, keepdims=True)

        def step(hp, _):
            heads = [hp * HEADS_PER_STEP + u for u in range(HEADS_PER_STEP)]
            kv = [_lane_tile_at(hd // (A_HEADS // A_KV_HEADS)) for hd in heads]
            outs = _softmax_heads([q_ref[0, :, _lane_tile_at(hd)] for hd in heads], k_ref, v_ref, s_scr,
                                  kv, kv, tq=tq, tk=tk, blocks=blocks, own_from=own_from, own_mask=None,
                                  bias_ref=bias_s)
            o_ref[0, :, _lane_tile_at(hp)] = jnp.concatenate(outs, axis=1).astype(o_ref.dtype)
            return 0
        lax.fori_loop(0, A_HEADS // HEADS_PER_STEP, step, 0)

    _span_cases(i, select_and_attend, tq=tq, span=tk * group, q_off=q_off, n_tiles=n_tiles)


def _dsa_attention(q, k, v, qi, ki, wi, *, q_off, sk, tq, tk, group):
    bsz, sq, _ = q.shape
    sk_pad = k.shape[1]
    _check_tiling(sk_pad, tq=tq, tk=tk, group=group, q_off=q_off)
    topk = min(TOPK_MAX, sk // 4)
    kern = functools.partial(_dsa_kernel, tq=tq, tk=tk, group=group, q_off=q_off, sk=sk, topk=topk,
                             n_tiles=sq // tq)
    kv_cols = A_KV_HEADS * LANES
    return _attn_call(kern, (q, k, v, qi, ki, wi),
                      [_q_spec(tq, A_HEADS * LANES), _k_spec(sk_pad, kv_cols), _k_spec(sk_pad, kv_cols),
                       _q_spec(tq, IDX_HEADS * IDX_DIM), _k_spec(sk_pad, IDX_DIM), _q_spec(tq, IDX_HEADS)],
                      A_HEADS * HEAD_DIM, bsz=bsz, sq=sq, tq=tq,
                      scratch=(pltpu.VMEM((tq, sk_pad), I32), pltpu.VMEM((tq, sk_pad), F32),
                               pltpu.VMEM((HEADS_PER_STEP, tq, sk_pad), F32),
                               pltpu.VMEM((IDX_HEADS, tq, LANES), F32)),
                      name="dsa_attention")


def _rmsnorm_rows(x, g):
    ms = jnp.mean(x * x, axis=-1, keepdims=True)
    return x * lax.rsqrt(ms + EPS) * g


def _lane_index(shape):
    return lax.broadcasted_iota(I32, shape, len(shape) - 1)


def _one_hot_row(*lanes):
    lane = _lane_index((1, LANES))
    hit = functools.reduce(jnp.logical_or, [lane == l for l in lanes])
    return jnp.where(hit, 1.0, 0.0)


def _rotate(tile, tables, half):
    cos, sin_up, sin_dn = tables
    return tile * cos + pltpu.roll(tile, half, 1) * sin_up + pltpu.roll(tile, LANES - half, 1) * sin_dn


def _split_pair(pair):
    low = _lane_index(pair.shape) < HEAD_DIM
    return jnp.where(low, pair, 0.0), jnp.where(low, pltpu.roll(pair, HEAD_DIM, 1), 0.0)


def _store_heads(out_ref, compact, *, scale=None, extra=None):
    for p, pair in enumerate(_lane_tiles(compact)):
        for u, tile in enumerate(_split_pair(pair)):
            if scale is not None:
                tile = tile * scale
            if extra is not None:
                tile = tile + extra
            h = 2 * p + u
            out_ref[:, h * LANES:(h + 1) * LANES] = tile.astype(out_ref.dtype)


_E_QA, _E_KA, _E_VA, _E_QI, _E_KI, _E_WI, _E_CQ, _E_CKV, _E_KR, _E_END = (
    0, 512, 640, 768, 1024, 1152, 1280, 1536, 1664, 1792)
_B_ROPE_LANE = B_NOPE


def _even_proj_kernel(x_ref, g_ref, w_ref, gq_ref, wuq_ref, gkv_ref, wukv_ref, ta_ref, ti_ref, tb_ref,
                      qa_o, ka_o, va_o, qi_o, ki_o, wi_o, qb_o, kb_o, vb_o,
                      ka_s, va_s, ki_s, lat_s, kr_s):
    h = _rmsnorm_rows(x_ref[...], g_ref[...]).astype(BF16)
    p = jnp.dot(h, w_ref[...], preferred_element_type=F32)
    ta = (ta_ref[0], ta_ref[1], ta_ref[2])
    ti = (ti_ref[0], ti_ref[1], ti_ref[2])
    tb = (tb_ref[0], tb_ref[1], tb_ref[2])
    ones_col = _one_hot_row(HEAD_DIM)

    qa = jnp.concatenate([_rotate(t, ta, ROT_DIM // 2) for t in _lane_tiles(p[:, _E_QA:_E_KA])], axis=1)
    _store_heads(qa_o, qa, scale=HEAD_DIM ** -0.5 * LOG2E)
    ka = _rotate(p[:, _E_KA:_E_VA], ta, ROT_DIM // 2)
    ka_s[...] = ka
    _store_heads(ka_o, ka)
    va = p[:, _E_VA:_E_QI]
    va_s[...] = va
    _store_heads(va_o, va, extra=ones_col)
    qi = jnp.concatenate([_rotate(t, ti, IDX_ROT // 2) for t in _lane_tiles(p[:, _E_QI:_E_KI])], axis=1)
    qi_o[...] = (qi * IDX_DIM ** -0.5).astype(qi_o.dtype)
    ki = _rotate(p[:, _E_KI:_E_WI], ti, IDX_ROT // 2)[:, :IDX_DIM]
    ki_s[...] = ki
    ki_o[...] = ki.astype(ki_o.dtype)
    wi_o[...] = p[:, _E_WI:_E_WI + IDX_HEADS] * IDX_HEADS ** -0.5

    cq = _rmsnorm_rows(p[:, _E_CQ:_E_CKV], gq_ref[...]).astype(BF16)
    qb = jnp.dot(cq, wuq_ref[...], preferred_element_type=F32)
    scale_b = (B_NOPE + B_ROPE) ** -0.5 * LOG2E
    for hd, t in enumerate(_lane_tiles(qb)):
        qb_o[:, hd * LANES:(hd + 1) * LANES] = (_rotate(t, tb, B_ROPE // 2) * scale_b).astype(qb_o.dtype)
    lat = _rmsnorm_rows(p[:, _E_CKV:_E_KR], gkv_ref[...])
    lat_s[...] = lat
    kv = jnp.dot(lat.astype(BF16), wukv_ref[...], preferred_element_type=F32)
    kr = _rotate(p[:, _E_KR:_E_END], tb, B_ROPE // 2)
    kr_s[...] = kr[:, _B_ROPE_LANE:_B_ROPE_LANE + B_ROPE]
    _store_mla_keys(kb_o, vb_o, kv, kr)


def _store_mla_keys(kb_o, vb_o, kv, kr):
    ones_col = _one_hot_row(HEAD_DIM)
    tiles = _lane_tiles(kv)
    for hd in range(B_HEADS):
        kb_o[:, hd * LANES:(hd + 1) * LANES] = (tiles[hd] + kr).astype(kb_o.dtype)
        vb_o[:, hd * LANES:(hd + 1) * LANES] = (tiles[B_HEADS + hd] + ones_col).astype(vb_o.dtype)


def _mla_past_kernel(lat_ref, kr_ref, wukv_ref, kb_o, vb_o):
    kv = jnp.dot(lat_ref[...].astype(BF16), wukv_ref[...], preferred_element_type=F32)
    rows = kv.shape[0]
    kr = jnp.concatenate([jnp.zeros((rows, _B_ROPE_LANE), F32), kr_ref[...],
                          jnp.zeros((rows, LANES - _B_ROPE_LANE - B_ROPE), F32)], axis=1)
    _store_mla_keys(kb_o, vb_o, kv, kr)


def _mla_past(lat, kr, wukv, *, tm=512):
    m = lat.shape[0]
    tm = min(tm, m)
    assert m % tm == 0
    row = lambda cols: pl.BlockSpec((tm, cols), lambda i: (i, 0))
    wide = B_HEADS * LANES
    return pl.pallas_call(
        _mla_past_kernel,
        grid=(m // tm,),
        in_specs=[row(KV_RANK), row(B_ROPE), pl.BlockSpec(wukv.shape, lambda i: (0, 0))],
        out_specs=[row(wide), row(wide)],
        out_shape=[jax.ShapeDtypeStruct((m, wide), BF16)] * 2,
        compiler_params=_params("parallel"),
        name="mla_past",
    )(lat, kr, wukv)


def _pad_cols(w, width):
    return jnp.pad(w, ((0, 0), (0, width - w.shape[1])))


def _even_weights(w_in, w_uq, w_ukv):
    cuts = np.cumsum(EVEN_SPLIT)[:-1].tolist()
    qa, ka, va, qi, ki, wi, cq, ckv, kr = jnp.split(w_in, cuts, axis=1)
    kr = jnp.pad(kr, ((0, 0), (_B_ROPE_LANE, LANES - _B_ROPE_LANE - B_ROPE)))
    w = jnp.concatenate([qa, ka, va, qi, _pad_cols(ki, LANES), _pad_cols(wi, LANES), cq, ckv, kr], axis=1)
    d = w_in.shape[0]
    wuq = jnp.pad(w_uq.reshape(Q_RANK, B_HEADS, B_NOPE + B_ROPE),
                  ((0, 0), (0, 0), (0, LANES - B_NOPE - B_ROPE))).reshape(Q_RANK, B_HEADS * LANES)
    kvw = w_ukv.reshape(KV_RANK, B_HEADS, B_NOPE + B_VDIM)
    pad = lambda a: jnp.pad(a, ((0, 0), (0, 0), (0, LANES - a.shape[2]))).reshape(KV_RANK, B_HEADS * LANES)
    wukv = jnp.concatenate([pad(kvw[:, :, :B_NOPE]), pad(kvw[:, :, B_NOPE:])], axis=1)
    assert w.shape == (d, _E_END)
    return w.astype(BF16), wuq.astype(BF16), wukv.astype(BF16)


def _rope_tables(pos, rows, *, period, start, n_rot):
    half = n_rot // 2
    inv = ROPE_THETA ** (-jnp.arange(half, dtype=F32) * 2.0 / n_rot)
    ang = pos.astype(F32)[:, None] * inv[None, :]
    cos, sin = jnp.cos(ang), jnp.sin(ang)
    off = np.arange(LANES) % period - start
    idx = np.where((off >= 0) & (off < n_rot), off % half, 0)
    lower = (off >= 0) & (off < half)
    upper = (off >= half) & (off < n_rot)
    cos_t = jnp.where(lower | upper, cos[:, idx], 1.0)
    sin_up = jnp.where(upper, sin[:, idx], 0.0)
    sin_dn = jnp.where(lower, -sin[:, idx], 0.0)
    tabs = jnp.stack([cos_t, sin_up, sin_dn])
    return jnp.tile(tabs, (1, rows // pos.shape[0], 1))


def _even_proj(x, g, weights, g_bq, g_bkv, tables, *, seq_len, tm=512):
    m, d = x.shape
    w, wuq, wukv = weights
    tm = min(tm, m)
    table_rows = tables[0].shape[1]
    assert m % tm == 0 and table_rows % tm == 0 and (seq_len % tm == 0 or tm % seq_len == 0)
    tblocks = table_rows // tm
    row = lambda cols: pl.BlockSpec((tm, cols), lambda i: (i, 0))
    const = lambda a: pl.BlockSpec(a.shape, lambda i: (0,) * a.ndim)
    tspec = pl.BlockSpec((3, tm, LANES), lambda i: (0, i % tblocks, 0))
    g2, gq2, gkv2 = g.reshape(1, d), g_bq.reshape(1, Q_RANK), g_bkv.reshape(1, KV_RANK)
    kv_cols = A_KV_HEADS * HEAD_DIM
    outs = [(A_HEADS * LANES, BF16), (A_KV_HEADS * LANES, BF16), (A_KV_HEADS * LANES, BF16),
            (IDX_HEADS * IDX_DIM, BF16), (IDX_DIM, BF16), (IDX_HEADS, F32),
            (B_HEADS * LANES, BF16), (B_HEADS * LANES, BF16), (B_HEADS * LANES, BF16),
            (kv_cols, F32), (kv_cols, F32), (IDX_DIM, F32), (KV_RANK, F32), (B_ROPE, F32)]
    return pl.pallas_call(
        _even_proj_kernel,
        grid=(m // tm,),
        in_specs=[row(d), const(g2), const(w), const(gq2), const(wuq), const(gkv2), const(wukv),
                  tspec, tspec, tspec],
        out_specs=[row(c) for c, _ in outs],
        out_shape=[jax.ShapeDtypeStruct((m, c), t) for c, t in outs],
        compiler_params=_params("parallel"),
        name="even_proj",
    )(x, g2, w, gq2, wuq, gkv2, wukv, *tables)


_O_QC, _O_KC, _O_VC, _O_F, _O_QD, _O_KD, _O_VD, _O_END = 0, 512, 1024, 1536, 1664, 2176, 2688, 3200
FOX_BIAS_TERMS = 3


def _three_terms(x):
    hi = x.astype(BF16)
    r = x - hi.astype(F32)
    mid = r.astype(BF16)
    return hi, mid, (r - mid.astype(F32)).astype(BF16)


def _cumulative(logf, carry_s, c0_ref, restart):
    tm = logf.shape[0]

    @pl.when(restart)
    def _():
        carry_s[...] = jnp.broadcast_to(c0_ref[0], carry_s.shape)

    r = lax.broadcasted_iota(I32, (tm, tm), 0)
    cidx = lax.broadcasted_iota(I32, (tm, tm), 1)
    upto = jnp.where(cidx <= r, 1.0, 0.0).astype(BF16)
    c = carry_s[0:1, :] + sum(jnp.dot(upto, t, preferred_element_type=F32) for t in _three_terms(logf))
    carry_s[...] = jnp.broadcast_to(c[tm - 1:tm, :], carry_s.shape)
    return c


def _fox_key_bias(c):
    r = lax.broadcasted_iota(I32, (LANES, C_HEADS * LANES), 0)
    cidx = lax.broadcasted_iota(I32, (LANES, C_HEADS * LANES), 1)
    placed = 0.0
    for j, term in enumerate(_three_terms(-LOG2E * c)):
        put = jnp.where((cidx == r * LANES + HEAD_DIM + j) & (r < C_HEADS), 1.0, 0.0).astype(BF16)
        placed = placed + jnp.dot(term, put, preferred_element_type=F32)
    return placed


def _store_fox_keys(out_ref, kc, c):
    bias = _lane_tiles(_fox_key_bias(c))
    for p, pair in enumerate(_lane_tiles(kc)):
        for u, tile in enumerate(_split_pair(pair)):
            h = 2 * p + u
            out_ref[:, h * LANES:(h + 1) * LANES] = (tile + bias[h]).astype(out_ref.dtype)


def _odd_proj_kernel(x_ref, g_ref, w_ref, bf_ref, c0_ref,
                     qc_o, kc_o, vc_o, qd_o, kd_o, vd_o, kc_s, vc_s, kd_s, vd_s, logf_s, carry_s,
                     *, tiles_per_seq):
    h = _rmsnorm_rows(x_ref[...], g_ref[...]).astype(BF16)
    p = jnp.dot(h, w_ref[...], preferred_element_type=F32)
    scale = HEAD_DIM ** -0.5 * LOG2E
    ones_col = _one_hot_row(HEAD_DIM)

    f = p[:, _O_F:_O_QD] + bf_ref[...]
    logf = jnp.minimum(f, 0.0) - jnp.log1p(jnp.exp(-jnp.abs(f)))
    logf = jnp.where(_lane_index(logf.shape) < C_HEADS, logf, 0.0)
    logf_s[...] = logf[:, :C_HEADS]
    c = _cumulative(logf, carry_s, c0_ref, pl.program_id(0) % tiles_per_seq == 0)

    kc, vc, kd, vd = p[:, _O_KC:_O_VC], p[:, _O_VC:_O_F], p[:, _O_KD:_O_VD], p[:, _O_VD:_O_END]
    kc_s[...], vc_s[...], kd_s[...], vd_s[...] = kc, vc, kd, vd
    _store_heads(qc_o, p[:, _O_QC:_O_KC], scale=scale,
                 extra=_one_hot_row(*range(HEAD_DIM, HEAD_DIM + FOX_BIAS_TERMS)))
    _store_fox_keys(kc_o, kc, c)
    _store_heads(vc_o, vc, extra=ones_col)
    _store_heads(qd_o, p[:, _O_QD:_O_KD], scale=scale)
    _store_heads(kd_o, kd)
    _store_heads(vd_o, vd)


def _odd_weights(w_in):
    cuts = np.cumsum(ODD_SPLIT)[:-1].tolist()
    qc, kc, vc, fc, qd, kd, vd = jnp.split(w_in, cuts, axis=1)
    w = jnp.concatenate([qc, kc, vc, _pad_cols(fc, LANES), qd, kd, vd], axis=1)
    assert w.shape[1] == _O_END
    return w.astype(BF16)


def _odd_proj(x, g, w, b_f, c0, *, seq_len, tm=256):
    m, d = x.shape
    tm = min(tm, seq_len)
    assert seq_len % tm == 0
    tiles_per_seq = seq_len // tm
    row = lambda cols: pl.BlockSpec((tm, cols), lambda i: (i, 0))
    const = lambda a: pl.BlockSpec(a.shape, lambda i: (0,) * a.ndim)
    g2 = g.reshape(1, d)
    bf2 = _pad_cols(b_f.reshape(1, C_HEADS), LANES)
    wide, flat = C_HEADS * LANES, C_HEADS * HEAD_DIM
    outs = [(wide, BF16)] * 6 + [(flat, F32)] * 4 + [(C_HEADS, F32)]
    return pl.pallas_call(
        functools.partial(_odd_proj_kernel, tiles_per_seq=tiles_per_seq),
        grid=(m // tm,),
        in_specs=[row(d), const(g2), const(w), const(bf2),
                  pl.BlockSpec((1, 1, LANES), lambda i: (i // tiles_per_seq, 0, 0))],
        out_specs=[row(c) for c, _ in outs],
        out_shape=[jax.ShapeDtypeStruct((m, c), t) for c, t in outs],
        scratch_shapes=[pltpu.VMEM((8, LANES), F32)],
        compiler_params=_params("arbitrary"),
        name="odd_proj",
    )(x, g2, w, bf2, c0)


def _fox_past_kernel(k_ref, logf_ref, c0_ref, k_o, cend_o, carry_s, *, tiles_per_seq):
    logf = jnp.pad(logf_ref[...], ((0, 0), (0, LANES - C_HEADS)))
    c = _cumulative(logf, carry_s, c0_ref, pl.program_id(0) % tiles_per_seq == 0)
    _store_fox_keys(k_o, k_ref[...], c)
    cend_o[0] = c[c.shape[0] - 1:, :]


def _fox_past(k, logf, *, seq_len, tm=256):
    m = k.shape[0]
    nseq = m // seq_len
    tm = min(tm, seq_len)
    tiles_per_seq = seq_len // tm
    row = lambda cols: pl.BlockSpec((tm, cols), lambda i: (i, 0))
    per_seq = pl.BlockSpec((1, 1, LANES), lambda i: (i // tiles_per_seq, 0, 0))
    return pl.pallas_call(
        functools.partial(_fox_past_kernel, tiles_per_seq=tiles_per_seq),
        grid=(m // tm,),
        in_specs=[row(C_HEADS * HEAD_DIM), row(C_HEADS), per_seq],
        out_specs=[row(C_HEADS * LANES), per_seq],
        out_shape=[jax.ShapeDtypeStruct((m, C_HEADS * LANES), BF16),
                   jax.ShapeDtypeStruct((nseq, 1, LANES), F32)],
        scratch_shapes=[pltpu.VMEM((8, LANES), F32)],
        compiler_params=_params("arbitrary"),
        name="fox_past",
    )(k, logf, jnp.zeros((nseq, 1, LANES), F32))


def _head_lanes(parts, ones_col=False):
    bsz, s_len, heads = parts[0].shape[:3]
    parts = [p.astype(BF16) for p in parts]
    used = sum(p.shape[-1] for p in parts)
    if ones_col:
        parts.append(jnp.ones((bsz, s_len, heads, 1), BF16))
        used += 1
    parts.append(jnp.zeros((bsz, s_len, heads, LANES - used), BF16))
    return jnp.concatenate(parts, axis=-1).reshape(bsz, s_len, heads * LANES)


def _keys(past, new, sk_pad):
    rows = new if past is None else jnp.concatenate([past, new], axis=1)
    pad = sk_pad - rows.shape[1]
    return jnp.pad(rows, ((0, 0), (0, pad), (0, 0))) if pad else rows


def _tiles(s_len, past_len):
    tq = min(256, s_len)
    tk, group = (256, 4) if past_len == 0 else (128, 3)
    return tq, tk, group, _round_up(past_len + s_len, tk * group)


def _fine_group(group):
    return group // 2 if group % 2 == 0 else group


def _even_mixer(x, h_gain, tables, past, weights, g_bq, g_bkv, w_o):
    bsz, s_len, d = x.shape
    m = bsz * s_len
    past_len = 0 if past is None else past[0].shape[1]
    sk = past_len + s_len
    tq, tk, group, sk_pad = _tiles(s_len, past_len)
    outs = _even_proj(x.reshape(m, d), h_gain, weights, g_bq, g_bkv, tables, seq_len=s_len)
    qa, ka, va, qi, ki, wi, qb, kb, vb, ka_s, va_s, ki_s, lat_s, kr_s = [
        o.reshape(bsz, s_len, -1) for o in outs]
    new_rows = (ka_s.reshape(bsz, s_len, A_KV_HEADS, HEAD_DIM), va_s.reshape(bsz, s_len, A_KV_HEADS, HEAD_DIM),
                ki_s, lat_s, kr_s)
    if past is None:
        pa = (None,) * 5
    else:
        c_k, c_v, c_ki, c_lat, c_kr = past
        kb_past, vb_past = _mla_past(c_lat.reshape(bsz * past_len, KV_RANK),
                                     c_kr.reshape(bsz * past_len, B_ROPE), weights[2])
        pa = (_head_lanes([c_k]), _head_lanes([c_v], ones_col=True), c_ki.astype(BF16),
              kb_past.reshape(bsz, past_len, -1), vb_past.reshape(bsz, past_len, -1))

    out_a = _dsa_attention(qa, _keys(pa[0], ka, sk_pad), _keys(pa[1], va, sk_pad), qi, _keys(pa[2], ki, sk_pad),
                           wi, q_off=past_len, sk=sk, tq=tq, tk=tk, group=group)
    out_b = _softmax_attention(qb, _keys(pa[3], kb, sk_pad), _keys(pa[4], vb, sk_pad),
                               heads=B_HEADS, mask_kind="chunk", q_off=past_len, sk=sk, tq=tq, tk=tk,
                               group=_fine_group(group), name="mla_attention")
    mix = jnp.concatenate([out_a, out_b], axis=-1).reshape(m, -1)
    y = _matmul_residual(mix, w_o, x.reshape(m, d)).reshape(bsz, s_len, d)
    return y, new_rows


def _odd_mixer(x, h_gain, past, w, b_f, w_o):
    bsz, s_len, d = x.shape
    m = bsz * s_len
    past_len = 0 if past is None else past[0].shape[1]
    sk = past_len + s_len
    tq, tk, group, sk_pad = _tiles(s_len, past_len)
    if past is None:
        pa = (None,) * 4
        c0 = jnp.zeros((bsz, 1, LANES), F32)
    else:
        c_k, c_v, c_logf, d_k, d_v = past
        kc_past, c0 = _fox_past(c_k.reshape(bsz * past_len, -1), c_logf.reshape(bsz * past_len, C_HEADS),
                                seq_len=past_len)
        pa = (kc_past.reshape(bsz, past_len, -1), _head_lanes([c_v], ones_col=True),
              _head_lanes([d_k]), _head_lanes([d_v]))
    outs = _odd_proj(x.reshape(m, d), h_gain, w, b_f, c0, seq_len=s_len)
    qc, kc, vc, qd, kd, vd, kc_s, vc_s, kd_s, vd_s, logf = [o.reshape(bsz, s_len, -1) for o in outs]
    heads = lambda a: a.reshape(bsz, s_len, -1, HEAD_DIM)
    new_rows = (heads(kc_s), heads(vc_s), logf, heads(kd_s), heads(vd_s))

    out_c = _softmax_attention(qc, _keys(pa[0], kc, sk_pad), _keys(pa[1], vc, sk_pad),
                               heads=C_HEADS, mask_kind="causal", q_off=past_len, sk=sk, tq=tq, tk=tk,
                               group=_fine_group(group), name="fox_attention")
    out_d = _sb_attention(qd, _keys(pa[2], kd, sk_pad), _keys(pa[3], vd, sk_pad),
                          q_off=past_len, sk=sk, tq=tq, tk=tk)
    mix = jnp.concatenate([out_c, out_d], axis=-1).reshape(m, -1)
    y = _matmul_residual(mix, w_o, x.reshape(m, d)).reshape(bsz, s_len, d)
    return y, new_rows


def _final_norm_kernel(x_ref, g_ref, o_ref):
    x = x_ref[...]
    ms = jnp.mean(x * x, axis=-1, keepdims=True)
    o_ref[...] = x * lax.rsqrt(ms + EPS) * g_ref[...]


def _final_norm(x, g, *, tm=512):
    m, d = x.shape
    tm = min(tm, m)
    return pl.pallas_call(
        _final_norm_kernel,
        grid=(m // tm,),
        in_specs=[pl.BlockSpec((tm, d), lambda i: (i, 0)), pl.BlockSpec((1, d), lambda i: (0, 0))],
        out_specs=pl.BlockSpec((tm, d), lambda i: (i, 0)),
        out_shape=jax.ShapeDtypeStruct((m, d), F32),
        compiler_params=_params("parallel"),
        name="final_norm",
    )(x, g.reshape(1, d).astype(F32))


def _trunk(x, q_pos, caches, params, even_w, odd_w):
    (g_mix, g_ffn, g_final, w_in_even, g_b_q, g_b_kv, w_b_uq, w_b_ukv, w_o_even,
     w_in_odd, b_forget, w_o_odd, w_up, w_conv, b_conv, w_down) = params
    bsz, s_len, d = x.shape
    depth = g_mix.shape[0]
    even_rows = [[] for _ in range(5)]
    odd_rows = [[] for _ in range(5)]
    conv_rows = []
    rows = max(s_len, min(512, bsz * s_len))
    tables = (_rope_tables(q_pos, rows, period=HEAD_DIM, start=0, n_rot=ROT_DIM),
              _rope_tables(q_pos, rows, period=IDX_DIM, start=0, n_rot=IDX_ROT),
              _rope_tables(q_pos, rows, period=LANES, start=_B_ROPE_LANE, n_rot=B_ROPE))
    for l in range(depth):
        j = l // 2
        if l % 2 == 0:
            past = None if caches is None else tuple(c[j] for c in caches[0:5])
            x, rows = _even_mixer(x, g_mix[l], tables, past, even_w[j], g_b_q[j], g_b_kv[j], w_o_even[j])
            for lst, r in zip(even_rows, rows):
                lst.append(r)
        else:
            past = None if caches is None else tuple(c[j] for c in caches[5:10])
            x, rows = _odd_mixer(x, g_mix[l], past, odd_w[j], b_forget[j], w_o_odd[j])
            for lst, r in zip(odd_rows, rows):
                lst.append(r)
        state = jnp.zeros((bsz, CONV_W - 1, D_FF), F32) if caches is None else caches[10][l]
        y, new_buf = _conv_ffn(x.reshape(bsz * s_len, d), g_ffn[l], w_up[l], w_conv[l], b_conv[l], w_down[l],
                               state, seq_len=s_len, tm=1024)
        x = y.reshape(bsz, s_len, d)
        conv_rows.append(new_buf)
    out = _final_norm(x.reshape(bsz * s_len, d), g_final).reshape(bsz, s_len, d)
    states = [jnp.stack(r, axis=0) for r in even_rows + odd_rows] + [jnp.stack(conv_rows, axis=0)]
    return out, states


def kernel(x_prompt, x_sample, cache_a_k, cache_a_v, cache_a_idx_k, cache_b_latent, cache_b_rope,
           cache_c_k, cache_c_v, cache_c_logf, cache_d_k, cache_d_v, state_ffn_conv,
           g_mix, g_ffn, g_final, w_in_even, g_b_q, g_b_kv, w_b_uq, w_b_ukv, w_o_even,
           w_in_odd, b_forget, w_o_odd, w_up, w_conv, b_conv, w_down):
    params = (g_mix, g_ffn, g_final, w_in_even, g_b_q, g_b_kv, w_b_uq, w_b_ukv, w_o_even,
              w_in_odd, b_forget, w_o_odd, w_up, w_conv, b_conv, w_down)
    caches = (cache_a_k, cache_a_v, cache_a_idx_k, cache_b_latent, cache_b_rope,
              cache_c_k, cache_c_v, cache_c_logf, cache_d_k, cache_d_v, state_ffn_conv)
    past_len = cache_a_k.shape[2]
    pos_prompt = jnp.arange(x_prompt.shape[1], dtype=I32)
    pos_sample = past_len + jnp.arange(x_sample.shape[1], dtype=I32)
    even_w = [_even_weights(w_in_even[j], w_b_uq[j], w_b_ukv[j]) for j in range(w_in_even.shape[0])]
    odd_w = [_odd_weights(w_in_odd[j]) for j in range(w_in_odd.shape[0])]
    y_prompt, p_states = _trunk(x_prompt, pos_prompt, None, params, even_w, odd_w)
    y_sample, s_states = _trunk(x_sample, pos_sample, caches, params, even_w, odd_w)
    return (y_prompt, y_sample, *p_states, *s_states)
```

```python
import functools

import jax
import jax.numpy as jnp
import numpy as np
from jax import lax
from jax.experimental import pallas as pl
from jax.experimental.pallas import tpu as pltpu

F32 = jnp.float32
BF16 = jnp.bfloat16
I32 = jnp.int32

CHUNK = 64
ROPE_THETA = 500000.0
EPS = 1e-6
HEAD_DIM = 64
ROT_DIM = HEAD_DIM // 4
A_HEADS = 8
A_KV_HEADS = 2
IDX_HEADS = 8
IDX_DIM = 32
IDX_ROT = IDX_DIM // 4
TOPK_MAX = 256
B_HEADS = 8
Q_RANK = 256
KV_RANK = 128
B_NOPE = 64
B_ROPE = 32
B_VDIM = 64
C_HEADS = 8
D_HEADS = 8
D_FF = 2816
CONV_W = 3

EVEN_SPLIT = [A_HEADS * HEAD_DIM, A_KV_HEADS * HEAD_DIM, A_KV_HEADS * HEAD_DIM,
              IDX_HEADS * IDX_DIM, IDX_DIM, IDX_HEADS, Q_RANK, KV_RANK, B_ROPE]
ODD_SPLIT = [C_HEADS * HEAD_DIM] * 3 + [C_HEADS] + [D_HEADS * HEAD_DIM] * 3

LANES = 128
VMEM_LIMIT_BYTES = 56 * 1024 * 1024
MASKED = -1e30
INT_MIN = -2 ** 31
LOG2E = 1.4426950408889634
F32_EXP2_UNDERFLOW = -152.0

_NT = (((1,), (1,)), ((), ()))


def _params(*sem):
    return pltpu.CompilerParams(dimension_semantics=sem, vmem_limit_bytes=VMEM_LIMIT_BYTES)


def _round_up(n, m):
    return (n + m - 1) // m * m


def _chunk_of(pos):
    return jnp.right_shift(pos, CHUNK.bit_length() - 1)


def _proj_kernel(x_ref, g_ref, w_ref, *out_refs, norm, emit_h):
    x = x_ref[...]
    if norm:
        ms = jnp.mean(x * x, axis=-1, keepdims=True)
        x = x * lax.rsqrt(ms + EPS) * g_ref[...]
    out_refs[0][...] = jnp.dot(x.astype(BF16), w_ref[...], preferred_element_type=F32)
    if emit_h:
        out_refs[1][...] = x


def _proj(x, g, w, *, norm=True, emit_h=False, tm=512):
    m, k = x.shape
    n = w.shape[1]
    tm = min(tm, m)
    assert m % tm == 0
    out_shape = [jax.ShapeDtypeStruct((m, n), F32)]
    out_specs = [pl.BlockSpec((tm, n), lambda i: (i, 0))]
    if emit_h:
        out_shape.append(jax.ShapeDtypeStruct((m, k), F32))
        out_specs.append(pl.BlockSpec((tm, k), lambda i: (i, 0)))
    res = pl.pallas_call(
        functools.partial(_proj_kernel, norm=norm, emit_h=emit_h),
        grid=(m // tm,),
        in_specs=[pl.BlockSpec((tm, k), lambda i: (i, 0)),
                  pl.BlockSpec((1, k), lambda i: (0, 0)),
                  pl.BlockSpec((k, n), lambda i: (0, 0))],
        out_specs=out_specs,
        out_shape=out_shape,
        compiler_params=_params("parallel"),
        name="proj",
    )(x, g.reshape(1, k).astype(F32), w.astype(BF16))
    return res if emit_h else res[0]


def _matmul_residual_kernel(a_ref, b_ref, wa_ref, wb_ref, r_ref, o_ref):
    o_ref[...] = (r_ref[...] + jnp.dot(a_ref[...], wa_ref[...], preferred_element_type=F32)
                  + jnp.dot(b_ref[...], wb_ref[...], preferred_element_type=F32))


def _matmul_residual(a, b, w, res, *, tm=512):
    m, ka = a.shape
    kb = b.shape[1]
    n = w.shape[1]
    tm = min(tm, m)
    assert m % tm == 0 and w.shape[0] == ka + kb
    row = lambda cols: pl.BlockSpec((tm, cols), lambda i: (i, 0))
    const = lambda rows: pl.BlockSpec((rows, n), lambda i: (0, 0))
    w = w.astype(BF16)
    return pl.pallas_call(
        _matmul_residual_kernel,
        grid=(m // tm,),
        in_specs=[row(ka), row(kb), const(ka), const(kb), row(n)],
        out_specs=row(n),
        out_shape=jax.ShapeDtypeStruct((m, n), F32),
        compiler_params=_params("parallel"),
        name="matmul_residual",
    )(a, b, w[:ka], w[ka:], res)


def _conv_ffn_kernel(x_ref, g_ref, wg_ref, wu_ref, wc_ref, bc_ref, wd_ref, st_ref,
                     y_ref, ns_ref, h_s, acc_s, carry_s, *, tiles_per_seq, seqs, tm, tf):
    i = pl.program_id(0)
    c = pl.program_id(1)
    nc = pl.num_programs(1)
    cols = pl.ds(pl.multiple_of(c * tf, tf), tf)
    rows_per = tm // seqs

    @pl.when(c == 0)
    def _():
        x = x_ref[...]
        ms = jnp.mean(x * x, axis=-1, keepdims=True)
        h_s[...] = (x * lax.rsqrt(ms + EPS) * g_ref[...]).astype(BF16)
        acc_s[...] = jnp.zeros_like(acc_s)

    h = h_s[...]
    gate = jnp.dot(h, wg_ref[...], preferred_element_type=F32)
    up = jnp.dot(h, wu_ref[...], preferred_element_type=F32)

    if seqs == 1:
        @pl.when(i % tiles_per_seq == 0)
        def _():
            carry_s[c] = st_ref[0, :, cols]
        before = [carry_s[c]]
    else:
        before = [st_ref[s, :, cols] for s in range(seqs)]
    row = lax.broadcasted_iota(I32, gate.shape, 0)
    g1 = pltpu.roll(gate, 1, 0)
    g2 = pltpu.roll(gate, 2, 0)
    for s, prev in enumerate(before):
        first = s * rows_per
        g1 = jnp.where(row == first, prev[1:2, :], g1)
        g2 = jnp.where(row == first, prev[0:1, :], jnp.where(row == first + 1, prev[1:2, :], g2))
    wc = wc_ref[...]
    gc = bc_ref[...] + g2 * wc[0:1, :]
    gc = gc + g1 * wc[1:2, :]
    gc = gc + gate * wc[2:3, :]
    act = gc * jax.nn.sigmoid(gc) * up
    acc_s[...] += jnp.dot(act.astype(BF16), wd_ref[...], preferred_element_type=F32)

    for s in range(seqs):
        ns_ref[s, :, cols] = gate[(s + 1) * rows_per - 2:(s + 1) * rows_per, :]
    if seqs == 1:
        carry_s[c] = gate[tm - 2:tm, :]

    @pl.when(c == nc - 1)
    def _():
        y_ref[...] = x_ref[...] + acc_s[...]


def _conv_ffn(x, g, w_up, w_conv, b_conv, w_down, state, *, seq_len, tm, tf=256):
    m, d = x.shape
    nseq = m // seq_len
    tm = min(tm, m)
    assert m % tm == 0 and D_FF % tf == 0 and (seq_len % tm == 0 or tm % seq_len == 0)
    tiles_per_seq = max(1, seq_len // tm)
    seqs = max(1, tm // seq_len)
    nc = D_FF // tf
    wg = w_up[:, :D_FF].astype(BF16)
    wu = w_up[:, D_FF:].astype(BF16)
    state_spec = pl.BlockSpec((seqs, CONV_W - 1, D_FF), lambda i, c: (i // tiles_per_seq, 0, 0))
    y, ns = pl.pallas_call(
        functools.partial(_conv_ffn_kernel, tiles_per_seq=tiles_per_seq, seqs=seqs, tm=tm, tf=tf),
        grid=(m // tm, nc),
        in_specs=[pl.BlockSpec((tm, d), lambda i, c: (i, 0)),
                  pl.BlockSpec((1, d), lambda i, c: (0, 0)),
                  pl.BlockSpec((d, tf), lambda i, c: (0, c)),
                  pl.BlockSpec((d, tf), lambda i, c: (0, c)),
                  pl.BlockSpec((CONV_W, tf), lambda i, c: (0, c)),
                  pl.BlockSpec((1, tf), lambda i, c: (0, c)),
                  pl.BlockSpec((tf, d), lambda i, c: (c, 0)),
                  state_spec],
        out_specs=[pl.BlockSpec((tm, d), lambda i, c: (i, 0)), state_spec],
        out_shape=[jax.ShapeDtypeStruct((m, d), F32),
                   jax.ShapeDtypeStruct((nseq, CONV_W - 1, D_FF), F32)],
        scratch_shapes=[pltpu.VMEM((tm, d), BF16),
                        pltpu.VMEM((tm, d), F32),
                        pltpu.VMEM((nc, CONV_W - 1, tf), F32)],
        compiler_params=_params("arbitrary", "arbitrary"),
        name="conv_ffn",
    )(x, g.reshape(1, d).astype(F32), wg, wu, w_conv.astype(F32), b_conv.reshape(1, D_FF).astype(F32),
      w_down.astype(BF16), state.astype(F32))
    return y, ns


def _attn_call(kernel, inputs, in_specs, out_cols, *, bsz, sq, tq, scratch=(), name):
    return pl.pallas_call(
        kernel,
        grid=(bsz, sq // tq),
        in_specs=in_specs,
        out_specs=pl.BlockSpec((1, tq, out_cols), lambda b, i: (b, i, 0)),
        out_shape=jax.ShapeDtypeStruct((bsz, sq, out_cols), BF16),
        scratch_shapes=list(scratch),
        compiler_params=_params("parallel", "arbitrary"),
        name=name,
    )(*inputs)


def _q_spec(tq, cols):
    return pl.BlockSpec((1, tq, cols), lambda b, i: (b, i, 0))


def _k_spec(rows, cols):
    return pl.BlockSpec((1, rows, cols), lambda b, i: (b, 0, 0))


def _tile_geometry(i, *, tq, span, q_off):
    q_lo = q_off + i * tq
    return q_lo, q_lo // span


def _lane_tiles(x):
    return [x[:, j * LANES:(j + 1) * LANES] for j in range(x.shape[1] // LANES)]


def _visible(kind, q_lo, ks, *, tq, tk, sk):
    q_pos = q_lo + lax.broadcasted_iota(I32, (tq, 1), 0)
    k_pos = ks + lax.broadcasted_iota(I32, (1, tk), 1)
    if kind == "chunk":
        return (_chunk_of(k_pos) <= _chunk_of(q_pos)) & (k_pos < sk)
    if kind == "causal":
        return k_pos <= q_pos
    assert kind == "strict"
    return k_pos < q_pos


HEADS_PER_STEP = 2


def _span_cases(i, body, *, tq, span, q_off, n_tiles):
    g_own = (q_off + i * tq) // span
    for g in sorted({(q_off + t * tq) // span for t in range(n_tiles)}):
        pl.when(g_own == g)(functools.partial(body, g + 1))


def _softmax_heads(qs, k_ref, v_ref, s_scr, kcols, vcols, *, tq, tk, blocks, own_from, own_mask,
                   bias_ref=None):
    row_max = []
    for slot, (q, kcol) in enumerate(zip(qs, kcols)):
        mx = jnp.full((tq, LANES), MASKED, F32)
        for b in range(blocks):
            ks = b * tk
            s = lax.dot_general(q, k_ref[0, ks:ks + tk, kcol], _NT, preferred_element_type=F32)
            if bias_ref is not None:
                s = s + bias_ref[:, ks:ks + tk]
            if own_mask is not None and b >= own_from:
                s = jnp.where(own_mask(ks), s, MASKED)
            s_scr[slot, :, ks:ks + tk] = s
            for t in _lane_tiles(s):
                mx = jnp.maximum(mx, t)
        row_max.append(jnp.broadcast_to(jnp.max(mx, axis=1, keepdims=True), (tq, LANES)))
    outs = []
    for slot, (m, vcol) in enumerate(zip(row_max, vcols)):
        acc = jnp.zeros((tq, LANES), F32)
        for b in range(blocks):
            ks = b * tk
            p = jnp.concatenate([jnp.exp2(t - m) for t in _lane_tiles(s_scr[slot, :, ks:ks + tk])], axis=1)
            acc = acc + jnp.dot(p.astype(BF16), v_ref[0, ks:ks + tk, vcol], preferred_element_type=F32)
        outs.append(acc[:, :HEAD_DIM] / acc[:, HEAD_DIM:HEAD_DIM + 1])
    return outs


def _lane_tile_at(index):
    return pl.ds(pl.multiple_of(index * LANES, LANES), LANES)


def _softmax_attn_kernel(q_ref, k_ref, v_ref, o_ref, s_scr, *, heads, mask_kind, tq, tk, group, q_off, sk,
                         n_tiles):
    i = pl.program_id(1)
    own_mask = functools.partial(_visible, mask_kind, q_off + i * tq, tq=tq, tk=tk, sk=sk)

    def attend(n_spans):
        def step(hp, _):
            cols = [_lane_tile_at(hp * HEADS_PER_STEP + u) for u in range(HEADS_PER_STEP)]
            outs = _softmax_heads([q_ref[0, :, c] for c in cols], k_ref, v_ref, s_scr, cols, cols,
                                  tq=tq, tk=tk, blocks=n_spans * group, own_from=(n_spans - 1) * group,
                                  own_mask=own_mask)
            o_ref[0, :, _lane_tile_at(hp)] = jnp.concatenate(outs, axis=1).astype(o_ref.dtype)
            return 0
        lax.fori_loop(0, heads // HEADS_PER_STEP, step, 0)

    _span_cases(i, attend, tq=tq, span=tk * group, q_off=q_off, n_tiles=n_tiles)


def _check_tiling(sk_pad, *, tq, tk, group, q_off):
    assert (tk * group) % tq == 0 and q_off % tq == 0 and sk_pad % (tk * group) == 0 and tk % LANES == 0
    assert HEADS_PER_STEP * HEAD_DIM == LANES


def _softmax_attention(q, k, v, *, heads, mask_kind, q_off, sk, tq, tk, group, name):
    bsz, sq, _ = q.shape
    sk_pad = k.shape[1]
    _check_tiling(sk_pad, tq=tq, tk=tk, group=group, q_off=q_off)
    kern = functools.partial(_softmax_attn_kernel, heads=heads, mask_kind=mask_kind,
                             tq=tq, tk=tk, group=group, q_off=q_off, sk=sk, n_tiles=sq // tq)
    cols = heads * LANES
    return _attn_call(kern, (q, k, v), [_q_spec(tq, cols), _k_spec(sk_pad, cols), _k_spec(sk_pad, cols)],
                      heads * HEAD_DIM, bsz=bsz, sq=sq, tq=tq,
                      scratch=(pltpu.VMEM((HEADS_PER_STEP, tq, sk_pad), F32),), name=name)


def _split_bf16(x):
    hi = x.astype(BF16)
    lo = (x - hi.astype(F32)).astype(BF16)
    return hi, lo


SB_HEADS_PER_LOOP = 2


def _sb_kernel(q_ref, k_ref, v_ref, o_ref, *, tq, tk, q_off, sk):
    q_lo, n_before = _tile_geometry(pl.program_id(1), tq=tq, span=tk, q_off=q_off)
    r = lax.broadcasted_iota(I32, (tk, tk), 0)
    cidx = lax.broadcasted_iota(I32, (tk, tk), 1)
    later = jnp.where(r > cidx, 1.0, 0.0).astype(BF16)

    def block(q, h, kb, run, acc, mask):
        ks = pl.multiple_of(kb * tk, tk)
        k = k_ref[0, pl.ds(ks, tk), h * LANES:(h + 1) * LANES]
        v = v_ref[0, pl.ds(ks, tk), h * LANES:(h + 1) * LANES]
        z = lax.dot_general(q, k, _NT, preferred_element_type=F32)
        log_beta = jnp.minimum(z, 0.0) - jnp.log2(1.0 + jnp.exp2(-jnp.abs(z)))
        log_1m = log_beta - z
        if mask is not None:
            log_1m = jnp.where(mask, log_1m, 0.0)
        hi, lo = _split_bf16(log_1m)
        after = (jnp.dot(hi, later, preferred_element_type=F32)
                 + jnp.dot(lo, later, preferred_element_type=F32))
        a = jnp.exp2(log_beta + after + run)
        if mask is not None:
            a = jnp.where(mask, a, 0.0)
        acc = acc + jnp.dot(a.astype(BF16), v, preferred_element_type=F32)
        run = run + after[:, 0:1] + log_1m[:, 0:1]
        return run, acc

    own = _visible("strict", q_lo, n_before * tk, tq=tq, tk=tk, sk=sk)
    for h0 in range(0, D_HEADS, SB_HEADS_PER_LOOP):
        hs = range(h0, h0 + SB_HEADS_PER_LOOP)
        qs = [q_ref[0, :, h * LANES:(h + 1) * LANES] for h in hs]
        state = []
        for h, q in zip(hs, qs):
            state.extend(block(q, h, n_before, jnp.zeros((tq, 1), F32), jnp.zeros((tq, LANES), F32), own))

        def alive(state):
            top = functools.reduce(jnp.maximum, state[0::2])
            return jnp.max(top) > F32_EXP2_UNDERFLOW

        def cond(c):
            return (c[0] < n_before) & c[1]

        def body(c, hs=hs, qs=qs):
            j, _, state = c
            new = []
            for n, (h, q) in enumerate(zip(hs, qs)):
                new.extend(block(q, h, n_before - 1 - j, state[2 * n], state[2 * n + 1], None))
            return j + 1, alive(new), tuple(new)

        _, _, state = lax.while_loop(cond, body, (jnp.int32(0), alive(state), tuple(state)))
        for n, h in enumerate(hs):
            o_ref[0, :, h * HEAD_DIM:(h + 1) * HEAD_DIM] = state[2 * n + 1][:, :HEAD_DIM].astype(o_ref.dtype)


def _sb_attention(q, k, v, *, q_off, sk, tq, tk):
    bsz, sq, _ = q.shape
    sk_pad = k.shape[1]
    _check_tiling(sk_pad, tq=tq, tk=tk, group=1, q_off=q_off)
    kern = functools.partial(_sb_kernel, tq=tq, tk=tk, q_off=q_off, sk=sk)
    cols = D_HEADS * LANES
    return _attn_call(kern, (q, k, v), [_q_spec(tq, cols), _k_spec(sk_pad, cols), _k_spec(sk_pad, cols)],
                      D_HEADS * HEAD_DIM, bsz=bsz, sq=sq, tq=tq, name="sb_attention")


def _sortable_key(score):
    bits = lax.bitcast_convert_type(score, I32)
    return jnp.where(bits < 0, bits ^ 0x7FFFFFFF, bits)


KEY_OF_NEG_INF = -0x7F800001


def _dsa_kernel(q_ref, k_ref, v_ref, qi_ref, ki_ref, wi_ref, o_ref, key_s, bias_s, s_scr, w_s,
                *, tq, tk, group, q_off, sk, topk, n_tiles):
    i = pl.program_id(1)
    nt = tk // LANES
    own_visible = functools.partial(_visible, "chunk", q_off + i * tq, tq=tq, tk=tk, sk=sk)
    kf = float(topk)

    for h in range(IDX_HEADS):
        w_s[h] = jnp.broadcast_to(wi_ref[0, :, h:h + 1], (tq, LANES))

    def select_and_attend(n_spans):
        blocks, own_from = n_spans * group, (n_spans - 1) * group
        starts = [b * tk for b in range(blocks)]

        def own(ks, value, masked):
            return jnp.where(own_visible(ks), value, masked) if ks >= own_from * tk else value

        qis = [qi_ref[0, :, h * IDX_DIM:(h + 1) * IDX_DIM] for h in range(IDX_HEADS)]
        for ks in starts:
            ki = ki_ref[0, ks:ks + tk, :]
            total = jnp.zeros((tq, tk), F32)
            for h in range(IDX_HEADS):
                sc = lax.dot_general(qis[h], ki, _NT, preferred_element_type=F32)
                total = total + jnp.maximum(sc, 0.0) * jnp.concatenate([w_s[h]] * nt, axis=1)
            key_s[:, ks:ks + tk] = _sortable_key(own(ks, total, -jnp.inf))

        def count(hit):
            cnt = jnp.zeros((tq, LANES), F32)
            for ks in starts:
                for t in _lane_tiles(key_s[:, ks:ks + tk]):
                    cnt = cnt + jnp.where(hit(t), 1.0, 0.0)
            return jnp.sum(cnt, axis=1, keepdims=True)

        def count_ge(cand):
            cand = jnp.broadcast_to(cand, (tq, LANES))
            return count(lambda t: t >= cand)

        lowest = jnp.full((tq, 1), INT_MIN, I32)
        zero = jnp.zeros((tq, 1), I32)
        n_zero = count_ge(zero)
        ok = n_zero >= kf
        thr = jnp.where(ok, zero, lowest)
        n_thr = jnp.where(ok, n_zero, float(blocks * tk))

        def unsettled(n_thr):
            return jnp.max(jnp.abs(n_thr - kf)) > 0.0

        def bit_step(c):
            it, _, thr, n_thr = c
            cand = thr | jnp.left_shift(jnp.int32(1), 30 - it)
            n_cand = count_ge(cand)
            ok = n_cand >= kf
            n_thr = jnp.where(ok, n_cand, n_thr)
            return it + 1, unsettled(n_thr), jnp.where(ok, cand, thr), n_thr

        _, _, thr, n_thr = lax.while_loop(lambda c: (c[0] < 31) & c[1], bit_step,
                                          (jnp.int32(0), unsettled(n_thr), thr, n_thr))
        thr_b = jnp.broadcast_to(thr, (tq, LANES))

        tied = jnp.max(jnp.where(thr > KEY_OF_NEG_INF, n_thr, 0.0)) > kf

        @pl.when(jnp.logical_not(tied))
        def _():
            for ks in starts:
                tiles = _lane_tiles(key_s[:, ks:ks + tk])
                bias = jnp.concatenate([jnp.where(t >= thr_b, 0.0, MASKED) for t in tiles], axis=1)
                bias_s[:, ks:ks + tk] = own(ks, bias, MASKED)

        @pl.when(tied)
        def _():
            r = lax.broadcasted_iota(I32, (tk, tk), 0)
            cidx = lax.broadcasted_iota(I32, (tk, tk), 1)
            upto = jnp.where(r <= cidx, 1.0, 0.0).astype(BF16)
            need = kf - count(lambda t: t > thr_b)
            seen = jnp.zeros((tq, 1), F32)
            for ks in starts:
                key = key_s[:, ks:ks + tk]
                tie = jnp.where(key == thr, 1.0, 0.0)
                rank = seen + jnp.dot(tie.astype(BF16), upto, preferred_element_type=F32)
                sel = (key > thr) | ((key == thr) & (rank <= need))
                bias_s[:, ks:ks + tk] = own(ks, jnp.where(sel, 0.0, MASKED), MASKED)
                seen = seen + jnp.sum(tie, axis=1, keepdims=True)

        def step(hp, _):
            heads = [hp * HEADS_PER_STEP + u for u in range(HEADS_PER_STEP)]
            kv = [_lane_tile_at(hd // (A_HEADS // A_KV_HEADS)) for hd in heads]
            outs = _softmax_heads([q_ref[0, :, _lane_tile_at(hd)] for hd in heads], k_ref, v_ref, s_scr,
                                  kv, kv, tq=tq, tk=tk, blocks=blocks, own_from=own_from, own_mask=None,
                                  bias_ref=bias_s)
            o_ref[0, :, _lane_tile_at(hp)] = jnp.concatenate(outs, axis=1).astype(o_ref.dtype)
            return 0
        lax.fori_loop(0, A_HEADS // HEADS_PER_STEP, step, 0)

    _span_cases(i, select_and_attend, tq=tq, span=tk * group, q_off=q_off, n_tiles=n_tiles)


def _dsa_attention(q, k, v, qi, ki, wi, *, q_off, sk, tq, tk, group):
    bsz, sq, _ = q.shape
    sk_pad = k.shape[1]
    _check_tiling(sk_pad, tq=tq, tk=tk, group=group, q_off=q_off)
    topk = min(TOPK_MAX, sk // 4)
    kern = functools.partial(_dsa_kernel, tq=tq, tk=tk, group=group, q_off=q_off, sk=sk, topk=topk,
                             n_tiles=sq // tq)
    kv_cols = A_KV_HEADS * LANES
    return _attn_call(kern, (q, k, v, qi, ki, wi),
                      [_q_spec(tq, A_HEADS * LANES), _k_spec(sk_pad, kv_cols), _k_spec(sk_pad, kv_cols),
                       _q_spec(tq, IDX_HEADS * IDX_DIM), _k_spec(sk_pad, IDX_DIM), _q_spec(tq, IDX_HEADS)],
                      A_HEADS * HEAD_DIM, bsz=bsz, sq=sq, tq=tq,
                      scratch=(pltpu.VMEM((tq, sk_pad), I32), pltpu.VMEM((tq, sk_pad), F32),
                               pltpu.VMEM((HEADS_PER_STEP, tq, sk_pad), F32),
                               pltpu.VMEM((IDX_HEADS, tq, LANES), F32)),
                      name="dsa_attention")


def _rmsnorm_rows(x, g):
    ms = jnp.mean(x * x, axis=-1, keepdims=True)
    return x * lax.rsqrt(ms + EPS) * g


def _lane_index(shape):
    return lax.broadcasted_iota(I32, shape, len(shape) - 1)


def _one_hot_row(*lanes):
    lane = _lane_index((1, LANES))
    hit = functools.reduce(jnp.logical_or, [lane == l for l in lanes])
    return jnp.where(hit, 1.0, 0.0)


def _rotate(tile, tables, half):
    cos, sin_up, sin_dn = tables
    return tile * cos + pltpu.roll(tile, half, 1) * sin_up + pltpu.roll(tile, LANES - half, 1) * sin_dn


def _split_pair(pair):
    low = _lane_index(pair.shape) < HEAD_DIM
    return jnp.where(low, pair, 0.0), jnp.where(low, pltpu.roll(pair, HEAD_DIM, 1), 0.0)


def _store_heads(out_ref, compact, *, scale=None, extra=None):
    for p, pair in enumerate(_lane_tiles(compact)):
        for u, tile in enumerate(_split_pair(pair)):
            if scale is not None:
                tile = tile * scale
            if extra is not None:
                tile = tile + extra
            h = 2 * p + u
            out_ref[:, h * LANES:(h + 1) * LANES] = tile.astype(out_ref.dtype)


_E_QA, _E_KA, _E_VA, _E_QI, _E_KI, _E_WI, _E_CQ, _E_CKV, _E_KR, _E_END = (
    0, 512, 640, 768, 1024, 1152, 1280, 1536, 1664, 1792)
_B_ROPE_LANE = B_NOPE


def _even_proj_kernel(x_ref, g_ref, w_ref, gq_ref, wuq_ref, gkv_ref, wukv_ref, ta_ref, ti_ref, tb_ref,
                      qa_o, ka_o, va_o, qi_o, ki_o, wi_o, qb_o, kb_o, vb_o,
                      ka_s, va_s, ki_s, lat_s, kr_s):
    h = _rmsnorm_rows(x_ref[...], g_ref[...]).astype(BF16)
    p = jnp.dot(h, w_ref[...], preferred_element_type=F32)
    ta = (ta_ref[0], ta_ref[1], ta_ref[2])
    ti = (ti_ref[0], ti_ref[1], ti_ref[2])
    tb = (tb_ref[0], tb_ref[1], tb_ref[2])
    ones_col = _one_hot_row(HEAD_DIM)

    qa = jnp.concatenate([_rotate(t, ta, ROT_DIM // 2) for t in _lane_tiles(p[:, _E_QA:_E_KA])], axis=1)
    _store_heads(qa_o, qa, scale=HEAD_DIM ** -0.5 * LOG2E)
    ka = _rotate(p[:, _E_KA:_E_VA], ta, ROT_DIM // 2)
    ka_s[...] = ka
    _store_heads(ka_o, ka)
    va = p[:, _E_VA:_E_QI]
    va_s[...] = va
    _store_heads(va_o, va, extra=ones_col)
    qi = jnp.concatenate([_rotate(t, ti, IDX_ROT // 2) for t in _lane_tiles(p[:, _E_QI:_E_KI])], axis=1)
    qi_o[...] = (qi * IDX_DIM ** -0.5).astype(qi_o.dtype)
    ki = _rotate(p[:, _E_KI:_E_WI], ti, IDX_ROT // 2)[:, :IDX_DIM]
    ki_s[...] = ki
    ki_o[...] = ki.astype(ki_o.dtype)
    wi_o[...] = p[:, _E_WI:_E_WI + IDX_HEADS] * IDX_HEADS ** -0.5

    cq = _rmsnorm_rows(p[:, _E_CQ:_E_CKV], gq_ref[...]).astype(BF16)
    qb = jnp.dot(cq, wuq_ref[...], preferred_element_type=F32)
    scale_b = (B_NOPE + B_ROPE) ** -0.5 * LOG2E
    for hd, t in enumerate(_lane_tiles(qb)):
        qb_o[:, hd * LANES:(hd + 1) * LANES] = (_rotate(t, tb, B_ROPE // 2) * scale_b).astype(qb_o.dtype)
    lat = _rmsnorm_rows(p[:, _E_CKV:_E_KR], gkv_ref[...])
    lat_s[...] = lat
    kv = jnp.dot(lat.astype(BF16), wukv_ref[...], preferred_element_type=F32)
    kr = _rotate(p[:, _E_KR:_E_END], tb, B_ROPE // 2)
    kr_s[...] = kr[:, _B_ROPE_LANE:_B_ROPE_LANE + B_ROPE]
    _store_mla_keys(kb_o, vb_o, kv, kr)


def _store_mla_keys(kb_o, vb_o, kv, kr):
    ones_col = _one_hot_row(HEAD_DIM)
    tiles = _lane_tiles(kv)
    for hd in range(B_HEADS):
        kb_o[:, hd * LANES:(hd + 1) * LANES] = (tiles[hd] + kr).astype(kb_o.dtype)
        vb_o[:, hd * LANES:(hd + 1) * LANES] = (tiles[B_HEADS + hd] + ones_col).astype(vb_o.dtype)


def _mla_past_kernel(lat_ref, kr_ref, wukv_ref, kb_o, vb_o):
    kv = jnp.dot(lat_ref[...].astype(BF16), wukv_ref[...], preferred_element_type=F32)
    rows = kv.shape[0]
    kr = jnp.concatenate([jnp.zeros((rows, _B_ROPE_LANE), F32), kr_ref[...],
                          jnp.zeros((rows, LANES - _B_ROPE_LANE - B_ROPE), F32)], axis=1)
    _store_mla_keys(kb_o, vb_o, kv, kr)


def _mla_past(lat, kr, wukv, *, tm=512):
    m = lat.shape[0]
    tm = min(tm, m)
    assert m % tm == 0
    row = lambda cols: pl.BlockSpec((tm, cols), lambda i: (i, 0))
    wide = B_HEADS * LANES
    return pl.pallas_call(
        _mla_past_kernel,
        grid=(m // tm,),
        in_specs=[row(KV_RANK), row(B_ROPE), pl.BlockSpec(wukv.shape, lambda i: (0, 0))],
        out_specs=[row(wide), row(wide)],
        out_shape=[jax.ShapeDtypeStruct((m, wide), BF16)] * 2,
        compiler_params=_params("parallel"),
        name="mla_past",
    )(lat, kr, wukv)


def _pad_cols(w, width):
    return jnp.pad(w, ((0, 0), (0, width - w.shape[1])))


def _even_weights(w_in, w_uq, w_ukv):
    cuts = np.cumsum(EVEN_SPLIT)[:-1].tolist()
    qa, ka, va, qi, ki, wi, cq, ckv, kr = jnp.split(w_in, cuts, axis=1)
    kr = jnp.pad(kr, ((0, 0), (_B_ROPE_LANE, LANES - _B_ROPE_LANE - B_ROPE)))
    w = jnp.concatenate([qa, ka, va, qi, _pad_cols(ki, LANES), _pad_cols(wi, LANES), cq, ckv, kr], axis=1)
    d = w_in.shape[0]
    wuq = jnp.pad(w_uq.reshape(Q_RANK, B_HEADS, B_NOPE + B_ROPE),
                  ((0, 0), (0, 0), (0, LANES - B_NOPE - B_ROPE))).reshape(Q_RANK, B_HEADS * LANES)
    kvw = w_ukv.reshape(KV_RANK, B_HEADS, B_NOPE + B_VDIM)
    pad = lambda a: jnp.pad(a, ((0, 0), (0, 0), (0, LANES - a.shape[2]))).reshape(KV_RANK, B_HEADS * LANES)
    wukv = jnp.concatenate([pad(kvw[:, :, :B_NOPE]), pad(kvw[:, :, B_NOPE:])], axis=1)
    assert w.shape == (d, _E_END)
    return w.astype(BF16), wuq.astype(BF16), wukv.astype(BF16)


def _rope_tables(pos, rows, *, period, start, n_rot):
    half = n_rot // 2
    inv = ROPE_THETA ** (-jnp.arange(half, dtype=F32) * 2.0 / n_rot)
    ang = pos.astype(F32)[:, None] * inv[None, :]
    cos, sin = jnp.cos(ang), jnp.sin(ang)
    off = np.arange(LANES) % period - start
    idx = np.where((off >= 0) & (off < n_rot), off % half, 0)
    lower = (off >= 0) & (off < half)
    upper = (off >= half) & (off < n_rot)
    cos_t = jnp.where(lower | upper, cos[:, idx], 1.0)
    sin_up = jnp.where(upper, sin[:, idx], 0.0)
    sin_dn = jnp.where(lower, -sin[:, idx], 0.0)
    tabs = jnp.stack([cos_t, sin_up, sin_dn])
    return jnp.tile(tabs, (1, rows // pos.shape[0], 1))


def _even_proj(x, g, weights, g_bq, g_bkv, tables, *, seq_len, tm=512):
    m, d = x.shape
    w, wuq, wukv = weights
    tm = min(tm, m)
    table_rows = tables[0].shape[1]
    assert m % tm == 0 and table_rows % tm == 0 and (seq_len % tm == 0 or tm % seq_len == 0)
    tblocks = table_rows // tm
    row = lambda cols: pl.BlockSpec((tm, cols), lambda i: (i, 0))
    const = lambda a: pl.BlockSpec(a.shape, lambda i: (0,) * a.ndim)
    tspec = pl.BlockSpec((3, tm, LANES), lambda i: (0, i % tblocks, 0))
    g2, gq2, gkv2 = g.reshape(1, d), g_bq.reshape(1, Q_RANK), g_bkv.reshape(1, KV_RANK)
    kv_cols = A_KV_HEADS * HEAD_DIM
    outs = [(A_HEADS * LANES, BF16), (A_KV_HEADS * LANES, BF16), (A_KV_HEADS * LANES, BF16),
            (IDX_HEADS * IDX_DIM, BF16), (IDX_DIM, BF16), (IDX_HEADS, F32),
            (B_HEADS * LANES, BF16), (B_HEADS * LANES, BF16), (B_HEADS * LANES, BF16),
            (kv_cols, F32), (kv_cols, F32), (IDX_DIM, F32), (KV_RANK, F32), (B_ROPE, F32)]
    return pl.pallas_call(
        _even_proj_kernel,
        grid=(m // tm,),
        in_specs=[row(d), const(g2), const(w), const(gq2), const(wuq), const(gkv2), const(wukv),
                  tspec, tspec, tspec],
        out_specs=[row(c) for c, _ in outs],
        out_shape=[jax.ShapeDtypeStruct((m, c), t) for c, t in outs],
        compiler_params=_params("parallel"),
        name="even_proj",
    )(x, g2, w, gq2, wuq, gkv2, wukv, *tables)


_O_QC, _O_KC, _O_VC, _O_F, _O_QD, _O_KD, _O_VD, _O_END = 0, 512, 1024, 1536, 1664, 2176, 2688, 3200
FOX_BIAS_TERMS = 3


def _three_terms(x):
    hi = x.astype(BF16)
    r = x - hi.astype(F32)
    mid = r.astype(BF16)
    return hi, mid, (r - mid.astype(F32)).astype(BF16)


def _cumulative(logf, carry_s, c0_ref, restart):
    tm = logf.shape[0]

    @pl.when(restart)
    def _():
        carry_s[...] = jnp.broadcast_to(c0_ref[0], carry_s.shape)

    r = lax.broadcasted_iota(I32, (tm, tm), 0)
    cidx = lax.broadcasted_iota(I32, (tm, tm), 1)
    upto = jnp.where(cidx <= r, 1.0, 0.0).astype(BF16)
    c = carry_s[0:1, :] + sum(jnp.dot(upto, t, preferred_element_type=F32) for t in _three_terms(logf))
    carry_s[...] = jnp.broadcast_to(c[tm - 1:tm, :], carry_s.shape)
    return c


def _fox_key_bias(c):
    r = lax.broadcasted_iota(I32, (LANES, C_HEADS * LANES), 0)
    cidx = lax.broadcasted_iota(I32, (LANES, C_HEADS * LANES), 1)
    placed = 0.0
    for j, term in enumerate(_three_terms(-LOG2E * c)):
        put = jnp.where((cidx == r * LANES + HEAD_DIM + j) & (r < C_HEADS), 1.0, 0.0).astype(BF16)
        placed = placed + jnp.dot(term, put, preferred_element_type=F32)
    return placed


def _store_fox_keys(out_ref, kc, c):
    bias = _lane_tiles(_fox_key_bias(c))
    for p, pair in enumerate(_lane_tiles(kc)):
        for u, tile in enumerate(_split_pair(pair)):
            h = 2 * p + u
            out_ref[:, h * LANES:(h + 1) * LANES] = (tile + bias[h]).astype(out_ref.dtype)


def _odd_proj_kernel(x_ref, g_ref, w_ref, bf_ref, c0_ref,
                     qc_o, kc_o, vc_o, qd_o, kd_o, vd_o, kc_s, vc_s, kd_s, vd_s, logf_s, carry_s,
                     *, tiles_per_seq):
    h = _rmsnorm_rows(x_ref[...], g_ref[...]).astype(BF16)
    p = jnp.dot(h, w_ref[...], preferred_element_type=F32)
    scale = HEAD_DIM ** -0.5 * LOG2E
    ones_col = _one_hot_row(HEAD_DIM)

    f = p[:, _O_F:_O_QD] + bf_ref[...]
    logf = jnp.minimum(f, 0.0) - jnp.log1p(jnp.exp(-jnp.abs(f)))
    logf = jnp.where(_lane_index(logf.shape) < C_HEADS, logf, 0.0)
    logf_s[...] = logf[:, :C_HEADS]
    c = _cumulative(logf, carry_s, c0_ref, pl.program_id(0) % tiles_per_seq == 0)

    kc, vc, kd, vd = p[:, _O_KC:_O_VC], p[:, _O_VC:_O_F], p[:, _O_KD:_O_VD], p[:, _O_VD:_O_END]
    kc_s[...], vc_s[...], kd_s[...], vd_s[...] = kc, vc, kd, vd
    _store_heads(qc_o, p[:, _O_QC:_O_KC], scale=scale,
                 extra=_one_hot_row(*range(HEAD_DIM, HEAD_DIM + FOX_BIAS_TERMS)))
    _store_fox_keys(kc_o, kc, c)
    _store_heads(vc_o, vc, extra=ones_col)
    _store_heads(qd_o, p[:, _O_QD:_O_KD], scale=scale)
    _store_heads(kd_o, kd)
    _store_heads(vd_o, vd)


def _odd_weights(w_in):
    cuts = np.cumsum(ODD_SPLIT)[:-1].tolist()
    qc, kc, vc, fc, qd, kd, vd = jnp.split(w_in, cuts, axis=1)
    w = jnp.concatenate([qc, kc, vc, _pad_cols(fc, LANES), qd, kd, vd], axis=1)
    assert w.shape[1] == _O_END
    return w.astype(BF16)


def _odd_proj(x, g, w, b_f, c0, *, seq_len, tm=256):
    m, d = x.shape
    tm = min(tm, seq_len)
    assert seq_len % tm == 0
    tiles_per_seq = seq_len // tm
    row = lambda cols: pl.BlockSpec((tm, cols), lambda i: (i, 0))
    const = lambda a: pl.BlockSpec(a.shape, lambda i: (0,) * a.ndim)
    g2 = g.reshape(1, d)
    bf2 = _pad_cols(b_f.reshape(1, C_HEADS), LANES)
    wide, flat = C_HEADS * LANES, C_HEADS * HEAD_DIM
    outs = [(wide, BF16)] * 6 + [(flat, F32)] * 4 + [(C_HEADS, F32)]
    return pl.pallas_call(
        functools.partial(_odd_proj_kernel, tiles_per_seq=tiles_per_seq),
        grid=(m // tm,),
        in_specs=[row(d), const(g2), const(w), const(bf2),
                  pl.BlockSpec((1, 1, LANES), lambda i: (i // tiles_per_seq, 0, 0))],
        out_specs=[row(c) for c, _ in outs],
        out_shape=[jax.ShapeDtypeStruct((m, c), t) for c, t in outs],
        scratch_shapes=[pltpu.VMEM((8, LANES), F32)],
        compiler_params=_params("arbitrary"),
        name="odd_proj",
    )(x, g2, w, bf2, c0)


def _fox_past_kernel(k_ref, logf_ref, c0_ref, k_o, cend_o, carry_s, *, tiles_per_seq):
    logf = jnp.pad(logf_ref[...], ((0, 0), (0, LANES - C_HEADS)))
    c = _cumulative(logf, carry_s, c0_ref, pl.program_id(0) % tiles_per_seq == 0)
    _store_fox_keys(k_o, k_ref[...], c)
    cend_o[0] = c[c.shape[0] - 1:, :]


def _fox_past(k, logf, *, seq_len, tm=256):
    m = k.shape[0]
    nseq = m // seq_len
    tm = min(tm, seq_len)
    tiles_per_seq = seq_len // tm
    row = lambda cols: pl.BlockSpec((tm, cols), lambda i: (i, 0))
    per_seq = pl.BlockSpec((1, 1, LANES), lambda i: (i // tiles_per_seq, 0, 0))
    return pl.pallas_call(
        functools.partial(_fox_past_kernel, tiles_per_seq=tiles_per_seq),
        grid=(m // tm,),
        in_specs=[row(C_HEADS * HEAD_DIM), row(C_HEADS), per_seq],
        out_specs=[row(C_HEADS * LANES), per_seq],
        out_shape=[jax.ShapeDtypeStruct((m, C_HEADS * LANES), BF16),
                   jax.ShapeDtypeStruct((nseq, 1, LANES), F32)],
        scratch_shapes=[pltpu.VMEM((8, LANES), F32)],
        compiler_params=_params("arbitrary"),
        name="fox_past",
    )(k, logf, jnp.zeros((nseq, 1, LANES), F32))


def _head_lanes(parts, ones_col=False):
    bsz, s_len, heads = parts[0].shape[:3]
    parts = [p.astype(BF16) for p in parts]
    used = sum(p.shape[-1] for p in parts)
    if ones_col:
        parts.append(jnp.ones((bsz, s_len, heads, 1), BF16))
        used += 1
    parts.append(jnp.zeros((bsz, s_len, heads, LANES - used), BF16))
    return jnp.concatenate(parts, axis=-1).reshape(bsz, s_len, heads * LANES)


def _keys(past, new, sk_pad):
    rows = new if past is None else jnp.concatenate([past, new], axis=1)
    pad = sk_pad - rows.shape[1]
    return jnp.pad(rows, ((0, 0), (0, pad), (0, 0))) if pad else rows


def _tiles(s_len, past_len):
    tq = min(256, s_len)
    tk, group = (256, 4) if past_len == 0 else (128, 3)
    return tq, tk, group, _round_up(past_len + s_len, tk * group)


def _fine_group(group):
    return group // 2 if group % 2 == 0 else group


def _even_mixer(x, h_gain, tables, past, weights, g_bq, g_bkv, w_o):
    bsz, s_len, d = x.shape
    m = bsz * s_len
    past_len = 0 if past is None else past[0].shape[1]
    sk = past_len + s_len
    tq, tk, group, sk_pad = _tiles(s_len, past_len)
    outs = _even_proj(x.reshape(m, d), h_gain, weights, g_bq, g_bkv, tables, seq_len=s_len)
    qa, ka, va, qi, ki, wi, qb, kb, vb, ka_s, va_s, ki_s, lat_s, kr_s = [
        o.reshape(bsz, s_len, -1) for o in outs]
    new_rows = (ka_s.reshape(bsz, s_len, A_KV_HEADS, HEAD_DIM), va_s.reshape(bsz, s_len, A_KV_HEADS, HEAD_DIM),
                ki_s, lat_s, kr_s)
    if past is None:
        pa = (None,) * 5
    else:
        c_k, c_v, c_ki, c_lat, c_kr = past
        kb_past, vb_past = _mla_past(c_lat.reshape(bsz * past_len, KV_RANK),
                                     c_kr.reshape(bsz * past_len, B_ROPE), weights[2])
        pa = (_head_lanes([c_k]), _head_lanes([c_v], ones_col=True), c_ki.astype(BF16),
              kb_past.reshape(bsz, past_len, -1), vb_past.reshape(bsz, past_len, -1))

    out_a = _dsa_attention(qa, _keys(pa[0], ka, sk_pad), _keys(pa[1], va, sk_pad), qi, _keys(pa[2], ki, sk_pad),
                           wi, q_off=past_len, sk=sk, tq=tq, tk=tk, group=group)
    out_b = _softmax_attention(qb, _keys(pa[3], kb, sk_pad), _keys(pa[4], vb, sk_pad),
                               heads=B_HEADS, mask_kind="chunk", q_off=past_len, sk=sk, tq=tq, tk=tk,
                               group=_fine_group(group), name="mla_attention")
    y = _matmul_residual(out_a.reshape(m, -1), out_b.reshape(m, -1), w_o, x.reshape(m, d)).reshape(bsz, s_len, d)
    return y, new_rows


def _odd_mixer(x, h_gain, past, w, b_f, w_o):
    bsz, s_len, d = x.shape
    m = bsz * s_len
    past_len = 0 if past is None else past[0].shape[1]
    sk = past_len + s_len
    tq, tk, group, sk_pad = _tiles(s_len, past_len)
    if past is None:
        pa = (None,) * 4
        c0 = jnp.zeros((bsz, 1, LANES), F32)
    else:
        c_k, c_v, c_logf, d_k, d_v = past
        kc_past, c0 = _fox_past(c_k.reshape(bsz * past_len, -1), c_logf.reshape(bsz * past_len, C_HEADS),
                                seq_len=past_len)
        pa = (kc_past.reshape(bsz, past_len, -1), _head_lanes([c_v], ones_col=True),
              _head_lanes([d_k]), _head_lanes([d_v]))
    outs = _odd_proj(x.reshape(m, d), h_gain, w, b_f, c0, seq_len=s_len)
    qc, kc, vc, qd, kd, vd, kc_s, vc_s, kd_s, vd_s, logf = [o.reshape(bsz, s_len, -1) for o in outs]
    heads = lambda a: a.reshape(bsz, s_len, -1, HEAD_DIM)
    new_rows = (heads(kc_s), heads(vc_s), logf, heads(kd_s), heads(vd_s))

    out_c = _softmax_attention(qc, _keys(pa[0], kc, sk_pad), _keys(pa[1], vc, sk_pad),
                               heads=C_HEADS, mask_kind="causal", q_off=past_len, sk=sk, tq=tq, tk=tk,
                               group=_fine_group(group), name="fox_attention")
    out_d = _sb_attention(qd, _keys(pa[2], kd, sk_pad), _keys(pa[3], vd, sk_pad),
                          q_off=past_len, sk=sk, tq=tq, tk=tk)
    y = _matmul_residual(out_c.reshape(m, -1), out_d.reshape(m, -1), w_o, x.reshape(m, d)).reshape(bsz, s_len, d)
    return y, new_rows


def _final_norm_kernel(x_ref, g_ref, o_ref):
    x = x_ref[...]
    ms = jnp.mean(x * x, axis=-1, keepdims=True)
    o_ref[...] = x * lax.rsqrt(ms + EPS) * g_ref[...]


def _final_norm(x, g, *, tm=512):
    m, d = x.shape
    tm = min(tm, m)
    return pl.pallas_call(
        _final_norm_kernel,
        grid=(m // tm,),
        in_specs=[pl.BlockSpec((tm, d), lambda i: (i, 0)), pl.BlockSpec((1, d), lambda i: (0, 0))],
        out_specs=pl.BlockSpec((tm, d), lambda i: (i, 0)),
        out_shape=jax.ShapeDtypeStruct((m, d), F32),
        compiler_params=_params("parallel"),
        name="final_norm",
    )(x, g.reshape(1, d).astype(F32))


def _trunk(x, q_pos, caches, params, even_w, odd_w):
    (g_mix, g_ffn, g_final, w_in_even, g_b_q, g_b_kv, w_b_uq, w_b_ukv, w_o_even,
     w_in_odd, b_forget, w_o_odd, w_up, w_conv, b_conv, w_down) = params
    bsz, s_len, d = x.shape
    depth = g_mix.shape[0]
    even_rows = [[] for _ in range(5)]
    odd_rows = [[] for _ in range(5)]
    conv_rows = []
    rows = max(s_len, min(512, bsz * s_len))
    tables = (_rope_tables(q_pos, rows, period=HEAD_DIM, start=0, n_rot=ROT_DIM),
              _rope_tables(q_pos, rows, period=IDX_DIM, start=0, n_rot=IDX_ROT),
              _rope_tables(q_pos, rows, period=LANES, start=_B_ROPE_LANE, n_rot=B_ROPE))
    for l in range(depth):
        j = l // 2
        if l % 2 == 0:
            past = None if caches is None else tuple(c[j] for c in caches[0:5])
            x, rows = _even_mixer(x, g_mix[l], tables, past, even_w[j], g_b_q[j], g_b_kv[j], w_o_even[j])
            for lst, r in zip(even_rows, rows):
                lst.append(r)
        else:
            past = None if caches is None else tuple(c[j] for c in caches[5:10])
            x, rows = _odd_mixer(x, g_mix[l], past, odd_w[j], b_forget[j], w_o_odd[j])
            for lst, r in zip(odd_rows, rows):
                lst.append(r)
        state = jnp.zeros((bsz, CONV_W - 1, D_FF), F32) if caches is None else caches[10][l]
        y, new_buf = _conv_ffn(x.reshape(bsz * s_len, d), g_ffn[l], w_up[l], w_conv[l], b_conv[l], w_down[l],
                               state, seq_len=s_len, tm=1024)
        x = y.reshape(bsz, s_len, d)
        conv_rows.append(new_buf)
    out = _final_norm(x.reshape(bsz * s_len, d), g_final).reshape(bsz, s_len, d)
    states = [jnp.stack(r, axis=0) for r in even_rows + odd_rows] + [jnp.stack(conv_rows, axis=0)]
    return out, states


def kernel(x_prompt, x_sample, cache_a_k, cache_a_v, cache_a_idx_k, cache_b_latent, cache_b_rope,
           cache_c_k, cache_c_v, cache_c_logf, cache_d_k, cache_d_v, state_ffn_conv,
           g_mix, g_ffn, g_final, w_in_even, g_b_q, g_b_kv, w_b_uq, w_b_ukv, w_o_even,
           w_in_odd, b_forget, w_o_odd, w_up, w_conv, b_conv, w_down):
    params = (g_mix, g_ffn, g_final, w_in_even, g_b_q, g_b_kv, w_b_uq, w_b_ukv, w_o_even,
              w_in_odd, b_forget, w_o_odd, w_up, w_conv, b_conv, w_down)
    caches = (cache_a_k, cache_a_v, cache_a_idx_k, cache_b_latent, cache_b_rope,
              cache_c_k, cache_c_v, cache_c_logf, cache_d_k, cache_d_v, state_ffn_conv)
    past_len = cache_a_k.shape[2]
    pos_prompt = jnp.arange(x_prompt.shape[1], dtype=I32)
    pos_sample = past_len + jnp.arange(x_sample.shape[1], dtype=I32)
    even_w = [_even_weights(w_in_even[j], w_b_uq[j], w_b_ukv[j]) for j in range(w_in_even.shape[0])]
    odd_w = [_odd_weights(w_in_odd[j]) for j in range(w_in_odd.shape[0])]
    y_prompt, p_states = _trunk(x_prompt, pos_prompt, None, params, even_w, odd_w)
    y_sample, s_states = _trunk(x_sample, pos_sample, caches, params, even_w, odd_w)
    return (y_prompt, y_sample, *p_states, *s_states)
```

```python
import functools

import jax
import jax.numpy as jnp
import numpy as np
from jax import lax
from jax.experimental import pallas as pl
from jax.experimental.pallas import tpu as pltpu

F32 = jnp.float32
BF16 = jnp.bfloat16
I32 = jnp.int32

CHUNK = 64
ROPE_THETA = 500000.0
EPS = 1e-6
HEAD_DIM = 64
ROT_DIM = HEAD_DIM // 4
A_HEADS = 8
A_KV_HEADS = 2
IDX_HEADS = 8
IDX_DIM = 32
IDX_ROT = IDX_DIM // 4
TOPK_MAX = 256
B_HEADS = 8
Q_RANK = 256
KV_RANK = 128
B_NOPE = 64
B_ROPE = 32
B_VDIM = 64
C_HEADS = 8
D_HEADS = 8
D_FF = 2816
CONV_W = 3

EVEN_SPLIT = [A_HEADS * HEAD_DIM, A_KV_HEADS * HEAD_DIM, A_KV_HEADS * HEAD_DIM,
              IDX_HEADS * IDX_DIM, IDX_DIM, IDX_HEADS, Q_RANK, KV_RANK, B_ROPE]
ODD_SPLIT = [C_HEADS * HEAD_DIM] * 3 + [C_HEADS] + [D_HEADS * HEAD_DIM] * 3

LANES = 128
VMEM_LIMIT_BYTES = 56 * 1024 * 1024
MASKED = -1e30
INT_MIN = -2 ** 31
LOG2E = 1.4426950408889634
F32_EXP2_UNDERFLOW = -152.0

_NT = (((1,), (1,)), ((), ()))


def _params(*sem):
    return pltpu.CompilerParams(dimension_semantics=sem, vmem_limit_bytes=VMEM_LIMIT_BYTES)


def _round_up(n, m):
    return (n + m - 1) // m * m


def _chunk_of(pos):
    return jnp.right_shift(pos, CHUNK.bit_length() - 1)


def _proj_kernel(x_ref, g_ref, w_ref, *out_refs, norm, emit_h):
    x = x_ref[...]
    if norm:
        ms = jnp.mean(x * x, axis=-1, keepdims=True)
        x = x * lax.rsqrt(ms + EPS) * g_ref[...]
    out_refs[0][...] = jnp.dot(x.astype(BF16), w_ref[...], preferred_element_type=F32)
    if emit_h:
        out_refs[1][...] = x


def _proj(x, g, w, *, norm=True, emit_h=False, tm=512):
    m, k = x.shape
    n = w.shape[1]
    tm = min(tm, m)
    assert m % tm == 0
    out_shape = [jax.ShapeDtypeStruct((m, n), F32)]
    out_specs = [pl.BlockSpec((tm, n), lambda i: (i, 0))]
    if emit_h:
        out_shape.append(jax.ShapeDtypeStruct((m, k), F32))
        out_specs.append(pl.BlockSpec((tm, k), lambda i: (i, 0)))
    res = pl.pallas_call(
        functools.partial(_proj_kernel, norm=norm, emit_h=emit_h),
        grid=(m // tm,),
        in_specs=[pl.BlockSpec((tm, k), lambda i: (i, 0)),
                  pl.BlockSpec((1, k), lambda i: (0, 0)),
                  pl.BlockSpec((k, n), lambda i: (0, 0))],
        out_specs=out_specs,
        out_shape=out_shape,
        compiler_params=_params("parallel"),
        name="proj",
    )(x, g.reshape(1, k).astype(F32), w.astype(BF16))
    return res if emit_h else res[0]


def _matmul_residual_kernel(a_ref, b_ref, wa_ref, wb_ref, r_ref, o_ref):
    o_ref[...] = (r_ref[...] + jnp.dot(a_ref[...], wa_ref[...], preferred_element_type=F32)
                  + jnp.dot(b_ref[...], wb_ref[...], preferred_element_type=F32))


def _matmul_residual(a, b, w, res, *, tm=512):
    m, ka = a.shape
    kb = b.shape[1]
    n = w.shape[1]
    tm = min(tm, m)
    assert m % tm == 0 and w.shape[0] == ka + kb
    row = lambda cols: pl.BlockSpec((tm, cols), lambda i: (i, 0))
    const = lambda rows: pl.BlockSpec((rows, n), lambda i: (0, 0))
    w = w.astype(BF16)
    return pl.pallas_call(
        _matmul_residual_kernel,
        grid=(m // tm,),
        in_specs=[row(ka), row(kb), const(ka), const(kb), row(n)],
        out_specs=row(n),
        out_shape=jax.ShapeDtypeStruct((m, n), F32),
        compiler_params=_params("parallel"),
        name="matmul_residual",
    )(a, b, w[:ka], w[ka:], res)


def _conv_ffn_kernel(x_ref, g_ref, wg_ref, wu_ref, wc_ref, bc_ref, wd_ref, st_ref,
                     y_ref, ns_ref, h_s, acc_s, carry_s, *, tiles_per_seq, seqs, tm, tf):
    i = pl.program_id(0)
    c = pl.program_id(1)
    nc = pl.num_programs(1)
    cols = pl.ds(pl.multiple_of(c * tf, tf), tf)
    rows_per = tm // seqs

    @pl.when(c == 0)
    def _():
        x = x_ref[...]
        ms = jnp.mean(x * x, axis=-1, keepdims=True)
        h_s[...] = (x * lax.rsqrt(ms + EPS) * g_ref[...]).astype(BF16)
        acc_s[...] = jnp.zeros_like(acc_s)

    h = h_s[...]
    gate = jnp.dot(h, wg_ref[...], preferred_element_type=F32)
    up = jnp.dot(h, wu_ref[...], preferred_element_type=F32)

    if seqs == 1:
        @pl.when(i % tiles_per_seq == 0)
        def _():
            carry_s[c] = st_ref[0, :, cols]
        before = [carry_s[c]]
    else:
        before = [st_ref[s, :, cols] for s in range(seqs)]
    row = lax.broadcasted_iota(I32, gate.shape, 0)
    g1 = pltpu.roll(gate, 1, 0)
    g2 = pltpu.roll(gate, 2, 0)
    for s, prev in enumerate(before):
        first = s * rows_per
        g1 = jnp.where(row == first, prev[1:2, :], g1)
        g2 = jnp.where(row == first, prev[0:1, :], jnp.where(row == first + 1, prev[1:2, :], g2))
    wc = wc_ref[...]
    gc = bc_ref[...] + g2 * wc[0:1, :]
    gc = gc + g1 * wc[1:2, :]
    gc = gc + gate * wc[2:3, :]
    act = gc * jax.nn.sigmoid(gc) * up
    acc_s[...] += jnp.dot(act.astype(BF16), wd_ref[...], preferred_element_type=F32)

    for s in range(seqs):
        ns_ref[s, :, cols] = gate[(s + 1) * rows_per - 2:(s + 1) * rows_per, :]
    if seqs == 1:
        carry_s[c] = gate[tm - 2:tm, :]

    @pl.when(c == nc - 1)
    def _():
        y_ref[...] = x_ref[...] + acc_s[...]


def _conv_ffn(x, g, w_up, w_conv, b_conv, w_down, state, *, seq_len, tm, tf=256):
    m, d = x.shape
    nseq = m // seq_len
    tm = min(tm, m)
    assert m % tm == 0 and D_FF % tf == 0 and (seq_len % tm == 0 or tm % seq_len == 0)
    tiles_per_seq = max(1, seq_len // tm)
    seqs = max(1, tm // seq_len)
    nc = D_FF // tf
    wg = w_up[:, :D_FF].astype(BF16)
    wu = w_up[:, D_FF:].astype(BF16)
    state_spec = pl.BlockSpec((seqs, CONV_W - 1, D_FF), lambda i, c: (i // tiles_per_seq, 0, 0))
    y, ns = pl.pallas_call(
        functools.partial(_conv_ffn_kernel, tiles_per_seq=tiles_per_seq, seqs=seqs, tm=tm, tf=tf),
        grid=(m // tm, nc),
        in_specs=[pl.BlockSpec((tm, d), lambda i, c: (i, 0)),
                  pl.BlockSpec((1, d), lambda i, c: (0, 0)),
                  pl.BlockSpec((d, tf), lambda i, c: (0, c)),
                  pl.BlockSpec((d, tf), lambda i, c: (0, c)),
                  pl.BlockSpec((CONV_W, tf), lambda i, c: (0, c)),
                  pl.BlockSpec((1, tf), lambda i, c: (0, c)),
                  pl.BlockSpec((tf, d), lambda i, c: (c, 0)),
                  state_spec],
        out_specs=[pl.BlockSpec((tm, d), lambda i, c: (i, 0)), state_spec],
        out_shape=[jax.ShapeDtypeStruct((m, d), F32),
                   jax.ShapeDtypeStruct((nseq, CONV_W - 1, D_FF), F32)],
        scratch_shapes=[pltpu.VMEM((tm, d), BF16),
                        pltpu.VMEM((tm, d), F32),
                        pltpu.VMEM((nc, CONV_W - 1, tf), F32)],
        compiler_params=_params("arbitrary", "arbitrary"),
        name="conv_ffn",
    )(x, g.reshape(1, d).astype(F32), wg, wu, w_conv.astype(F32), b_conv.reshape(1, D_FF).astype(F32),
      w_down.astype(BF16), state.astype(F32))
    return y, ns


def _attn_call(kernel, inputs, in_specs, out_cols, *, bsz, sq, tq, scratch=(), name):
    return pl.pallas_call(
        kernel,
        grid=(bsz, sq // tq),
        in_specs=in_specs,
        out_specs=pl.BlockSpec((1, tq, out_cols), lambda b, i: (b, i, 0)),
        out_shape=jax.ShapeDtypeStruct((bsz, sq, out_cols), BF16),
        scratch_shapes=list(scratch),
        compiler_params=_params("parallel", "arbitrary"),
        name=name,
    )(*inputs)


def _q_spec(tq, cols):
    return pl.BlockSpec((1, tq, cols), lambda b, i: (b, i, 0))


def _k_spec(rows, cols):
    return pl.BlockSpec((1, rows, cols), lambda b, i: (b, 0, 0))


def _tile_geometry(i, *, tq, span, q_off):
    q_lo = q_off + i * tq
    return q_lo, q_lo // span


def _lane_tiles(x):
    return [x[:, j * LANES:(j + 1) * LANES] for j in range(x.shape[1] // LANES)]


def _visible(kind, q_lo, ks, *, tq, tk, sk):
    q_pos = q_lo + lax.broadcasted_iota(I32, (tq, 1), 0)
    k_pos = ks + lax.broadcasted_iota(I32, (1, tk), 1)
    if kind == "chunk":
        return (_chunk_of(k_pos) <= _chunk_of(q_pos)) & (k_pos < sk)
    if kind == "causal":
        return k_pos <= q_pos
    assert kind == "strict"
    return k_pos < q_pos


HEADS_PER_STEP = 2


def _span_cases(i, body, *, tq, span, q_off, n_tiles):
    g_own = (q_off + i * tq) // span
    for g in sorted({(q_off + t * tq) // span for t in range(n_tiles)}):
        pl.when(g_own == g)(functools.partial(body, g + 1))


def _softmax_heads(qs, k_ref, v_ref, s_scr, kcols, vcols, *, tq, tk, blocks, own_from, own_mask,
                   bias_ref=None):
    row_max = []
    for slot, (q, kcol) in enumerate(zip(qs, kcols)):
        mx = jnp.full((tq, LANES), MASKED, F32)
        for b in range(blocks):
            ks = b * tk
            s = lax.dot_general(q, k_ref[0, ks:ks + tk, kcol], _NT, preferred_element_type=F32)
            if bias_ref is not None:
                s = s + bias_ref[:, ks:ks + tk]
            if own_mask is not None and b >= own_from:
                s = jnp.where(own_mask(ks), s, MASKED)
            s_scr[slot, :, ks:ks + tk] = s
            for t in _lane_tiles(s):
                mx = jnp.maximum(mx, t)
        row_max.append(jnp.broadcast_to(jnp.max(mx, axis=1, keepdims=True), (tq, LANES)))
    outs = []
    for slot, (m, vcol) in enumerate(zip(row_max, vcols)):
        acc = jnp.zeros((tq, LANES), F32)
        for b in range(blocks):
            ks = b * tk
            p = jnp.concatenate([jnp.exp2(t - m) for t in _lane_tiles(s_scr[slot, :, ks:ks + tk])], axis=1)
            acc = acc + jnp.dot(p.astype(BF16), v_ref[0, ks:ks + tk, vcol], preferred_element_type=F32)
        outs.append(acc[:, :HEAD_DIM] / acc[:, HEAD_DIM:HEAD_DIM + 1])
    return outs


def _lane_tile_at(index):
    return pl.ds(pl.multiple_of(index * LANES, LANES), LANES)


def _softmax_attn_kernel(q_ref, k_ref, v_ref, o_ref, s_scr, *, heads, mask_kind, tq, tk, group, q_off, sk,
                         n_tiles):
    i = pl.program_id(1)
    own_mask = functools.partial(_visible, mask_kind, q_off + i * tq, tq=tq, tk=tk, sk=sk)

    def attend(n_spans):
        def step(hp, _):
            cols = [_lane_tile_at(hp * HEADS_PER_STEP + u) for u in range(HEADS_PER_STEP)]
            outs = _softmax_heads([q_ref[0, :, c] for c in cols], k_ref, v_ref, s_scr, cols, cols,
                                  tq=tq, tk=tk, blocks=n_spans * group, own_from=(n_spans - 1) * group,
                                  own_mask=own_mask)
            o_ref[0, :, _lane_tile_at(hp)] = jnp.concatenate(outs, axis=1).astype(o_ref.dtype)
            return 0
        lax.fori_loop(0, heads // HEADS_PER_STEP, step, 0)

    _span_cases(i, attend, tq=tq, span=tk * group, q_off=q_off, n_tiles=n_tiles)


def _check_tiling(sk_pad, *, tq, tk, group, q_off):
    assert (tk * group) % tq == 0 and q_off % tq == 0 and sk_pad % (tk * group) == 0 and tk % LANES == 0
    assert HEADS_PER_STEP * HEAD_DIM == LANES


def _softmax_attention(q, k, v, *, heads, mask_kind, q_off, sk, tq, tk, group, name):
    bsz, sq, _ = q.shape
    sk_pad = k.shape[1]
    _check_tiling(sk_pad, tq=tq, tk=tk, group=group, q_off=q_off)
    kern = functools.partial(_softmax_attn_kernel, heads=heads, mask_kind=mask_kind,
                             tq=tq, tk=tk, group=group, q_off=q_off, sk=sk, n_tiles=sq // tq)
    cols = heads * LANES
    return _attn_call(kern, (q, k, v), [_q_spec(tq, cols), _k_spec(sk_pad, cols), _k_spec(sk_pad, cols)],
                      heads * HEAD_DIM, bsz=bsz, sq=sq, tq=tq,
                      scratch=(pltpu.VMEM((HEADS_PER_STEP, tq, sk_pad), F32),), name=name)


def _split_bf16(x):
    hi = x.astype(BF16)
    lo = (x - hi.astype(F32)).astype(BF16)
    return hi, lo


SB_HEADS_PER_LOOP = 2


def _sb_kernel(q_ref, k_ref, v_ref, o_ref, *, tq, tk, q_off, sk):
    q_lo, n_before = _tile_geometry(pl.program_id(1), tq=tq, span=tk, q_off=q_off)
    r = lax.broadcasted_iota(I32, (tk, tk), 0)
    cidx = lax.broadcasted_iota(I32, (tk, tk), 1)
    later = jnp.where(r > cidx, 1.0, 0.0).astype(BF16)

    def block(q, h, kb, run, acc, mask):
        ks = pl.multiple_of(kb * tk, tk)
        k = k_ref[0, pl.ds(ks, tk), h * LANES:(h + 1) * LANES]
        v = v_ref[0, pl.ds(ks, tk), h * LANES:(h + 1) * LANES]
        z = lax.dot_general(q, k, _NT, preferred_element_type=F32)
        log_beta = jnp.minimum(z, 0.0) - jnp.log2(1.0 + jnp.exp2(-jnp.abs(z)))
        log_1m = log_beta - z
        if mask is not None:
            log_1m = jnp.where(mask, log_1m, 0.0)
        hi, lo = _split_bf16(log_1m)
        after = (jnp.dot(hi, later, preferred_element_type=F32)
                 + jnp.dot(lo, later, preferred_element_type=F32))
        a = jnp.exp2(log_beta + after + run)
        if mask is not None:
            a = jnp.where(mask, a, 0.0)
        acc = acc + jnp.dot(a.astype(BF16), v, preferred_element_type=F32)
        run = run + after[:, 0:1] + log_1m[:, 0:1]
        return run, acc

    own = _visible("strict", q_lo, n_before * tk, tq=tq, tk=tk, sk=sk)
    for h0 in range(0, D_HEADS, SB_HEADS_PER_LOOP):
        hs = range(h0, h0 + SB_HEADS_PER_LOOP)
        qs = [q_ref[0, :, h * LANES:(h + 1) * LANES] for h in hs]
        state = []
        for h, q in zip(hs, qs):
            state.extend(block(q, h, n_before, jnp.zeros((tq, 1), F32), jnp.zeros((tq, LANES), F32), own))

        def alive(state):
            top = functools.reduce(jnp.maximum, state[0::2])
            return jnp.max(top) > F32_EXP2_UNDERFLOW

        def cond(c):
            return (c[0] < n_before) & c[1]

        def body(c, hs=hs, qs=qs):
            j, _, state = c
            new = []
            for n, (h, q) in enumerate(zip(hs, qs)):
                new.extend(block(q, h, n_before - 1 - j, state[2 * n], state[2 * n + 1], None))
            return j + 1, alive(new), tuple(new)

        _, _, state = lax.while_loop(cond, body, (jnp.int32(0), alive(state), tuple(state)))
        for n, h in enumerate(hs):
            o_ref[0, :, h * HEAD_DIM:(h + 1) * HEAD_DIM] = state[2 * n + 1][:, :HEAD_DIM].astype(o_ref.dtype)


def _sb_attention(q, k, v, *, q_off, sk, tq, tk):
    bsz, sq, _ = q.shape
    sk_pad = k.shape[1]
    _check_tiling(sk_pad, tq=tq, tk=tk, group=1, q_off=q_off)
    kern = functools.partial(_sb_kernel, tq=tq, tk=tk, q_off=q_off, sk=sk)
    cols = D_HEADS * LANES
    return _attn_call(kern, (q, k, v), [_q_spec(tq, cols), _k_spec(sk_pad, cols), _k_spec(sk_pad, cols)],
                      D_HEADS * HEAD_DIM, bsz=bsz, sq=sq, tq=tq, name="sb_attention")


def _sortable_key(score):
    bits = lax.bitcast_convert_type(score, I32)
    return jnp.where(bits < 0, bits ^ 0x7FFFFFFF, bits)


KEY_OF_NEG_INF = -0x7F800001


def _dsa_kernel(q_ref, k_ref, v_ref, qi_ref, ki_ref, wi_ref, o_ref, key_s, bias_s, s_scr, w_s,
                *, tq, tk, group, q_off, sk, topk, n_tiles):
    i = pl.program_id(1)
    nt = tk // LANES
    own_visible = functools.partial(_visible, "chunk", q_off + i * tq, tq=tq, tk=tk, sk=sk)
    kf = float(topk)

    for h in range(IDX_HEADS):
        w_s[h] = jnp.broadcast_to(wi_ref[0, :, h:h + 1], (tq, LANES))

    def select_and_attend(n_spans):
        blocks, own_from = n_spans * group, (n_spans - 1) * group
        starts = [b * tk for b in range(blocks)]

        def own(ks, value, masked):
            return jnp.where(own_visible(ks), value, masked) if ks >= own_from * tk else value

        qis = [qi_ref[0, :, h * IDX_DIM:(h + 1) * IDX_DIM] for h in range(IDX_HEADS)]
        for ks in starts:
            ki = ki_ref[0, ks:ks + tk, :]
            total = jnp.zeros((tq, tk), F32)
            for h in range(IDX_HEADS):
                sc = lax.dot_general(qis[h], ki, _NT, preferred_element_type=F32)
                total = total + jnp.maximum(sc, 0.0) * jnp.concatenate([w_s[h]] * nt, axis=1)
            key_s[:, ks:ks + tk] = _sortable_key(own(ks, total, -jnp.inf))

        def count(hit):
            cnt = jnp.zeros((tq, LANES), F32)
            for ks in starts:
                for t in _lane_tiles(key_s[:, ks:ks + tk]):
                    cnt = cnt + jnp.where(hit(t), 1.0, 0.0)
            return jnp.sum(cnt, axis=1, keepdims=True)

        def count_ge(cand):
            cand = jnp.broadcast_to(cand, (tq, LANES))
            return count(lambda t: t >= cand)

        lowest = jnp.full((tq, 1), INT_MIN, I32)
        zero = jnp.zeros((tq, 1), I32)
        n_zero = count_ge(zero)
        ok = n_zero >= kf
        thr = jnp.where(ok, zero, lowest)
        n_thr = jnp.where(ok, n_zero, float(blocks * tk))

        def bit_step(it, c):
            thr, n_thr = c
            cand = thr | jnp.left_shift(jnp.int32(1), 30 - it)
            n_cand = count_ge(cand)
            ok = n_cand >= kf
            return jnp.where(ok, cand, thr), jnp.where(ok, n_cand, n_thr)

        thr, n_thr = lax.fori_loop(0, 31, bit_step, (thr, n_thr))
        thr_b = jnp.broadcast_to(thr, (tq, LANES))

        tied = jnp.max(jnp.where(thr > KEY_OF_NEG_INF, n_thr, 0.0)) > kf

        @pl.when(jnp.logical_not(tied))
        def _():
            for ks in starts:
                tiles = _lane_tiles(key_s[:, ks:ks + tk])
                bias = jnp.concatenate([jnp.where(t >= thr_b, 0.0, MASKED) for t in tiles], axis=1)
                bias_s[:, ks:ks + tk] = own(ks, bias, MASKED)

        @pl.when(tied)
        def _():
            r = lax.broadcasted_iota(I32, (tk, tk), 0)
            cidx = lax.broadcasted_iota(I32, (tk, tk), 1)
            upto = jnp.where(r <= cidx, 1.0, 0.0).astype(BF16)
            need = kf - count(lambda t: t > thr_b)
            seen = jnp.zeros((tq, 1), F32)
            for ks in starts:
                key = key_s[:, ks:ks + tk]
                tie = jnp.where(key == thr, 1.0, 0.0)
                rank = seen + jnp.dot(tie.astype(BF16), upto, preferred_element_type=F32)
                sel = (key > thr) | ((key == thr) & (rank <= need))
                bias_s[:, ks:ks + tk] = own(ks, jnp.where(sel, 0.0, MASKED), MASKED)
                seen = seen + jnp.sum(tie, axis=1, keepdims=True)

        def step(hp, _):
            heads = [hp * HEADS_PER_STEP + u for u in range(HEADS_PER_STEP)]
            kv = [_lane_tile_at(hd // (A_HEADS // A_KV_HEADS)) for hd in heads]
            outs = _softmax_heads([q_ref[0, :, _lane_tile_at(hd)] for hd in heads], k_ref, v_ref, s_scr,
                                  kv, kv, tq=tq, tk=tk, blocks=blocks, own_from=own_from, own_mask=None,
                                  bias_ref=bias_s)
            o_ref[0, :, _lane_tile_at(hp)] = jnp.concatenate(outs, axis=1).astype(o_ref.dtype)
            return 0
        lax.fori_loop(0, A_HEADS // HEADS_PER_STEP, step, 0)

    _span_cases(i, select_and_attend, tq=tq, span=tk * group, q_off=q_off, n_tiles=n_tiles)


def _dsa_attention(q, k, v, qi, ki, wi, *, q_off, sk, tq, tk, group):
    bsz, sq, _ = q.shape
    sk_pad = k.shape[1]
    _check_tiling(sk_pad, tq=tq, tk=tk, group=group, q_off=q_off)
    topk = min(TOPK_MAX, sk // 4)
    kern = functools.partial(_dsa_kernel, tq=tq, tk=tk, group=group, q_off=q_off, sk=sk, topk=topk,
                             n_tiles=sq // tq)
    kv_cols = A_KV_HEADS * LANES
    return _attn_call(kern, (q, k, v, qi, ki, wi),
                      [_q_spec(tq, A_HEADS * LANES), _k_spec(sk_pad, kv_cols), _k_spec(sk_pad, kv_cols),
                       _q_spec(tq, IDX_HEADS * IDX_DIM), _k_spec(sk_pad, IDX_DIM), _q_spec(tq, IDX_HEADS)],
                      A_HEADS * HEAD_DIM, bsz=bsz, sq=sq, tq=tq,
                      scratch=(pltpu.VMEM((tq, sk_pad), I32), pltpu.VMEM((tq, sk_pad), F32),
                               pltpu.VMEM((HEADS_PER_STEP, tq, sk_pad), F32),
                               pltpu.VMEM((IDX_HEADS, tq, LANES), F32)),
                      name="dsa_attention")


def _rmsnorm_rows(x, g):
    ms = jnp.mean(x * x, axis=-1, keepdims=True)
    return x * lax.rsqrt(ms + EPS) * g


def _lane_index(shape):
    return lax.broadcasted_iota(I32, shape, len(shape) - 1)


def _one_hot_row(*lanes):
    lane = _lane_index((1, LANES))
    hit = functools.reduce(jnp.logical_or, [lane == l for l in lanes])
    return jnp.where(hit, 1.0, 0.0)


def _rotate(tile, tables, half):
    cos, sin_up, sin_dn = tables
    return tile * cos + pltpu.roll(tile, half, 1) * sin_up + pltpu.roll(tile, LANES - half, 1) * sin_dn


def _split_pair(pair):
    low = _lane_index(pair.shape) < HEAD_DIM
    return jnp.where(low, pair, 0.0), jnp.where(low, pltpu.roll(pair, HEAD_DIM, 1), 0.0)


def _store_heads(out_ref, compact, *, scale=None, extra=None):
    for p, pair in enumerate(_lane_tiles(compact)):
        for u, tile in enumerate(_split_pair(pair)):
            if scale is not None:
                tile = tile * scale
            if extra is not None:
                tile = tile + extra
            h = 2 * p + u
            out_ref[:, h * LANES:(h + 1) * LANES] = tile.astype(out_ref.dtype)


_E_QA, _E_KA, _E_VA, _E_QI, _E_KI, _E_WI, _E_CQ, _E_CKV, _E_KR, _E_END = (
    0, 512, 640, 768, 1024, 1152, 1280, 1536, 1664, 1792)
_B_ROPE_LANE = B_NOPE


def _even_proj_kernel(x_ref, g_ref, w_ref, gq_ref, wuq_ref, gkv_ref, wukv_ref, ta_ref, ti_ref, tb_ref,
                      qa_o, ka_o, va_o, qi_o, ki_o, wi_o, qb_o, kb_o, vb_o,
                      ka_s, va_s, ki_s, lat_s, kr_s):
    h = _rmsnorm_rows(x_ref[...], g_ref[...]).astype(BF16)
    p = jnp.dot(h, w_ref[...], preferred_element_type=F32)
    ta = (ta_ref[0], ta_ref[1], ta_ref[2])
    ti = (ti_ref[0], ti_ref[1], ti_ref[2])
    tb = (tb_ref[0], tb_ref[1], tb_ref[2])
    ones_col = _one_hot_row(HEAD_DIM)

    qa = jnp.concatenate([_rotate(t, ta, ROT_DIM // 2) for t in _lane_tiles(p[:, _E_QA:_E_KA])], axis=1)
    _store_heads(qa_o, qa, scale=HEAD_DIM ** -0.5 * LOG2E)
    ka = _rotate(p[:, _E_KA:_E_VA], ta, ROT_DIM // 2)
    ka_s[...] = ka
    _store_heads(ka_o, ka)
    va = p[:, _E_VA:_E_QI]
    va_s[...] = va
    _store_heads(va_o, va, extra=ones_col)
    qi = jnp.concatenate([_rotate(t, ti, IDX_ROT // 2) for t in _lane_tiles(p[:, _E_QI:_E_KI])], axis=1)
    qi_o[...] = (qi * IDX_DIM ** -0.5).astype(qi_o.dtype)
    ki = _rotate(p[:, _E_KI:_E_WI], ti, IDX_ROT // 2)[:, :IDX_DIM]
    ki_s[...] = ki
    ki_o[...] = ki.astype(ki_o.dtype)
    wi_o[...] = p[:, _E_WI:_E_WI + IDX_HEADS] * IDX_HEADS ** -0.5

    cq = _rmsnorm_rows(p[:, _E_CQ:_E_CKV], gq_ref[...]).astype(BF16)
    qb = jnp.dot(cq, wuq_ref[...], preferred_element_type=F32)
    scale_b = (B_NOPE + B_ROPE) ** -0.5 * LOG2E
    for hd, t in enumerate(_lane_tiles(qb)):
        qb_o[:, hd * LANES:(hd + 1) * LANES] = (_rotate(t, tb, B_ROPE // 2) * scale_b).astype(qb_o.dtype)
    lat = _rmsnorm_rows(p[:, _E_CKV:_E_KR], gkv_ref[...])
    lat_s[...] = lat
    kv = jnp.dot(lat.astype(BF16), wukv_ref[...], preferred_element_type=F32)
    kr = _rotate(p[:, _E_KR:_E_END], tb, B_ROPE // 2)
    kr_s[...] = kr[:, _B_ROPE_LANE:_B_ROPE_LANE + B_ROPE]
    _store_mla_keys(kb_o, vb_o, kv, kr)


def _store_mla_keys(kb_o, vb_o, kv, kr):
    ones_col = _one_hot_row(HEAD_DIM)
    tiles = _lane_tiles(kv)
    for hd in range(B_HEADS):
        kb_o[:, hd * LANES:(hd + 1) * LANES] = (tiles[hd] + kr).astype(kb_o.dtype)
        vb_o[:, hd * LANES:(hd + 1) * LANES] = (tiles[B_HEADS + hd] + ones_col).astype(vb_o.dtype)


def _mla_past_kernel(lat_ref, kr_ref, wukv_ref, kb_o, vb_o):
    kv = jnp.dot(lat_ref[...].astype(BF16), wukv_ref[...], preferred_element_type=F32)
    rows = kv.shape[0]
    kr = jnp.concatenate([jnp.zeros((rows, _B_ROPE_LANE), F32), kr_ref[...],
                          jnp.zeros((rows, LANES - _B_ROPE_LANE - B_ROPE), F32)], axis=1)
    _store_mla_keys(kb_o, vb_o, kv, kr)


def _mla_past(lat, kr, wukv, *, tm=512):
    m = lat.shape[0]
    tm = min(tm, m)
    assert m % tm == 0
    row = lambda cols: pl.BlockSpec((tm, cols), lambda i: (i, 0))
    wide = B_HEADS * LANES
    return pl.pallas_call(
        _mla_past_kernel,
        grid=(m // tm,),
        in_specs=[row(KV_RANK), row(B_ROPE), pl.BlockSpec(wukv.shape, lambda i: (0, 0))],
        out_specs=[row(wide), row(wide)],
        out_shape=[jax.ShapeDtypeStruct((m, wide), BF16)] * 2,
        compiler_params=_params("parallel"),
        name="mla_past",
    )(lat, kr, wukv)


def _pad_cols(w, width):
    return jnp.pad(w, ((0, 0), (0, width - w.shape[1])))


def _even_weights(w_in, w_uq, w_ukv):
    cuts = np.cumsum(EVEN_SPLIT)[:-1].tolist()
    qa, ka, va, qi, ki, wi, cq, ckv, kr = jnp.split(w_in, cuts, axis=1)
    kr = jnp.pad(kr, ((0, 0), (_B_ROPE_LANE, LANES - _B_ROPE_LANE - B_ROPE)))
    w = jnp.concatenate([qa, ka, va, qi, _pad_cols(ki, LANES), _pad_cols(wi, LANES), cq, ckv, kr], axis=1)
    d = w_in.shape[0]
    wuq = jnp.pad(w_uq.reshape(Q_RANK, B_HEADS, B_NOPE + B_ROPE),
                  ((0, 0), (0, 0), (0, LANES - B_NOPE - B_ROPE))).reshape(Q_RANK, B_HEADS * LANES)
    kvw = w_ukv.reshape(KV_RANK, B_HEADS, B_NOPE + B_VDIM)
    pad = lambda a: jnp.pad(a, ((0, 0), (0, 0), (0, LANES - a.shape[2]))).reshape(KV_RANK, B_HEADS * LANES)
    wukv = jnp.concatenate([pad(kvw[:, :, :B_NOPE]), pad(kvw[:, :, B_NOPE:])], axis=1)
    assert w.shape == (d, _E_END)
    return w.astype(BF16), wuq.astype(BF16), wukv.astype(BF16)


def _rope_tables(pos, rows, *, period, start, n_rot):
    half = n_rot // 2
    inv = ROPE_THETA ** (-jnp.arange(half, dtype=F32) * 2.0 / n_rot)
    ang = pos.astype(F32)[:, None] * inv[None, :]
    cos, sin = jnp.cos(ang), jnp.sin(ang)
    off = np.arange(LANES) % period - start
    idx = np.where((off >= 0) & (off < n_rot), off % half, 0)
    lower = (off >= 0) & (off < half)
    upper = (off >= half) & (off < n_rot)
    cos_t = jnp.where(lower | upper, cos[:, idx], 1.0)
    sin_up = jnp.where(upper, sin[:, idx], 0.0)
    sin_dn = jnp.where(lower, -sin[:, idx], 0.0)
    tabs = jnp.stack([cos_t, sin_up, sin_dn])
    return jnp.tile(tabs, (1, rows // pos.shape[0], 1))


def _even_proj(x, g, weights, g_bq, g_bkv, tables, *, seq_len, tm=512):
    m, d = x.shape
    w, wuq, wukv = weights
    tm = min(tm, m)
    table_rows = tables[0].shape[1]
    assert m % tm == 0 and table_rows % tm == 0 and (seq_len % tm == 0 or tm % seq_len == 0)
    tblocks = table_rows // tm
    row = lambda cols: pl.BlockSpec((tm, cols), lambda i: (i, 0))
    const = lambda a: pl.BlockSpec(a.shape, lambda i: (0,) * a.ndim)
    tspec = pl.BlockSpec((3, tm, LANES), lambda i: (0, i % tblocks, 0))
    g2, gq2, gkv2 = g.reshape(1, d), g_bq.reshape(1, Q_RANK), g_bkv.reshape(1, KV_RANK)
    kv_cols = A_KV_HEADS * HEAD_DIM
    outs = [(A_HEADS * LANES, BF16), (A_KV_HEADS * LANES, BF16), (A_KV_HEADS * LANES, BF16),
            (IDX_HEADS * IDX_DIM, BF16), (IDX_DIM, BF16), (IDX_HEADS, F32),
            (B_HEADS * LANES, BF16), (B_HEADS * LANES, BF16), (B_HEADS * LANES, BF16),
            (kv_cols, F32), (kv_cols, F32), (IDX_DIM, F32), (KV_RANK, F32), (B_ROPE, F32)]
    return pl.pallas_call(
        _even_proj_kernel,
        grid=(m // tm,),
        in_specs=[row(d), const(g2), const(w), const(gq2), const(wuq), const(gkv2), const(wukv),
                  tspec, tspec, tspec],
        out_specs=[row(c) for c, _ in outs],
        out_shape=[jax.ShapeDtypeStruct((m, c), t) for c, t in outs],
        compiler_params=_params("parallel"),
        name="even_proj",
    )(x, g2, w, gq2, wuq, gkv2, wukv, *tables)


_O_QC, _O_KC, _O_VC, _O_F, _O_QD, _O_KD, _O_VD, _O_END = 0, 512, 1024, 1536, 1664, 2176, 2688, 3200
FOX_BIAS_TERMS = 3


def _three_terms(x):
    hi = x.astype(BF16)
    r = x - hi.astype(F32)
    mid = r.astype(BF16)
    return hi, mid, (r - mid.astype(F32)).astype(BF16)


def _cumulative(logf, carry_s, c0_ref, restart):
    tm = logf.shape[0]

    @pl.when(restart)
    def _():
        carry_s[...] = jnp.broadcast_to(c0_ref[0], carry_s.shape)

    r = lax.broadcasted_iota(I32, (tm, tm), 0)
    cidx = lax.broadcasted_iota(I32, (tm, tm), 1)
    upto = jnp.where(cidx <= r, 1.0, 0.0).astype(BF16)
    c = carry_s[0:1, :] + sum(jnp.dot(upto, t, preferred_element_type=F32) for t in _three_terms(logf))
    carry_s[...] = jnp.broadcast_to(c[tm - 1:tm, :], carry_s.shape)
    return c


def _fox_key_bias(c):
    r = lax.broadcasted_iota(I32, (LANES, C_HEADS * LANES), 0)
    cidx = lax.broadcasted_iota(I32, (LANES, C_HEADS * LANES), 1)
    placed = 0.0
    for j, term in enumerate(_three_terms(-LOG2E * c)):
        put = jnp.where((cidx == r * LANES + HEAD_DIM + j) & (r < C_HEADS), 1.0, 0.0).astype(BF16)
        placed = placed + jnp.dot(term, put, preferred_element_type=F32)
    return placed


def _store_fox_keys(out_ref, kc, c):
    bias = _lane_tiles(_fox_key_bias(c))
    for p, pair in enumerate(_lane_tiles(kc)):
        for u, tile in enumerate(_split_pair(pair)):
            h = 2 * p + u
            out_ref[:, h * LANES:(h + 1) * LANES] = (tile + bias[h]).astype(out_ref.dtype)


def _odd_proj_kernel(x_ref, g_ref, w_ref, bf_ref, c0_ref,
                     qc_o, kc_o, vc_o, qd_o, kd_o, vd_o, kc_s, vc_s, kd_s, vd_s, logf_s, carry_s,
                     *, tiles_per_seq):
    h = _rmsnorm_rows(x_ref[...], g_ref[...]).astype(BF16)
    p = jnp.dot(h, w_ref[...], preferred_element_type=F32)
    scale = HEAD_DIM ** -0.5 * LOG2E
    ones_col = _one_hot_row(HEAD_DIM)

    f = p[:, _O_F:_O_QD] + bf_ref[...]
    logf = jnp.minimum(f, 0.0) - jnp.log1p(jnp.exp(-jnp.abs(f)))
    logf = jnp.where(_lane_index(logf.shape) < C_HEADS, logf, 0.0)
    logf_s[...] = logf[:, :C_HEADS]
    c = _cumulative(logf, carry_s, c0_ref, pl.program_id(0) % tiles_per_seq == 0)

    kc, vc, kd, vd = p[:, _O_KC:_O_VC], p[:, _O_VC:_O_F], p[:, _O_KD:_O_VD], p[:, _O_VD:_O_END]
    kc_s[...], vc_s[...], kd_s[...], vd_s[...] = kc, vc, kd, vd
    _store_heads(qc_o, p[:, _O_QC:_O_KC], scale=scale,
                 extra=_one_hot_row(*range(HEAD_DIM, HEAD_DIM + FOX_BIAS_TERMS)))
    _store_fox_keys(kc_o, kc, c)
    _store_heads(vc_o, vc, extra=ones_col)
    _store_heads(qd_o, p[:, _O_QD:_O_KD], scale=scale)
    _store_heads(kd_o, kd)
    _store_heads(vd_o, vd)


def _odd_weights(w_in):
    cuts = np.cumsum(ODD_SPLIT)[:-1].tolist()
    qc, kc, vc, fc, qd, kd, vd = jnp.split(w_in, cuts, axis=1)
    w = jnp.concatenate([qc, kc, vc, _pad_cols(fc, LANES), qd, kd, vd], axis=1)
    assert w.shape[1] == _O_END
    return w.astype(BF16)


def _odd_proj(x, g, w, b_f, c0, *, seq_len, tm=256):
    m, d = x.shape
    tm = min(tm, seq_len)
    assert seq_len % tm == 0
    tiles_per_seq = seq_len // tm
    row = lambda cols: pl.BlockSpec((tm, cols), lambda i: (i, 0))
    const = lambda a: pl.BlockSpec(a.shape, lambda i: (0,) * a.ndim)
    g2 = g.reshape(1, d)
    bf2 = _pad_cols(b_f.reshape(1, C_HEADS), LANES)
    wide, flat = C_HEADS * LANES, C_HEADS * HEAD_DIM
    outs = [(wide, BF16)] * 6 + [(flat, F32)] * 4 + [(C_HEADS, F32)]
    return pl.pallas_call(
        functools.partial(_odd_proj_kernel, tiles_per_seq=tiles_per_seq),
        grid=(m // tm,),
        in_specs=[row(d), const(g2), const(w), const(bf2),
                  pl.BlockSpec((1, 1, LANES), lambda i: (i // tiles_per_seq, 0, 0))],
        out_specs=[row(c) for c, _ in outs],
        out_shape=[jax.ShapeDtypeStruct((m, c), t) for c, t in outs],
        scratch_shapes=[pltpu.VMEM((8, LANES), F32)],
        compiler_params=_params("arbitrary"),
        name="odd_proj",
    )(x, g2, w, bf2, c0)


def _fox_past_kernel(k_ref, logf_ref, c0_ref, k_o, cend_o, carry_s, *, tiles_per_seq):
    logf = jnp.pad(logf_ref[...], ((0, 0), (0, LANES - C_HEADS)))
    c = _cumulative(logf, carry_s, c0_ref, pl.program_id(0) % tiles_per_seq == 0)
    _store_fox_keys(k_o, k_ref[...], c)
    cend_o[0] = c[c.shape[0] - 1:, :]


def _fox_past(k, logf, *, seq_len, tm=256):
    m = k.shape[0]
    nseq = m // seq_len
    tm = min(tm, seq_len)
    tiles_per_seq = seq_len // tm
    row = lambda cols: pl.BlockSpec((tm, cols), lambda i: (i, 0))
    per_seq = pl.BlockSpec((1, 1, LANES), lambda i: (i // tiles_per_seq, 0, 0))
    return pl.pallas_call(
        functools.partial(_fox_past_kernel, tiles_per_seq=tiles_per_seq),
        grid=(m // tm,),
        in_specs=[row(C_HEADS * HEAD_DIM), row(C_HEADS), per_seq],
        out_specs=[row(C_HEADS * LANES), per_seq],
        out_shape=[jax.ShapeDtypeStruct((m, C_HEADS * LANES), BF16),
                   jax.ShapeDtypeStruct((nseq, 1, LANES), F32)],
        scratch_shapes=[pltpu.VMEM((8, LANES), F32)],
        compiler_params=_params("arbitrary"),
        name="fox_past",
    )(k, logf, jnp.zeros((nseq, 1, LANES), F32))


def _head_lanes(parts, ones_col=False):
    bsz, s_len, heads = parts[0].shape[:3]
    parts = [p.astype(BF16) for p in parts]
    used = sum(p.shape[-1] for p in parts)
    if ones_col:
        parts.append(jnp.ones((bsz, s_len, heads, 1), BF16))
        used += 1
    parts.append(jnp.zeros((bsz, s_len, heads, LANES - used), BF16))
    return jnp.concatenate(parts, axis=-1).reshape(bsz, s_len, heads * LANES)


def _keys(past, new, sk_pad):
    rows = new if past is None else jnp.concatenate([past, new], axis=1)
    pad = sk_pad - rows.shape[1]
    return jnp.pad(rows, ((0, 0), (0, pad), (0, 0))) if pad else rows


def _tiles(s_len, past_len):
    tq = min(256, s_len)
    tk, group = (256, 4) if past_len == 0 else (128, 3)
    return tq, tk, group, _round_up(past_len + s_len, tk * group)


def _fine_group(group):
    return group // 2 if group % 2 == 0 else group


def _even_mixer(x, h_gain, tables, past, weights, g_bq, g_bkv, w_o):
    bsz, s_len, d = x.shape
    m = bsz * s_len
    past_len = 0 if past is None else past[0].shape[1]
    sk = past_len + s_len
    tq, tk, group, sk_pad = _tiles(s_len, past_len)
    outs = _even_proj(x.reshape(m, d), h_gain, weights, g_bq, g_bkv, tables, seq_len=s_len)
    qa, ka, va, qi, ki, wi, qb, kb, vb, ka_s, va_s, ki_s, lat_s, kr_s = [
        o.reshape(bsz, s_len, -1) for o in outs]
    new_rows = (ka_s.reshape(bsz, s_len, A_KV_HEADS, HEAD_DIM), va_s.reshape(bsz, s_len, A_KV_HEADS, HEAD_DIM),
                ki_s, lat_s, kr_s)
    if past is None:
        pa = (None,) * 5
    else:
        c_k, c_v, c_ki, c_lat, c_kr = past
        kb_past, vb_past = _mla_past(c_lat.reshape(bsz * past_len, KV_RANK),
                                     c_kr.reshape(bsz * past_len, B_ROPE), weights[2])
        pa = (_head_lanes([c_k]), _head_lanes([c_v], ones_col=True), c_ki.astype(BF16),
              kb_past.reshape(bsz, past_len, -1), vb_past.reshape(bsz, past_len, -1))

    out_a = _dsa_attention(qa, _keys(pa[0], ka, sk_pad), _keys(pa[1], va, sk_pad), qi, _keys(pa[2], ki, sk_pad),
                           wi, q_off=past_len, sk=sk, tq=tq, tk=tk, group=group)
    out_b = _softmax_attention(qb, _keys(pa[3], kb, sk_pad), _keys(pa[4], vb, sk_pad),
                               heads=B_HEADS, mask_kind="chunk", q_off=past_len, sk=sk, tq=tq, tk=tk,
                               group=_fine_group(group), name="mla_attention")
    y = _matmul_residual(out_a.reshape(m, -1), out_b.reshape(m, -1), w_o, x.reshape(m, d)).reshape(bsz, s_len, d)
    return y, new_rows


def _odd_mixer(x, h_gain, past, w, b_f, w_o):
    bsz, s_len, d = x.shape
    m = bsz * s_len
    past_len = 0 if past is None else past[0].shape[1]
    sk = past_len + s_len
    tq, tk, group, sk_pad = _tiles(s_len, past_len)
    if past is None:
        pa = (None,) * 4
        c0 = jnp.zeros((bsz, 1, LANES), F32)
    else:
        c_k, c_v, c_logf, d_k, d_v = past
        kc_past, c0 = _fox_past(c_k.reshape(bsz * past_len, -1), c_logf.reshape(bsz * past_len, C_HEADS),
                                seq_len=past_len)
        pa = (kc_past.reshape(bsz, past_len, -1), _head_lanes([c_v], ones_col=True),
              _head_lanes([d_k]), _head_lanes([d_v]))
    outs = _odd_proj(x.reshape(m, d), h_gain, w, b_f, c0, seq_len=s_len)
    qc, kc, vc, qd, kd, vd, kc_s, vc_s, kd_s, vd_s, logf = [o.reshape(bsz, s_len, -1) for o in outs]
    heads = lambda a: a.reshape(bsz, s_len, -1, HEAD_DIM)
    new_rows = (heads(kc_s), heads(vc_s), logf, heads(kd_s), heads(vd_s))

    out_c = _softmax_attention(qc, _keys(pa[0], kc, sk_pad), _keys(pa[1], vc, sk_pad),
                               heads=C_HEADS, mask_kind="causal", q_off=past_len, sk=sk, tq=tq, tk=tk,
                               group=_fine_group(group), name="fox_attention")
    out_d = _sb_attention(qd, _keys(pa[2], kd, sk_pad), _keys(pa[3], vd, sk_pad),
                          q_off=past_len, sk=sk, tq=tq, tk=tk)
    y = _matmul_residual(out_c.reshape(m, -1), out_d.reshape(m, -1), w_o, x.reshape(m, d)).reshape(bsz, s_len, d)
    return y, new_rows


def _final_norm_kernel(x_ref, g_ref, o_ref):
    x = x_ref[...]
    ms = jnp.mean(x * x, axis=-1, keepdims=True)
    o_ref[...] = x * lax.rsqrt(ms + EPS) * g_ref[...]


def _final_norm(x, g, *, tm=512):
    m, d = x.shape
    tm = min(tm, m)
    return pl.pallas_call(
        _final_norm_kernel,
        grid=(m // tm,),
        in_specs=[pl.BlockSpec((tm, d), lambda i: (i, 0)), pl.BlockSpec((1, d), lambda i: (0, 0))],
        out_specs=pl.BlockSpec((tm, d), lambda i: (i, 0)),
        out_shape=jax.ShapeDtypeStruct((m, d), F32),
        compiler_params=_params("parallel"),
        name="final_norm",
    )(x, g.reshape(1, d).astype(F32))


def _trunk(x, q_pos, caches, params, even_w, odd_w):
    (g_mix, g_ffn, g_final, w_in_even, g_b_q, g_b_kv, w_b_uq, w_b_ukv, w_o_even,
     w_in_odd, b_forget, w_o_odd, w_up, w_conv, b_conv, w_down) = params
    bsz, s_len, d = x.shape
    depth = g_mix.shape[0]
    even_rows = [[] for _ in range(5)]
    odd_rows = [[] for _ in range(5)]
    conv_rows = []
    rows = max(s_len, min(512, bsz * s_len))
    tables = (_rope_tables(q_pos, rows, period=HEAD_DIM, start=0, n_rot=ROT_DIM),
              _rope_tables(q_pos, rows, period=IDX_DIM, start=0, n_rot=IDX_ROT),
              _rope_tables(q_pos, rows, period=LANES, start=_B_ROPE_LANE, n_rot=B_ROPE))
    for l in range(depth):
        j = l // 2
        if l % 2 == 0:
            past = None if caches is None else tuple(c[j] for c in caches[0:5])
            x, rows = _even_mixer(x, g_mix[l], tables, past, even_w[j], g_b_q[j], g_b_kv[j], w_o_even[j])
            for lst, r in zip(even_rows, rows):
                lst.append(r)
        else:
            past = None if caches is None else tuple(c[j] for c in caches[5:10])
            x, rows = _odd_mixer(x, g_mix[l], past, odd_w[j], b_forget[j], w_o_odd[j])
            for lst, r in zip(odd_rows, rows):
                lst.append(r)
        state = jnp.zeros((bsz, CONV_W - 1, D_FF), F32) if caches is None else caches[10][l]
        y, new_buf = _conv_ffn(x.reshape(bsz * s_len, d), g_ffn[l], w_up[l], w_conv[l], b_conv[l], w_down[l],
                               state, seq_len=s_len, tm=1024)
        x = y.reshape(bsz, s_len, d)
        conv_rows.append(new_buf)
    out = _final_norm(x.reshape(bsz * s_len, d), g_final).reshape(bsz, s_len, d)
    states = [jnp.stack(r, axis=0) for r in even_rows + odd_rows] + [jnp.stack(conv_rows, axis=0)]
    return out, states


def kernel(x_prompt, x_sample, cache_a_k, cache_a_v, cache_a_idx_k, cache_b_latent, cache_b_rope,
           cache_c_k, cache_c_v, cache_c_logf, cache_d_k, cache_d_v, state_ffn_conv,
           g_mix, g_ffn, g_final, w_in_even, g_b_q, g_b_kv, w_b_uq, w_b_ukv, w_o_even,
           w_in_odd, b_forget, w_o_odd, w_up, w_conv, b_conv, w_down):
    params = (g_mix, g_ffn, g_final, w_in_even, g_b_q, g_b_kv, w_b_uq, w_b_ukv, w_o_even,
              w_in_odd, b_forget, w_o_odd, w_up, w_conv, b_conv, w_down)
    caches = (cache_a_k, cache_a_v, cache_a_idx_k, cache_b_latent, cache_b_rope,
              cache_c_k, cache_c_v, cache_c_logf, cache_d_k, cache_d_v, state_ffn_conv)
    past_len = cache_a_k.shape[2]
    pos_prompt = jnp.arange(x_prompt.shape[1], dtype=I32)
    pos_sample = past_len + jnp.arange(x_sample.shape[1], dtype=I32)
    even_w = [_even_weights(w_in_even[j], w_b_uq[j], w_b_ukv[j]) for j in range(w_in_even.shape[0])]
    odd_w = [_odd_weights(w_in_odd[j]) for j in range(w_in_odd.shape[0])]
    y_prompt, p_states = _trunk(x_prompt, pos_prompt, None, params, even_w, odd_w)
    y_sample, s_states = _trunk(x_sample, pos_sample, caches, params, even_w, odd_w)
    return (y_prompt, y_sample, *p_states, *s_states)
```

```python
import functools

import jax
import jax.numpy as jnp
import numpy as np
from jax import lax
from jax.experimental import pallas as pl
from jax.experimental.pallas import tpu as pltpu

F32 = jnp.float32
BF16 = jnp.bfloat16
I32 = jnp.int32

CHUNK = 64
ROPE_THETA = 500000.0
EPS = 1e-6
HEAD_DIM = 64
ROT_DIM = HEAD_DIM // 4
A_HEADS = 8
A_KV_HEADS = 2
IDX_HEADS = 8
IDX_DIM = 32
IDX_ROT = IDX_DIM // 4
TOPK_MAX = 256
B_HEADS = 8
Q_RANK = 256
KV_RANK = 128
B_NOPE = 64
B_ROPE = 32
B_VDIM = 64
C_HEADS = 8
D_HEADS = 8
D_FF = 2816
CONV_W = 3

EVEN_SPLIT = [A_HEADS * HEAD_DIM, A_KV_HEADS * HEAD_DIM, A_KV_HEADS * HEAD_DIM,
              IDX_HEADS * IDX_DIM, IDX_DIM, IDX_HEADS, Q_RANK, KV_RANK, B_ROPE]
ODD_SPLIT = [C_HEADS * HEAD_DIM] * 3 + [C_HEADS] + [D_HEADS * HEAD_DIM] * 3

LANES = 128
VMEM_LIMIT_BYTES = 56 * 1024 * 1024
MASKED = -1e30
INT_MIN = -2 ** 31
LOG2E = 1.4426950408889634
F32_EXP2_UNDERFLOW = -152.0

_NT = (((1,), (1,)), ((), ()))


def _params(*sem):
    return pltpu.CompilerParams(dimension_semantics=sem, vmem_limit_bytes=VMEM_LIMIT_BYTES)


def _round_up(n, m):
    return (n + m - 1) // m * m


def _chunk_of(pos):
    return jnp.right_shift(pos, CHUNK.bit_length() - 1)


def _matmul_residual_kernel(a_ref, b_ref, wa_ref, wb_ref, r_ref, o_ref):
    o_ref[...] = (r_ref[...] + jnp.dot(a_ref[...], wa_ref[...], preferred_element_type=F32)
                  + jnp.dot(b_ref[...], wb_ref[...], preferred_element_type=F32))


def _matmul_residual(a, b, w, res, *, tm=512):
    m, ka = a.shape
    kb = b.shape[1]
    n = w.shape[1]
    tm = min(tm, m)
    assert m % tm == 0 and w.shape[0] == ka + kb
    row = lambda cols: pl.BlockSpec((tm, cols), lambda i: (i, 0))
    const = lambda rows: pl.BlockSpec((rows, n), lambda i: (0, 0))
    w = w.astype(BF16)
    return pl.pallas_call(
        _matmul_residual_kernel,
        grid=(m // tm,),
        in_specs=[row(ka), row(kb), const(ka), const(kb), row(n)],
        out_specs=row(n),
        out_shape=jax.ShapeDtypeStruct((m, n), F32),
        compiler_params=_params("parallel"),
        name="matmul_residual",
    )(a, b, w[:ka], w[ka:], res)


def _conv_ffn_kernel(x_ref, g_ref, wg_ref, wu_ref, wc_ref, bc_ref, wd_ref, st_ref,
                     y_ref, ns_ref, h_s, acc_s, carry_s, *, tiles_per_seq, seqs, tm, tf):
    i = pl.program_id(0)
    c = pl.program_id(1)
    nc = pl.num_programs(1)
    cols = pl.ds(pl.multiple_of(c * tf, tf), tf)
    rows_per = tm // seqs

    @pl.when(c == 0)
    def _():
        x = x_ref[...]
        ms = jnp.mean(x * x, axis=-1, keepdims=True)
        h_s[...] = (x * lax.rsqrt(ms + EPS) * g_ref[...]).astype(BF16)
        acc_s[...] = jnp.zeros_like(acc_s)

    h = h_s[...]
    gate = jnp.dot(h, wg_ref[...], preferred_element_type=F32)
    up = jnp.dot(h, wu_ref[...], preferred_element_type=F32)

    if seqs == 1:
        @pl.when(i % tiles_per_seq == 0)
        def _():
            carry_s[c] = st_ref[0, :, cols]
        before = [carry_s[c]]
    else:
        before = [st_ref[s, :, cols] for s in range(seqs)]
    row = lax.broadcasted_iota(I32, gate.shape, 0)
    g1 = pltpu.roll(gate, 1, 0)
    g2 = pltpu.roll(gate, 2, 0)
    for s, prev in enumerate(before):
        first = s * rows_per
        g1 = jnp.where(row == first, prev[1:2, :], g1)
        g2 = jnp.where(row == first, prev[0:1, :], jnp.where(row == first + 1, prev[1:2, :], g2))
    wc = wc_ref[...]
    gc = bc_ref[...] + g2 * wc[0:1, :]
    gc = gc + g1 * wc[1:2, :]
    gc = gc + gate * wc[2:3, :]
    act = gc * jax.nn.sigmoid(gc) * up
    acc_s[...] += jnp.dot(act.astype(BF16), wd_ref[...], preferred_element_type=F32)

    for s in range(seqs):
        ns_ref[s, :, cols] = gate[(s + 1) * rows_per - 2:(s + 1) * rows_per, :]
    if seqs == 1:
        carry_s[c] = gate[tm - 2:tm, :]

    @pl.when(c == nc - 1)
    def _():
        y_ref[...] = x_ref[...] + acc_s[...]


def _conv_ffn(x, g, w_up, w_conv, b_conv, w_down, state, *, seq_len, tm, tf=256):
    m, d = x.shape
    nseq = m // seq_len
    tm = min(tm, m)
    assert m % tm == 0 and D_FF % tf == 0 and (seq_len % tm == 0 or tm % seq_len == 0)
    tiles_per_seq = max(1, seq_len // tm)
    seqs = max(1, tm // seq_len)
    nc = D_FF // tf
    wg = w_up[:, :D_FF].astype(BF16)
    wu = w_up[:, D_FF:].astype(BF16)
    state_spec = pl.BlockSpec((seqs, CONV_W - 1, D_FF), lambda i, c: (i // tiles_per_seq, 0, 0))
    y, ns = pl.pallas_call(
        functools.partial(_conv_ffn_kernel, tiles_per_seq=tiles_per_seq, seqs=seqs, tm=tm, tf=tf),
        grid=(m // tm, nc),
        in_specs=[pl.BlockSpec((tm, d), lambda i, c: (i, 0)),
                  pl.BlockSpec((1, d), lambda i, c: (0, 0)),
                  pl.BlockSpec((d, tf), lambda i, c: (0, c)),
                  pl.BlockSpec((d, tf), lambda i, c: (0, c)),
                  pl.BlockSpec((CONV_W, tf), lambda i, c: (0, c)),
                  pl.BlockSpec((1, tf), lambda i, c: (0, c)),
                  pl.BlockSpec((tf, d), lambda i, c: (c, 0)),
                  state_spec],
        out_specs=[pl.BlockSpec((tm, d), lambda i, c: (i, 0)), state_spec],
        out_shape=[jax.ShapeDtypeStruct((m, d), F32),
                   jax.ShapeDtypeStruct((nseq, CONV_W - 1, D_FF), F32)],
        scratch_shapes=[pltpu.VMEM((tm, d), BF16),
                        pltpu.VMEM((tm, d), F32),
                        pltpu.VMEM((nc, CONV_W - 1, tf), F32)],
        compiler_params=_params("arbitrary", "arbitrary"),
        name="conv_ffn",
    )(x, g.reshape(1, d).astype(F32), wg, wu, w_conv.astype(F32), b_conv.reshape(1, D_FF).astype(F32),
      w_down.astype(BF16), state.astype(F32))
    return y, ns


def _attn_call(kernel, inputs, in_specs, out_cols, *, bsz, sq, tq, scratch=(), name):
    return pl.pallas_call(
        kernel,
        grid=(bsz, sq // tq),
        in_specs=in_specs,
        out_specs=pl.BlockSpec((1, tq, out_cols), lambda b, i: (b, i, 0)),
        out_shape=jax.ShapeDtypeStruct((bsz, sq, out_cols), BF16),
        scratch_shapes=list(scratch),
        compiler_params=_params("parallel", "arbitrary"),
        name=name,
    )(*inputs)


def _q_spec(tq, cols):
    return pl.BlockSpec((1, tq, cols), lambda b, i: (b, i, 0))


def _k_spec(rows, cols):
    return pl.BlockSpec((1, rows, cols), lambda b, i: (b, 0, 0))


def _tile_geometry(i, *, tq, span, q_off):
    q_lo = q_off + i * tq
    return q_lo, q_lo // span


def _lane_tiles(x):
    return [x[:, j * LANES:(j + 1) * LANES] for j in range(x.shape[1] // LANES)]


def _visible(kind, q_lo, ks, *, tq, tk, sk):
    q_pos = q_lo + lax.broadcasted_iota(I32, (tq, 1), 0)
    k_pos = ks + lax.broadcasted_iota(I32, (1, tk), 1)
    if kind == "chunk":
        return (_chunk_of(k_pos) <= _chunk_of(q_pos)) & (k_pos < sk)
    if kind == "causal":
        return k_pos <= q_pos
    assert kind == "strict"
    return k_pos < q_pos


HEADS_PER_STEP = 2


def _span_cases(i, body, *, tq, span, q_off, n_tiles):
    g_own = (q_off + i * tq) // span
    for g in sorted({(q_off + t * tq) // span for t in range(n_tiles)}):
        pl.when(g_own == g)(functools.partial(body, g + 1))


def _softmax_heads(qs, k_ref, v_ref, s_scr, kcols, vcols, *, tq, tk, blocks, own_from, own_mask,
                   bias_ref=None):
    row_max = []
    for slot, (q, kcol) in enumerate(zip(qs, kcols)):
        mx = jnp.full((tq, LANES), MASKED, F32)
        for b in range(blocks):
            ks = b * tk
            s = lax.dot_general(q, k_ref[0, ks:ks + tk, kcol], _NT, preferred_element_type=F32)
            if bias_ref is not None:
                s = s + bias_ref[:, ks:ks + tk]
            if own_mask is not None and b >= own_from:
                s = jnp.where(own_mask(ks), s, MASKED)
            s_scr[slot, :, ks:ks + tk] = s
            for t in _lane_tiles(s):
                mx = jnp.maximum(mx, t)
        row_max.append(jnp.broadcast_to(jnp.max(mx, axis=1, keepdims=True), (tq, LANES)))
    outs = []
    for slot, (m, vcol) in enumerate(zip(row_max, vcols)):
        acc = jnp.zeros((tq, LANES), F32)
        for b in range(blocks):
            ks = b * tk
            p = jnp.concatenate([jnp.exp2(t - m) for t in _lane_tiles(s_scr[slot, :, ks:ks + tk])], axis=1)
            acc = acc + jnp.dot(p.astype(BF16), v_ref[0, ks:ks + tk, vcol], preferred_element_type=F32)
        outs.append(acc[:, :HEAD_DIM] / acc[:, HEAD_DIM:HEAD_DIM + 1])
    return outs


def _lane_tile_at(index):
    return pl.ds(pl.multiple_of(index * LANES, LANES), LANES)


def _softmax_attn_kernel(q_ref, k_ref, v_ref, o_ref, s_scr, *, heads, mask_kind, tq, tk, group, q_off, sk,
                         n_tiles):
    i = pl.program_id(1)
    own_mask = functools.partial(_visible, mask_kind, q_off + i * tq, tq=tq, tk=tk, sk=sk)

    def attend(n_spans):
        def step(hp, _):
            cols = [_lane_tile_at(hp * HEADS_PER_STEP + u) for u in range(HEADS_PER_STEP)]
            outs = _softmax_heads([q_ref[0, :, c] for c in cols], k_ref, v_ref, s_scr, cols, cols,
                                  tq=tq, tk=tk, blocks=n_spans * group, own_from=(n_spans - 1) * group,
                                  own_mask=own_mask)
            o_ref[0, :, _lane_tile_at(hp)] = jnp.concatenate(outs, axis=1).astype(o_ref.dtype)
            return 0
        lax.fori_loop(0, heads // HEADS_PER_STEP, step, 0)

    _span_cases(i, attend, tq=tq, span=tk * group, q_off=q_off, n_tiles=n_tiles)


def _check_tiling(sk_pad, *, tq, tk, group, q_off):
    assert (tk * group) % tq == 0 and q_off % tq == 0 and sk_pad % (tk * group) == 0 and tk % LANES == 0
    assert HEADS_PER_STEP * HEAD_DIM == LANES


def _softmax_attention(q, k, v, *, heads, mask_kind, q_off, sk, tq, tk, group, name):
    bsz, sq, _ = q.shape
    sk_pad = k.shape[1]
    _check_tiling(sk_pad, tq=tq, tk=tk, group=group, q_off=q_off)
    kern = functools.partial(_softmax_attn_kernel, heads=heads, mask_kind=mask_kind,
                             tq=tq, tk=tk, group=group, q_off=q_off, sk=sk, n_tiles=sq // tq)
    cols = heads * LANES
    return _attn_call(kern, (q, k, v), [_q_spec(tq, cols), _k_spec(sk_pad, cols), _k_spec(sk_pad, cols)],
                      heads * HEAD_DIM, bsz=bsz, sq=sq, tq=tq,
                      scratch=(pltpu.VMEM((HEADS_PER_STEP, tq, sk_pad), F32),), name=name)


def _split_bf16(x):
    hi = x.astype(BF16)
    lo = (x - hi.astype(F32)).astype(BF16)
    return hi, lo


SB_HEADS_PER_LOOP = 2


def _sb_kernel(q_ref, k_ref, v_ref, o_ref, *, tq, tk, q_off, sk):
    q_lo, n_before = _tile_geometry(pl.program_id(1), tq=tq, span=tk, q_off=q_off)
    r = lax.broadcasted_iota(I32, (tk, tk), 0)
    cidx = lax.broadcasted_iota(I32, (tk, tk), 1)
    later = jnp.where(r > cidx, 1.0, 0.0).astype(BF16)

    def block(q, h, kb, run, acc, mask):
        ks = pl.multiple_of(kb * tk, tk)
        k = k_ref[0, pl.ds(ks, tk), h * LANES:(h + 1) * LANES]
        v = v_ref[0, pl.ds(ks, tk), h * LANES:(h + 1) * LANES]
        z = lax.dot_general(q, k, _NT, preferred_element_type=F32)
        log_beta = jnp.minimum(z, 0.0) - jnp.log2(1.0 + jnp.exp2(-jnp.abs(z)))
        log_1m = log_beta - z
        if mask is not None:
            log_1m = jnp.where(mask, log_1m, 0.0)
        hi, lo = _split_bf16(log_1m)
        after = (jnp.dot(hi, later, preferred_element_type=F32)
                 + jnp.dot(lo, later, preferred_element_type=F32))
        a = jnp.exp2(log_beta + after + run)
        if mask is not None:
            a = jnp.where(mask, a, 0.0)
        acc = acc + jnp.dot(a.astype(BF16), v, preferred_element_type=F32)
        run = run + after[:, 0:1] + log_1m[:, 0:1]
        return run, acc

    own = _visible("strict", q_lo, n_before * tk, tq=tq, tk=tk, sk=sk)
    for h0 in range(0, D_HEADS, SB_HEADS_PER_LOOP):
        hs = range(h0, h0 + SB_HEADS_PER_LOOP)
        qs = [q_ref[0, :, h * LANES:(h + 1) * LANES] for h in hs]
        state = []
        for h, q in zip(hs, qs):
            state.extend(block(q, h, n_before, jnp.zeros((tq, 1), F32), jnp.zeros((tq, LANES), F32), own))

        def alive(state):
            top = functools.reduce(jnp.maximum, state[0::2])
            return jnp.max(top) > F32_EXP2_UNDERFLOW

        def cond(c):
            return (c[0] < n_before) & c[1]

        def body(c, hs=hs, qs=qs):
            j, _, state = c
            new = []
            for n, (h, q) in enumerate(zip(hs, qs)):
                new.extend(block(q, h, n_before - 1 - j, state[2 * n], state[2 * n + 1], None))
            return j + 1, alive(new), tuple(new)

        _, _, state = lax.while_loop(cond, body, (jnp.int32(0), alive(state), tuple(state)))
        for n, h in enumerate(hs):
            o_ref[0, :, h * HEAD_DIM:(h + 1) * HEAD_DIM] = state[2 * n + 1][:, :HEAD_DIM].astype(o_ref.dtype)


def _sb_attention(q, k, v, *, q_off, sk, tq, tk):
    bsz, sq, _ = q.shape
    sk_pad = k.shape[1]
    _check_tiling(sk_pad, tq=tq, tk=tk, group=1, q_off=q_off)
    kern = functools.partial(_sb_kernel, tq=tq, tk=tk, q_off=q_off, sk=sk)
    cols = D_HEADS * LANES
    return _attn_call(kern, (q, k, v), [_q_spec(tq, cols), _k_spec(sk_pad, cols), _k_spec(sk_pad, cols)],
                      D_HEADS * HEAD_DIM, bsz=bsz, sq=sq, tq=tq, name="sb_attention")


def _sortable_key(score):
    bits = lax.bitcast_convert_type(score, I32)
    return jnp.where(bits < 0, bits ^ 0x7FFFFFFF, bits)


KEY_OF_NEG_INF = -0x7F800001


def _dsa_kernel(q_ref, k_ref, v_ref, qi_ref, ki_ref, wi_ref, o_ref, key_s, bias_s, s_scr, w_s,
                *, tq, tk, group, q_off, sk, topk, n_tiles):
    i = pl.program_id(1)
    nt = tk // LANES
    own_visible = functools.partial(_visible, "chunk", q_off + i * tq, tq=tq, tk=tk, sk=sk)
    kf = float(topk)

    for h in range(IDX_HEADS):
        w_s[h] = jnp.broadcast_to(wi_ref[0, :, h:h + 1], (tq, LANES))

    def select_and_attend(n_spans):
        blocks, own_from = n_spans * group, (n_spans - 1) * group
        starts = [b * tk for b in range(blocks)]

        def own(ks, value, masked):
            return jnp.where(own_visible(ks), value, masked) if ks >= own_from * tk else value

        qis = [qi_ref[0, :, h * IDX_DIM:(h + 1) * IDX_DIM] for h in range(IDX_HEADS)]
        for ks in starts:
            ki = ki_ref[0, ks:ks + tk, :]
            total = jnp.zeros((tq, tk), F32)
            for h in range(IDX_HEADS):
                sc = lax.dot_general(qis[h], ki, _NT, preferred_element_type=F32)
                total = total + jnp.maximum(sc, 0.0) * jnp.concatenate([w_s[h]] * nt, axis=1)
            key_s[:, ks:ks + tk] = _sortable_key(own(ks, total, -jnp.inf))

        def count(hit):
            cnt = jnp.zeros((tq, LANES), F32)
            for ks in starts:
                for t in _lane_tiles(key_s[:, ks:ks + tk]):
                    cnt = cnt + jnp.where(hit(t), 1.0, 0.0)
            return jnp.sum(cnt, axis=1, keepdims=True)

        def count_ge(cand):
            cand = jnp.broadcast_to(cand, (tq, LANES))
            return count(lambda t: t >= cand)

        lowest = jnp.full((tq, 1), INT_MIN, I32)
        zero = jnp.zeros((tq, 1), I32)
        n_zero = count_ge(zero)
        ok = n_zero >= kf
        thr = jnp.where(ok, zero, lowest)
        n_thr = jnp.where(ok, n_zero, float(blocks * tk))

        def bit_step(it, c):
            thr, n_thr = c
            cand = thr | jnp.left_shift(jnp.int32(1), 30 - it)
            n_cand = count_ge(cand)
            ok = n_cand >= kf
            return jnp.where(ok, cand, thr), jnp.where(ok, n_cand, n_thr)

        thr, n_thr = lax.fori_loop(0, 31, bit_step, (thr, n_thr))
        thr_b = jnp.broadcast_to(thr, (tq, LANES))

        tied = jnp.max(jnp.where(thr > KEY_OF_NEG_INF, n_thr, 0.0)) > kf

        @pl.when(jnp.logical_not(tied))
        def _():
            for ks in starts:
                tiles = _lane_tiles(key_s[:, ks:ks + tk])
                bias = jnp.concatenate([jnp.where(t >= thr_b, 0.0, MASKED) for t in tiles], axis=1)
                bias_s[:, ks:ks + tk] = own(ks, bias, MASKED)

        @pl.when(tied)
        def _():
            r = lax.broadcasted_iota(I32, (tk, tk), 0)
            cidx = lax.broadcasted_iota(I32, (tk, tk), 1)
            upto = jnp.where(r <= cidx, 1.0, 0.0).astype(BF16)
            need = kf - count(lambda t: t > thr_b)
            seen = jnp.zeros((tq, 1), F32)
            for ks in starts:
                key = key_s[:, ks:ks + tk]
                tie = jnp.where(key == thr, 1.0, 0.0)
                rank = seen + jnp.dot(tie.astype(BF16), upto, preferred_element_type=F32)
                sel = (key > thr) | ((key == thr) & (rank <= need))
                bias_s[:, ks:ks + tk] = own(ks, jnp.where(sel, 0.0, MASKED), MASKED)
                seen = seen + jnp.sum(tie, axis=1, keepdims=True)

        def step(hp, _):
            heads = [hp * HEADS_PER_STEP + u for u in range(HEADS_PER_STEP)]
            kv = [_lane_tile_at(hd // (A_HEADS // A_KV_HEADS)) for hd in heads]
            outs = _softmax_heads([q_ref[0, :, _lane_tile_at(hd)] for hd in heads], k_ref, v_ref, s_scr,
                                  kv, kv, tq=tq, tk=tk, blocks=blocks, own_from=own_from, own_mask=None,
                                  bias_ref=bias_s)
            o_ref[0, :, _lane_tile_at(hp)] = jnp.concatenate(outs, axis=1).astype(o_ref.dtype)
            return 0
        lax.fori_loop(0, A_HEADS // HEADS_PER_STEP, step, 0)

    _span_cases(i, select_and_attend, tq=tq, span=tk * group, q_off=q_off, n_tiles=n_tiles)


def _dsa_attention(q, k, v, qi, ki, wi, *, q_off, sk, tq, tk, group):
    bsz, sq, _ = q.shape
    sk_pad = k.shape[1]
    _check_tiling(sk_pad, tq=tq, tk=tk, group=group, q_off=q_off)
    topk = min(TOPK_MAX, sk // 4)
    kern = functools.partial(_dsa_kernel, tq=tq, tk=tk, group=group, q_off=q_off, sk=sk, topk=topk,
                             n_tiles=sq // tq)
    kv_cols = A_KV_HEADS * LANES
    return _attn_call(kern, (q, k, v, qi, ki, wi),
                      [_q_spec(tq, A_HEADS * LANES), _k_spec(sk_pad, kv_cols), _k_spec(sk_pad, kv_cols),
                       _q_spec(tq, IDX_HEADS * IDX_DIM), _k_spec(sk_pad, IDX_DIM), _q_spec(tq, IDX_HEADS)],
                      A_HEADS * HEAD_DIM, bsz=bsz, sq=sq, tq=tq,
                      scratch=(pltpu.VMEM((tq, sk_pad), I32), pltpu.VMEM((tq, sk_pad), F32),
                               pltpu.VMEM((HEADS_PER_STEP, tq, sk_pad), F32),
                               pltpu.VMEM((IDX_HEADS, tq, LANES), F32)),
                      name="dsa_attention")


def _rmsnorm_rows(x, g):
    ms = jnp.mean(x * x, axis=-1, keepdims=True)
    return x * lax.rsqrt(ms + EPS) * g


def _lane_index(shape):
    return lax.broadcasted_iota(I32, shape, len(shape) - 1)


def _one_hot_row(*lanes):
    lane = _lane_index((1, LANES))
    hit = functools.reduce(jnp.logical_or, [lane == l for l in lanes])
    return jnp.where(hit, 1.0, 0.0)


def _rotate(tile, tables, half):
    cos, sin_up, sin_dn = tables
    return tile * cos + pltpu.roll(tile, half, 1) * sin_up + pltpu.roll(tile, LANES - half, 1) * sin_dn


def _split_pair(pair):
    low = _lane_index(pair.shape) < HEAD_DIM
    return jnp.where(low, pair, 0.0), jnp.where(low, pltpu.roll(pair, HEAD_DIM, 1), 0.0)


def _store_heads(out_ref, compact, *, scale=None, extra=None):
    for p, pair in enumerate(_lane_tiles(compact)):
        for u, tile in enumerate(_split_pair(pair)):
            if scale is not None:
                tile = tile * scale
            if extra is not None:
                tile = tile + extra
            h = 2 * p + u
            out_ref[:, h * LANES:(h + 1) * LANES] = tile.astype(out_ref.dtype)


_E_QA, _E_KA, _E_VA, _E_QI, _E_KI, _E_WI, _E_CQ, _E_CKV, _E_KR, _E_END = (
    0, 512, 640, 768, 1024, 1152, 1280, 1536, 1664, 1792)
_B_ROPE_LANE = B_NOPE


def _even_proj_kernel(x_ref, g_ref, w_ref, gq_ref, wuq_ref, gkv_ref, wukv_ref, ta_ref, ti_ref, tb_ref,
                      qa_o, ka_o, va_o, qi_o, ki_o, wi_o, qb_o, kb_o, vb_o,
                      ka_s, va_s, ki_s, lat_s, kr_s):
    h = _rmsnorm_rows(x_ref[...], g_ref[...]).astype(BF16)
    p = jnp.dot(h, w_ref[...], preferred_element_type=F32)
    ta = (ta_ref[0], ta_ref[1], ta_ref[2])
    ti = (ti_ref[0], ti_ref[1], ti_ref[2])
    tb = (tb_ref[0], tb_ref[1], tb_ref[2])
    ones_col = _one_hot_row(HEAD_DIM)

    qa = jnp.concatenate([_rotate(t, ta, ROT_DIM // 2) for t in _lane_tiles(p[:, _E_QA:_E_KA])], axis=1)
    _store_heads(qa_o, qa, scale=HEAD_DIM ** -0.5 * LOG2E)
    ka = _rotate(p[:, _E_KA:_E_VA], ta, ROT_DIM // 2)
    ka_s[...] = ka
    _store_heads(ka_o, ka)
    va = p[:, _E_VA:_E_QI]
    va_s[...] = va
    _store_heads(va_o, va, extra=ones_col)
    qi = jnp.concatenate([_rotate(t, ti, IDX_ROT // 2) for t in _lane_tiles(p[:, _E_QI:_E_KI])], axis=1)
    qi_o[...] = (qi * IDX_DIM ** -0.5).astype(qi_o.dtype)
    ki = _rotate(p[:, _E_KI:_E_WI], ti, IDX_ROT // 2)[:, :IDX_DIM]
    ki_s[...] = ki
    ki_o[...] = ki.astype(ki_o.dtype)
    wi_o[...] = p[:, _E_WI:_E_WI + IDX_HEADS] * IDX_HEADS ** -0.5

    cq = _rmsnorm_rows(p[:, _E_CQ:_E_CKV], gq_ref[...]).astype(BF16)
    qb = jnp.dot(cq, wuq_ref[...], preferred_element_type=F32)
    scale_b = (B_NOPE + B_ROPE) ** -0.5 * LOG2E
    for hd, t in enumerate(_lane_tiles(qb)):
        qb_o[:, hd * LANES:(hd + 1) * LANES] = (_rotate(t, tb, B_ROPE // 2) * scale_b).astype(qb_o.dtype)
    lat = _rmsnorm_rows(p[:, _E_CKV:_E_KR], gkv_ref[...])
    lat_s[...] = lat
    kv = jnp.dot(lat.astype(BF16), wukv_ref[...], preferred_element_type=F32)
    kr = _rotate(p[:, _E_KR:_E_END], tb, B_ROPE // 2)
    kr_s[...] = kr[:, _B_ROPE_LANE:_B_ROPE_LANE + B_ROPE]
    _store_mla_keys(kb_o, vb_o, kv, kr)


def _store_mla_keys(kb_o, vb_o, kv, kr):
    ones_col = _one_hot_row(HEAD_DIM)
    tiles = _lane_tiles(kv)
    for hd in range(B_HEADS):
        kb_o[:, hd * LANES:(hd + 1) * LANES] = (tiles[hd] + kr).astype(kb_o.dtype)
        vb_o[:, hd * LANES:(hd + 1) * LANES] = (tiles[B_HEADS + hd] + ones_col).astype(vb_o.dtype)


def _mla_past_kernel(lat_ref, kr_ref, wukv_ref, kb_o, vb_o):
    kv = jnp.dot(lat_ref[...].astype(BF16), wukv_ref[...], preferred_element_type=F32)
    rows = kv.shape[0]
    kr = jnp.concatenate([jnp.zeros((rows, _B_ROPE_LANE), F32), kr_ref[...],
                          jnp.zeros((rows, LANES - _B_ROPE_LANE - B_ROPE), F32)], axis=1)
    _store_mla_keys(kb_o, vb_o, kv, kr)


def _mla_past(lat, kr, wukv, *, tm=512):
    m = lat.shape[0]
    tm = min(tm, m)
    assert m % tm == 0
    row = lambda cols: pl.BlockSpec((tm, cols), lambda i: (i, 0))
    wide = B_HEADS * LANES
    return pl.pallas_call(
        _mla_past_kernel,
        grid=(m // tm,),
        in_specs=[row(KV_RANK), row(B_ROPE), pl.BlockSpec(wukv.shape, lambda i: (0, 0))],
        out_specs=[row(wide), row(wide)],
        out_shape=[jax.ShapeDtypeStruct((m, wide), BF16)] * 2,
        compiler_params=_params("parallel"),
        name="mla_past",
    )(lat, kr, wukv)


def _pad_cols(w, width):
    return jnp.pad(w, ((0, 0), (0, width - w.shape[1])))


def _even_weights(w_in, w_uq, w_ukv):
    cuts = np.cumsum(EVEN_SPLIT)[:-1].tolist()
    qa, ka, va, qi, ki, wi, cq, ckv, kr = jnp.split(w_in, cuts, axis=1)
    kr = jnp.pad(kr, ((0, 0), (_B_ROPE_LANE, LANES - _B_ROPE_LANE - B_ROPE)))
    w = jnp.concatenate([qa, ka, va, qi, _pad_cols(ki, LANES), _pad_cols(wi, LANES), cq, ckv, kr], axis=1)
    d = w_in.shape[0]
    wuq = jnp.pad(w_uq.reshape(Q_RANK, B_HEADS, B_NOPE + B_ROPE),
                  ((0, 0), (0, 0), (0, LANES - B_NOPE - B_ROPE))).reshape(Q_RANK, B_HEADS * LANES)
    kvw = w_ukv.reshape(KV_RANK, B_HEADS, B_NOPE + B_VDIM)
    pad = lambda a: jnp.pad(a, ((0, 0), (0, 0), (0, LANES - a.shape[2]))).reshape(KV_RANK, B_HEADS * LANES)
    wukv = jnp.concatenate([pad(kvw[:, :, :B_NOPE]), pad(kvw[:, :, B_NOPE:])], axis=1)
    assert w.shape == (d, _E_END)
    return w.astype(BF16), wuq.astype(BF16), wukv.astype(BF16)


def _rope_tables(pos, rows, *, period, start, n_rot):
    half = n_rot // 2
    inv = ROPE_THETA ** (-jnp.arange(half, dtype=F32) * 2.0 / n_rot)
    ang = pos.astype(F32)[:, None] * inv[None, :]
    cos, sin = jnp.cos(ang), jnp.sin(ang)
    off = np.arange(LANES) % period - start
    idx = np.where((off >= 0) & (off < n_rot), off % half, 0)
    lower = (off >= 0) & (off < half)
    upper = (off >= half) & (off < n_rot)
    cos_t = jnp.where(lower | upper, cos[:, idx], 1.0)
    sin_up = jnp.where(upper, sin[:, idx], 0.0)
    sin_dn = jnp.where(lower, -sin[:, idx], 0.0)
    tabs = jnp.stack([cos_t, sin_up, sin_dn])
    return jnp.tile(tabs, (1, rows // pos.shape[0], 1))


def _even_proj(x, g, weights, g_bq, g_bkv, tables, *, seq_len, tm=512):
    m, d = x.shape
    w, wuq, wukv = weights
    tm = min(tm, m)
    table_rows = tables[0].shape[1]
    assert m % tm == 0 and table_rows % tm == 0 and (seq_len % tm == 0 or tm % seq_len == 0)
    tblocks = table_rows // tm
    row = lambda cols: pl.BlockSpec((tm, cols), lambda i: (i, 0))
    const = lambda a: pl.BlockSpec(a.shape, lambda i: (0,) * a.ndim)
    tspec = pl.BlockSpec((3, tm, LANES), lambda i: (0, i % tblocks, 0))
    g2, gq2, gkv2 = g.reshape(1, d), g_bq.reshape(1, Q_RANK), g_bkv.reshape(1, KV_RANK)
    kv_cols = A_KV_HEADS * HEAD_DIM
    outs = [(A_HEADS * LANES, BF16), (A_KV_HEADS * LANES, BF16), (A_KV_HEADS * LANES, BF16),
            (IDX_HEADS * IDX_DIM, BF16), (IDX_DIM, BF16), (IDX_HEADS, F32),
            (B_HEADS * LANES, BF16), (B_HEADS * LANES, BF16), (B_HEADS * LANES, BF16),
            (kv_cols, F32), (kv_cols, F32), (IDX_DIM, F32), (KV_RANK, F32), (B_ROPE, F32)]
    return pl.pallas_call(
        _even_proj_kernel,
        grid=(m // tm,),
        in_specs=[row(d), const(g2), const(w), const(gq2), const(wuq), const(gkv2), const(wukv),
                  tspec, tspec, tspec],
        out_specs=[row(c) for c, _ in outs],
        out_shape=[jax.ShapeDtypeStruct((m, c), t) for c, t in outs],
        compiler_params=_params("parallel"),
        name="even_proj",
    )(x, g2, w, gq2, wuq, gkv2, wukv, *tables)


_O_QC, _O_KC, _O_VC, _O_F, _O_QD, _O_KD, _O_VD, _O_END = 0, 512, 1024, 1536, 1664, 2176, 2688, 3200
FOX_BIAS_TERMS = 3


def _three_terms(x):
    hi = x.astype(BF16)
    r = x - hi.astype(F32)
    mid = r.astype(BF16)
    return hi, mid, (r - mid.astype(F32)).astype(BF16)


def _cumulative(logf, carry_s, c0_ref, restart):
    tm = logf.shape[0]

    @pl.when(restart)
    def _():
        carry_s[...] = jnp.broadcast_to(c0_ref[0], carry_s.shape)

    r = lax.broadcasted_iota(I32, (tm, tm), 0)
    cidx = lax.broadcasted_iota(I32, (tm, tm), 1)
    upto = jnp.where(cidx <= r, 1.0, 0.0).astype(BF16)
    c = carry_s[0:1, :] + sum(jnp.dot(upto, t, preferred_element_type=F32) for t in _three_terms(logf))
    carry_s[...] = jnp.broadcast_to(c[tm - 1:tm, :], carry_s.shape)
    return c


def _fox_key_bias(c):
    r = lax.broadcasted_iota(I32, (LANES, C_HEADS * LANES), 0)
    cidx = lax.broadcasted_iota(I32, (LANES, C_HEADS * LANES), 1)
    placed = 0.0
    for j, term in enumerate(_three_terms(-LOG2E * c)):
        put = jnp.where((cidx == r * LANES + HEAD_DIM + j) & (r < C_HEADS), 1.0, 0.0).astype(BF16)
        placed = placed + jnp.dot(term, put, preferred_element_type=F32)
    return placed


def _store_fox_keys(out_ref, kc, c):
    bias = _lane_tiles(_fox_key_bias(c))
    for p, pair in enumerate(_lane_tiles(kc)):
        for u, tile in enumerate(_split_pair(pair)):
            h = 2 * p + u
            out_ref[:, h * LANES:(h + 1) * LANES] = (tile + bias[h]).astype(out_ref.dtype)


def _odd_proj_kernel(x_ref, g_ref, w_ref, bf_ref, c0_ref,
                     qc_o, kc_o, vc_o, qd_o, kd_o, vd_o, kc_s, vc_s, kd_s, vd_s, logf_s, carry_s,
                     *, tiles_per_seq):
    h = _rmsnorm_rows(x_ref[...], g_ref[...]).astype(BF16)
    p = jnp.dot(h, w_ref[...], preferred_element_type=F32)
    scale = HEAD_DIM ** -0.5 * LOG2E
    ones_col = _one_hot_row(HEAD_DIM)

    f = p[:, _O_F:_O_QD] + bf_ref[...]
    logf = jnp.minimum(f, 0.0) - jnp.log1p(jnp.exp(-jnp.abs(f)))
    logf = jnp.where(_lane_index(logf.shape) < C_HEADS, logf, 0.0)
    logf_s[...] = logf[:, :C_HEADS]
    c = _cumulative(logf, carry_s, c0_ref, pl.program_id(0) % tiles_per_seq == 0)

    kc, vc, kd, vd = p[:, _O_KC:_O_VC], p[:, _O_VC:_O_F], p[:, _O_KD:_O_VD], p[:, _O_VD:_O_END]
    kc_s[...], vc_s[...], kd_s[...], vd_s[...] = kc, vc, kd, vd
    _store_heads(qc_o, p[:, _O_QC:_O_KC], scale=scale,
                 extra=_one_hot_row(*range(HEAD_DIM, HEAD_DIM + FOX_BIAS_TERMS)))
    _store_fox_keys(kc_o, kc, c)
    _store_heads(vc_o, vc, extra=ones_col)
    _store_heads(qd_o, p[:, _O_QD:_O_KD], scale=scale)
    _store_heads(kd_o, kd)
    _store_heads(vd_o, vd)


def _odd_weights(w_in):
    cuts = np.cumsum(ODD_SPLIT)[:-1].tolist()
    qc, kc, vc, fc, qd, kd, vd = jnp.split(w_in, cuts, axis=1)
    w = jnp.concatenate([qc, kc, vc, _pad_cols(fc, LANES), qd, kd, vd], axis=1)
    assert w.shape[1] == _O_END
    return w.astype(BF16)


def _odd_proj(x, g, w, b_f, c0, *, seq_len, tm=256):
    m, d = x.shape
    tm = min(tm, seq_len)
    assert seq_len % tm == 0
    tiles_per_seq = seq_len // tm
    row = lambda cols: pl.BlockSpec((tm, cols), lambda i: (i, 0))
    const = lambda a: pl.BlockSpec(a.shape, lambda i: (0,) * a.ndim)
    g2 = g.reshape(1, d)
    bf2 = _pad_cols(b_f.reshape(1, C_HEADS), LANES)
    wide, flat = C_HEADS * LANES, C_HEADS * HEAD_DIM
    outs = [(wide, BF16)] * 6 + [(flat, F32)] * 4 + [(C_HEADS, F32)]
    return pl.pallas_call(
        functools.partial(_odd_proj_kernel, tiles_per_seq=tiles_per_seq),
        grid=(m // tm,),
        in_specs=[row(d), const(g2), const(w), const(bf2),
                  pl.BlockSpec((1, 1, LANES), lambda i: (i // tiles_per_seq, 0, 0))],
        out_specs=[row(c) for c, _ in outs],
        out_shape=[jax.ShapeDtypeStruct((m, c), t) for c, t in outs],
        scratch_shapes=[pltpu.VMEM((8, LANES), F32)],
        compiler_params=_params("arbitrary"),
        name="odd_proj",
    )(x, g2, w, bf2, c0)


def _fox_past_kernel(k_ref, logf_ref, c0_ref, k_o, cend_o, carry_s, *, tiles_per_seq):
    logf = jnp.pad(logf_ref[...], ((0, 0), (0, LANES - C_HEADS)))
    c = _cumulative(logf, carry_s, c0_ref, pl.program_id(0) % tiles_per_seq == 0)
    _store_fox_keys(k_o, k_ref[...], c)
    cend_o[0] = c[c.shape[0] - 1:, :]


def _fox_past(k, logf, *, seq_len, tm=256):
    m = k.shape[0]
    nseq = m // seq_len
    tm = min(tm, seq_len)
    tiles_per_seq = seq_len // tm
    row = lambda cols: pl.BlockSpec((tm, cols), lambda i: (i, 0))
    per_seq = pl.BlockSpec((1, 1, LANES), lambda i: (i // tiles_per_seq, 0, 0))
    return pl.pallas_call(
        functools.partial(_fox_past_kernel, tiles_per_seq=tiles_per_seq),
        grid=(m // tm,),
        in_specs=[row(C_HEADS * HEAD_DIM), row(C_HEADS), per_seq],
        out_specs=[row(C_HEADS * LANES), per_seq],
        out_shape=[jax.ShapeDtypeStruct((m, C_HEADS * LANES), BF16),
                   jax.ShapeDtypeStruct((nseq, 1, LANES), F32)],
        scratch_shapes=[pltpu.VMEM((8, LANES), F32)],
        compiler_params=_params("arbitrary"),
        name="fox_past",
    )(k, logf, jnp.zeros((nseq, 1, LANES), F32))


def _head_lanes(parts, ones_col=False):
    bsz, s_len, heads = parts[0].shape[:3]
    parts = [p.astype(BF16) for p in parts]
    used = sum(p.shape[-1] for p in parts)
    if ones_col:
        parts.append(jnp.ones((bsz, s_len, heads, 1), BF16))
        used += 1
    parts.append(jnp.zeros((bsz, s_len, heads, LANES - used), BF16))
    return jnp.concatenate(parts, axis=-1).reshape(bsz, s_len, heads * LANES)


def _keys(past, new, sk_pad):
    rows = new if past is None else jnp.concatenate([past, new], axis=1)
    pad = sk_pad - rows.shape[1]
    return jnp.pad(rows, ((0, 0), (0, pad), (0, 0))) if pad else rows


def _tiles(s_len, past_len):
    tq = min(256, s_len)
    tk, group = (256, 4) if past_len == 0 else (128, 3)
    return tq, tk, group, _round_up(past_len + s_len, tk * group)


def _fine_group(group):
    return group // 2 if group % 2 == 0 else group


def _even_mixer(x, h_gain, tables, past, weights, g_bq, g_bkv, w_o):
    bsz, s_len, d = x.shape
    m = bsz * s_len
    past_len = 0 if past is None else past[0].shape[1]
    sk = past_len + s_len
    tq, tk, group, sk_pad = _tiles(s_len, past_len)
    outs = _even_proj(x.reshape(m, d), h_gain, weights, g_bq, g_bkv, tables, seq_len=s_len)
    qa, ka, va, qi, ki, wi, qb, kb, vb, ka_s, va_s, ki_s, lat_s, kr_s = [
        o.reshape(bsz, s_len, -1) for o in outs]
    new_rows = (ka_s.reshape(bsz, s_len, A_KV_HEADS, HEAD_DIM), va_s.reshape(bsz, s_len, A_KV_HEADS, HEAD_DIM),
                ki_s, lat_s, kr_s)
    if past is None:
        pa = (None,) * 5
    else:
        c_k, c_v, c_ki, c_lat, c_kr = past
        kb_past, vb_past = _mla_past(c_lat.reshape(bsz * past_len, KV_RANK),
                                     c_kr.reshape(bsz * past_len, B_ROPE), weights[2])
        pa = (_head_lanes([c_k]), _head_lanes([c_v], ones_col=True), c_ki.astype(BF16),
              kb_past.reshape(bsz, past_len, -1), vb_past.reshape(bsz, past_len, -1))

    out_a = _dsa_attention(qa, _keys(pa[0], ka, sk_pad), _keys(pa[1], va, sk_pad), qi, _keys(pa[2], ki, sk_pad),
                           wi, q_off=past_len, sk=sk, tq=tq, tk=tk, group=group)
    out_b = _softmax_attention(qb, _keys(pa[3], kb, sk_pad), _keys(pa[4], vb, sk_pad),
                               heads=B_HEADS, mask_kind="chunk", q_off=past_len, sk=sk, tq=tq, tk=tk,
                               group=_fine_group(group), name="mla_attention")
    y = _matmul_residual(out_a.reshape(m, -1), out_b.reshape(m, -1), w_o, x.reshape(m, d)).reshape(bsz, s_len, d)
    return y, new_rows


def _odd_mixer(x, h_gain, past, w, b_f, w_o):
    bsz, s_len, d = x.shape
    m = bsz * s_len
    past_len = 0 if past is None else past[0].shape[1]
    sk = past_len + s_len
    tq, tk, group, sk_pad = _tiles(s_len, past_len)
    if past is None:
        pa = (None,) * 4
        c0 = jnp.zeros((bsz, 1, LANES), F32)
    else:
        c_k, c_v, c_logf, d_k, d_v = past
        kc_past, c0 = _fox_past(c_k.reshape(bsz * past_len, -1), c_logf.reshape(bsz * past_len, C_HEADS),
                                seq_len=past_len)
        pa = (kc_past.reshape(bsz, past_len, -1), _head_lanes([c_v], ones_col=True),
              _head_lanes([d_k]), _head_lanes([d_v]))
    outs = _odd_proj(x.reshape(m, d), h_gain, w, b_f, c0, seq_len=s_len)
    qc, kc, vc, qd, kd, vd, kc_s, vc_s, kd_s, vd_s, logf = [o.reshape(bsz, s_len, -1) for o in outs]
    heads = lambda a: a.reshape(bsz, s_len, -1, HEAD_DIM)
    new_rows = (heads(kc_s), heads(vc_s), logf, heads(kd_s), heads(vd_s))

    out_c = _softmax_attention(qc, _keys(pa[0], kc, sk_pad), _keys(pa[1], vc, sk_pad),
                               heads=C_HEADS, mask_kind="causal", q_off=past_len, sk=sk, tq=tq, tk=tk,
                               group=_fine_group(group), name="fox_attention")
    out_d = _sb_attention(qd, _keys(pa[2], kd, sk_pad), _keys(pa[3], vd, sk_pad),
                          q_off=past_len, sk=sk, tq=tq, tk=tk)
    y = _matmul_residual(out_c.reshape(m, -1), out_d.reshape(m, -1), w_o, x.reshape(m, d)).reshape(bsz, s_len, d)
    return y, new_rows


def _final_norm_kernel(x_ref, g_ref, o_ref):
    x = x_ref[...]
    ms = jnp.mean(x * x, axis=-1, keepdims=True)
    o_ref[...] = x * lax.rsqrt(ms + EPS) * g_ref[...]


def _final_norm(x, g, *, tm=512):
    m, d = x.shape
    tm = min(tm, m)
    return pl.pallas_call(
        _final_norm_kernel,
        grid=(m // tm,),
        in_specs=[pl.BlockSpec((tm, d), lambda i: (i, 0)), pl.BlockSpec((1, d), lambda i: (0, 0))],
        out_specs=pl.BlockSpec((tm, d), lambda i: (i, 0)),
        out_shape=jax.ShapeDtypeStruct((m, d), F32),
        compiler_params=_params("parallel"),
        name="final_norm",
    )(x, g.reshape(1, d).astype(F32))


def _trunk(x, q_pos, caches, params, even_w, odd_w):
    (g_mix, g_ffn, g_final, w_in_even, g_b_q, g_b_kv, w_b_uq, w_b_ukv, w_o_even,
     w_in_odd, b_forget, w_o_odd, w_up, w_conv, b_conv, w_down) = params
    bsz, s_len, d = x.shape
    depth = g_mix.shape[0]
    even_rows = [[] for _ in range(5)]
    odd_rows = [[] for _ in range(5)]
    conv_rows = []
    rows = max(s_len, min(512, bsz * s_len))
    tables = (_rope_tables(q_pos, rows, period=HEAD_DIM, start=0, n_rot=ROT_DIM),
              _rope_tables(q_pos, rows, period=IDX_DIM, start=0, n_rot=IDX_ROT),
              _rope_tables(q_pos, rows, period=LANES, start=_B_ROPE_LANE, n_rot=B_ROPE))
    for l in range(depth):
        j = l // 2
        if l % 2 == 0:
            past = None if caches is None else tuple(c[j] for c in caches[0:5])
            x, rows = _even_mixer(x, g_mix[l], tables, past, even_w[j], g_b_q[j], g_b_kv[j], w_o_even[j])
            for lst, r in zip(even_rows, rows):
                lst.append(r)
        else:
            past = None if caches is None else tuple(c[j] for c in caches[5:10])
            x, rows = _odd_mixer(x, g_mix[l], past, odd_w[j], b_forget[j], w_o_odd[j])
            for lst, r in zip(odd_rows, rows):
                lst.append(r)
        state = jnp.zeros((bsz, CONV_W - 1, D_FF), F32) if caches is None else caches[10][l]
        y, new_buf = _conv_ffn(x.reshape(bsz * s_len, d), g_ffn[l], w_up[l], w_conv[l], b_conv[l], w_down[l],
                               state, seq_len=s_len, tm=1024)
        x = y.reshape(bsz, s_len, d)
        conv_rows.append(new_buf)
    out = _final_norm(x.reshape(bsz * s_len, d), g_final).reshape(bsz, s_len, d)
    states = [jnp.stack(r, axis=0) for r in even_rows + odd_rows] + [jnp.stack(conv_rows, axis=0)]
    return out, states


def kernel(x_prompt, x_sample, cache_a_k, cache_a_v, cache_a_idx_k, cache_b_latent, cache_b_rope,
           cache_c_k, cache_c_v, cache_c_logf, cache_d_k, cache_d_v, state_ffn_conv,
           g_mix, g_ffn, g_final, w_in_even, g_b_q, g_b_kv, w_b_uq, w_b_ukv, w_o_even,
           w_in_odd, b_forget, w_o_odd, w_up, w_conv, b_conv, w_down):
    params = (g_mix, g_ffn, g_final, w_in_even, g_b_q, g_b_kv, w_b_uq, w_b_ukv, w_o_even,
              w_in_odd, b_forget, w_o_odd, w_up, w_conv, b_conv, w_down)
    caches = (cache_a_k, cache_a_v, cache_a_idx_k, cache_b_latent, cache_b_rope,
              cache_c_k, cache_c_v, cache_c_logf, cache_d_k, cache_d_v, state_ffn_conv)
    past_len = cache_a_k.shape[2]
    pos_prompt = jnp.arange(x_prompt.shape[1], dtype=I32)
    pos_sample = past_len + jnp.arange(x_sample.shape[1], dtype=I32)
    even_w = [_even_weights(w_in_even[j], w_b_uq[j], w_b_ukv[j]) for j in range(w_in_even.shape[0])]
    odd_w = [_odd_weights(w_in_odd[j]) for j in range(w_in_odd.shape[0])]
    y_prompt, p_states = _trunk(x_prompt, pos_prompt, None, params, even_w, odd_w)
    y_sample, s_states = _trunk(x_sample, pos_sample, caches, params, even_w, odd_w)
    return (y_prompt, y_sample, *p_states, *s_states)
```
